```python
import jax, jax.numpy as jnp
from jax import lax
import numpy as np

D_MODEL = 2048
BATCH = 4
SEQ = 2048
DEPTH = 4

GRID_W = 64
CTX_LEN = 256
N_MIXERS = 2
EPS = 1e-6
ROPE_BASE = 10000.0
GLA_HEADS = 4
GLA_QK_DIM = D_MODEL // 2
GLA_DK = GLA_QK_DIM // GLA_HEADS
GLA_DV = D_MODEL // GLA_HEADS
GLA_GATE_RANK = 16
GLA_TAU = 16.0
GLA_CHUNK = 64
GLA_SPLITS = (GLA_QK_DIM, 2 * GLA_QK_DIM, 2 * GLA_QK_DIM + D_MODEL, 2 * GLA_QK_DIM + 2 * D_MODEL, 2 * GLA_QK_DIM + 2 * D_MODEL + GLA_GATE_RANK)
GLA_IN_COLS = 2 * GLA_QK_DIM + 2 * D_MODEL + 2 * GLA_GATE_RANK
NAT_HEADS = 16
NAT_DH = D_MODEL // NAT_HEADS
WIN_R = 8
WIN_C = 16
Q_BLOCK_C = 16
KV_BLOCK_C = 32
D_FF = 5632
N_EXPERTS = 8
TOP_K = 2
D_FF_EXPERT = 5632

kernel_name = 'hybrid_gla_natten_moe_dit'


def rmsnorm(x, g):
    x32 = x.astype(jnp.float32)
    y = x32 * lax.rsqrt(jnp.mean(x32 * x32, axis=-1, keepdims=True) + EPS)
    return (y * g.astype(jnp.float32)).astype(x.dtype)


def modulate(x, g, shift, scale):
    return rmsnorm(x, g) * (1 + scale) + shift


def ada_modulation(cvec, w, b):
    m = jax.nn.silu(cvec) @ w + b
    m = m.reshape(-1, 1, m.shape[-1])
    return jnp.split(m, 6, axis=-1)


def split_heads(a, n_heads):
    b, n, d = a.shape
    return a.reshape(b, n, n_heads, d // n_heads).transpose(0, 2, 1, 3)


def merge_heads(a):
    b, h, n, d = a.shape
    return a.transpose(0, 2, 1, 3).reshape(b, n, h * d)


def axial_rope(x):
    n, dim = x.shape[-2], x.shape[-1]
    half = dim // 2
    freqs = ROPE_BASE ** (-jnp.arange(0, half, 2, dtype=jnp.float32) / half)
    t = jnp.arange(n, dtype=jnp.int32)
    pos = jnp.stack([t // GRID_W, t % GRID_W], axis=-1).astype(jnp.float32)
    ang = pos[:, :, None] * freqs
    cos, sin = jnp.cos(ang), jnp.sin(ang)
    xs = x.astype(jnp.float32).reshape(x.shape[:-1] + (2, half))
    x1, x2 = xs[..., : half // 2], xs[..., half // 2:]
    rot = jnp.concatenate([x1 * cos - x2 * sin, x1 * sin + x2 * cos], axis=-1)
    return rot.reshape(x.shape).astype(x.dtype)


def gla_chunked(q, k, v, g, s0):
    b_, h_, n, _ = q.shape
    nc = n // GLA_CHUNK

    def to_chunks(a):
        return jnp.moveaxis(a.reshape(b_, h_, nc, GLA_CHUNK, a.shape[-1]), 2, 0)

    causal = jnp.tril(jnp.ones((GLA_CHUNK, GLA_CHUNK), dtype=bool))

    def step(s, inp):
        qc, kc, vc, gc = inp
        bcum = jnp.cumsum(gc, axis=2)
        o_inter = jnp.einsum('bhik,bhkv->bhiv', qc * jnp.exp(bcum), s)
        diff = bcum[:, :, :, None, :] - bcum[:, :, None, :, :]
        decay = jnp.exp(jnp.where(causal[:, :, None], diff, -jnp.inf))
        a = jnp.einsum('bhik,bhjk,bhijk->bhij', qc, kc, decay)
        o = o_inter + jnp.einsum('bhij,bhjv->bhiv', a, vc)
        b_last = bcum[:, :, -1:, :]
        s_new = jnp.exp(b_last[:, :, 0, :, None]) * s + jnp.einsum('bhjk,bhjv->bhkv', kc * jnp.exp(b_last - bcum), vc)
        return s_new, o

    s_fin, o = lax.scan(step, s0, (to_chunks(q), to_chunks(k), to_chunks(v), to_chunks(g)))
    o = jnp.moveaxis(o, 0, 2).reshape(b_, h_, n, v.shape[-1])
    return o, s_fin


def gla_final_state(k, v, g):
    bcum = jnp.cumsum(g, axis=2)
    return jnp.einsum('bhjk,bhjv->bhkv', k * jnp.exp(bcum[:, :, -1:] - bcum), v)


def gla_mixer(h, hc, w_in, wg_f, bg_f, wg_b, bg_b, norm_g, w_out, need_ctx_out):
    def project(a, rotary):
        q, k, v, r, zf, zb = jnp.split(a @ w_in, GLA_SPLITS, axis=-1)
        q = split_heads(q, GLA_HEADS) * GLA_DK ** -0.5
        k = split_heads(k, GLA_HEADS)
        if rotary:
            q, k = axial_rope(q), axial_rope(k)
        v = split_heads(v, GLA_HEADS)
        gf = split_heads(jax.nn.log_sigmoid((zf @ wg_f + bg_f).astype(jnp.float32)) / GLA_TAU, GLA_HEADS)
        gb = split_heads(jax.nn.log_sigmoid((zb @ wg_b + bg_b).astype(jnp.float32)) / GLA_TAU, GLA_HEADS)
        return q, k, v, r, gf, gb

    def readout(o, r):
        o = merge_heads(rmsnorm(o, norm_g))
        return ((o * jax.nn.silu(r)) @ w_out).astype(h.dtype)

    def rev(a):
        return jnp.flip(a, axis=2)

    q, k, v, r, gf, gb = project(h, True)
    qc, kc, vc, rc, gfc, gbc = project(hc, False)
    if need_ctx_out:
        zeros = jnp.zeros((h.shape[0], GLA_HEADS, GLA_DK, GLA_DV), jnp.float32)
        oc_f, s_f = gla_chunked(qc, kc, vc, gfc, zeros)
        oc_b, s_b = gla_chunked(rev(qc), rev(kc), rev(vc), rev(gbc), zeros)
        y_ctx = readout(oc_f + rev(oc_b), rc)
    else:
        s_f = gla_final_state(kc, vc, gfc)
        s_b = gla_final_state(rev(kc), rev(vc), rev(gbc))
        y_ctx = None
    o_f, _ = gla_chunked(q, k, v, gf, s_f)
    o_b, _ = gla_chunked(rev(q), rev(k), rev(v), rev(gb), s_b)
    return readout(o_f + rev(o_b), r), y_ctx


def nat_mixer(h, hc, w_in, rpb, w_out, need_ctx_out):
    b_, s_, _ = h.shape
    rows = s_ // GRID_W
    kr = min(WIN_R, rows)
    q, k, v = [split_heads(a, NAT_HEADS) for a in jnp.split(h @ w_in, 3, axis=-1)]
    qc, kc, vc = [split_heads(a, NAT_HEADS) for a in jnp.split(hc @ w_in, 3, axis=-1)]
    scale = NAT_DH ** -0.5
    qg = q.reshape(b_, NAT_HEADS, rows, GRID_W, NAT_DH) * scale
    kg = k.reshape(b_, NAT_HEADS, rows, GRID_W, NAT_DH)
    vg = v.reshape(b_, NAT_HEADS, rows, GRID_W, NAT_DH)
    n_cb = GRID_W // Q_BLOCK_C
    cb_start = np.clip(np.arange(n_cb) * Q_BLOCK_C - (KV_BLOCK_C - Q_BLOCK_C) // 2, 0, GRID_W - KV_BLOCK_C)
    key_cols = cb_start[:, None] + np.arange(KV_BLOCK_C)
    q_cols = np.arange(GRID_W).reshape(n_cb, Q_BLOCK_C)
    col_start = np.clip(q_cols - WIN_C // 2, 0, GRID_W - WIN_C)
    kcol = key_cols[:, None, :]
    in_win = (kcol >= col_start[..., None]) & (kcol < col_start[..., None] + WIN_C)
    dc_idx = np.clip(kcol - q_cols[..., None] + WIN_C - 1, 0, 2 * WIN_C - 2)
    n_lat = kr * KV_BLOCK_C

    def row_block(r):
        rs = jnp.clip(r - kr // 2, 0, rows - kr)
        k_blk = lax.dynamic_slice_in_dim(kg, rs, kr, axis=2)[:, :, :, key_cols]
        v_blk = lax.dynamic_slice_in_dim(vg, rs, kr, axis=2)[:, :, :, key_cols]
        q_blk = lax.dynamic_index_in_dim(qg, r, axis=2, keepdims=False).reshape(b_, NAT_HEADS, n_cb, Q_BLOCK_C, NAT_DH)
        s_lat = jnp.einsum('bhjqd,bhrjcd->bhjqrc', q_blk, k_blk).astype(jnp.float32)
        dr_idx = rs + jnp.arange(kr) - r + (WIN_R - 1)
        bias = jnp.transpose(rpb[:, dr_idx][:, :, dc_idx], (0, 2, 3, 1, 4))
        s_lat = jnp.where(in_win[:, :, None, :], s_lat + bias, -jnp.inf)
        s_ctx = jnp.einsum('bhjqd,bhld->bhjql', q_blk, kc).astype(jnp.float32)
        s_all = jnp.concatenate([s_lat.reshape(b_, NAT_HEADS, n_cb, Q_BLOCK_C, n_lat), s_ctx], axis=-1)
        p = jax.nn.softmax(s_all, axis=-1).astype(v.dtype)
        p_lat = p[..., :n_lat].reshape(b_, NAT_HEADS, n_cb, Q_BLOCK_C, kr, KV_BLOCK_C)
        o = jnp.einsum('bhjqrc,bhrjcd->bhjqd', p_lat, v_blk) + jnp.einsum('bhjql,bhld->bhjqd', p[..., n_lat:], vc)
        return o.reshape(b_, NAT_HEADS, GRID_W, NAT_DH)

    o = lax.map(row_block, jnp.arange(rows))
    o = jnp.transpose(o, (1, 0, 3, 2, 4)).reshape(b_, s_, D_MODEL)
    y = (o @ w_out).astype(h.dtype)
    y_ctx = None
    if need_ctx_out:
        sc = jnp.einsum('bhqd,bhld->bhql', qc * scale, kc).astype(jnp.float32)
        oc = jnp.einsum('bhql,bhld->bhqd', jax.nn.softmax(sc, axis=-1).astype(vc.dtype), vc)
        y_ctx = (merge_heads(oc) @ w_out).astype(hc.dtype)
    return y, y_ctx


def swiglu(t, wg, wu, wd):
    return (jax.nn.silu(t @ wg) * (t @ wu)) @ wd


def moe_swiglu(h, router, wg, wu, wd):
    shp = h.shape
    t = h.reshape(-1, shp[-1])
    logits = (t @ router).astype(jnp.float32)
    top_v, top_i = lax.top_k(logits, TOP_K)
    w = jax.nn.softmax(top_v, axis=-1)
    gates = jnp.sum(jax.nn.one_hot(top_i, N_EXPERTS, dtype=jnp.float32) * w[..., None], axis=1)
    out = jnp.zeros_like(t)
    for e in range(N_EXPERTS):
        out = out + gates[:, e:e + 1].astype(t.dtype) * swiglu(t, wg[e], wu[e], wd[e])
    return out.reshape(shp)


def setup_inputs(seed: int = 0) -> dict:
    key = jax.random.key(seed)
    ks = iter(jax.random.split(key, 32))
    d = D_MODEL
    n_even = (DEPTH + 1) // 2
    n_odd = DEPTH // 2

    def nrm(shape, scale):
        return jax.random.normal(next(ks), shape, jnp.float32) * scale

    return {
        'x': nrm((BATCH, SEQ, d), 1.0),
        'c': nrm((BATCH, d), 1.0),
        'ctx': nrm((BATCH, CTX_LEN, d), 1.0),
        'c_ctx': nrm((d,), 1.0),
        'ada_w': nrm((DEPTH, d, 6 * d), 0.5 * d ** -0.5),
        'ada_b': nrm((DEPTH, 6 * d), 0.02),
        'norm_mix_g': 1.0 + nrm((DEPTH, d), 0.02),
        'norm_ffn_g': 1.0 + nrm((DEPTH, d), 0.02),
        'gla_w_in': nrm((n_even, d, GLA_IN_COLS), d ** -0.5),
        'gla_wg_fwd': nrm((n_even, GLA_GATE_RANK, GLA_QK_DIM), GLA_GATE_RANK ** -0.5),
        'gla_bg_fwd': nrm((n_even, GLA_QK_DIM), 0.1),
        'gla_wg_bwd': nrm((n_even, GLA_GATE_RANK, GLA_QK_DIM), GLA_GATE_RANK ** -0.5),
        'gla_bg_bwd': nrm((n_even, GLA_QK_DIM), 0.1),
        'gla_norm_g': 1.0 + nrm((n_even, GLA_DV), 0.02),
        'gla_w_out': nrm((n_even, d, d), d ** -0.5),
        'nat_w_in': nrm((n_odd, d, 3 * d), d ** -0.5),
        'nat_rpb': nrm((n_odd, NAT_HEADS, 2 * WIN_R - 1, 2 * WIN_C - 1), 0.1),
        'nat_w_out': nrm((n_odd, d, d), d ** -0.5),
        'ffn_w_gate': nrm((n_even, d, D_FF), d ** -0.5),
        'ffn_w_up': nrm((n_even, d, D_FF), d ** -0.5),
        'ffn_w_down': nrm((n_even, D_FF, d), D_FF ** -0.5),
        'moe_router': nrm((n_odd, d, N_EXPERTS), d ** -0.5),
        'moe_w_gate': nrm((n_odd, N_EXPERTS, d, D_FF_EXPERT), d ** -0.5),
        'moe_w_up': nrm((n_odd, N_EXPERTS, d, D_FF_EXPERT), d ** -0.5),
        'moe_w_down': nrm((n_odd, N_EXPERTS, D_FF_EXPERT, d), D_FF_EXPERT ** -0.5),
        'final_norm_g': 1.0 + nrm((d,), 0.02),
    }


def reference(x, c, ctx, c_ctx, ada_w, ada_b, norm_mix_g, norm_ffn_g, gla_w_in, gla_wg_fwd, gla_bg_fwd, gla_wg_bwd, gla_bg_bwd, gla_norm_g, gla_w_out, nat_w_in, nat_rpb, nat_w_out, ffn_w_gate, ffn_w_up, ffn_w_down, moe_router, moe_w_gate, moe_w_up, moe_w_down, final_norm_g):
    seq = x.shape[1]
    xc = ctx
    for i in range(DEPTH):
        j = i // N_MIXERS
        last = i == DEPTH - 1
        sh1, sc1, g1, sh2, sc2, g2 = ada_modulation(c, ada_w[i], ada_b[i])
        csh1, csc1, cg1, csh2, csc2, cg2 = ada_modulation(c_ctx, ada_w[i], ada_b[i])
        h = modulate(x, norm_mix_g[i], sh1, sc1)
        hc = modulate(xc, norm_mix_g[i], csh1, csc1)
        if i % N_MIXERS == 0:
            y, yc = gla_mixer(h, hc, gla_w_in[j], gla_wg_fwd[j], gla_bg_fwd[j], gla_wg_bwd[j], gla_bg_bwd[j], gla_norm_g[j], gla_w_out[j], not last)
        else:
            y, yc = nat_mixer(h, hc, nat_w_in[j], nat_rpb[j], nat_w_out[j], not last)
        x = x + g1 * y
        h2 = modulate(x, norm_ffn_g[i], sh2, sc2)
        if not last:
            xc = xc + cg1 * yc
            h2 = jnp.concatenate([modulate(xc, norm_ffn_g[i], csh2, csc2), h2], axis=1)
        if i % 2 == 0:
            f = swiglu(h2, ffn_w_gate[j], ffn_w_up[j], ffn_w_down[j])
        else:
            f = moe_swiglu(h2, moe_router[j], moe_w_gate[j], moe_w_up[j], moe_w_down[j])
        x = x + g2 * f[:, -seq:]
        if not last:
            xc = xc + cg2 * f[:, : xc.shape[1]]
    return rmsnorm(x, final_norm_g)
```

```python
import functools

import numpy as np
import jax
import jax.numpy as jnp
from jax import lax
from jax.experimental import pallas as pl
from jax.experimental.pallas import tpu as pltpu

F32 = jnp.float32
BF16 = jnp.bfloat16
HIGHEST = lax.Precision.HIGHEST

D = 2048
BATCH = 4
SEQ = 2048
CTX = 256
ROWS = CTX + SEQ
T = BATCH * ROWS
DEPTH = 4
GRID_W = 64
GRID_H = SEQ // GRID_W
EPS = 1e-6
ROPE_BASE = 10000.0
CTX_GROUP = BATCH
N_MOD = 8

GLA_H = 4
GLA_DK = 256
GLA_DV = 512
GLA_QK = GLA_H * GLA_DK
GLA_RANK = 16
GLA_TAU = 16.0
GLA_C = 64
GLA_SUB = 16

NAT_H = 16
NAT_DH = 128
WIN_R = 8
WIN_C = 16

D_FF = 5632
N_EXP = 8

LANES = 128
VMEM_LIMIT = 56 * 1024 * 1024

TM_MOD = 256
TM_MOE = 512
TF_MOE = 512
TN_MOE_DOWN = 512
TG = 256


def _params(*sem):
    return pltpu.CompilerParams(dimension_semantics=sem, vmem_limit_bytes=VMEM_LIMIT)


def _dot(a, b):
    return jnp.dot(a, b, preferred_element_type=F32)


def _dot_nt(a, b):
    return lax.dot_general(a, b, (((1,), (1,)), ((), ())), preferred_element_type=F32)


def _dot_tn(a, b):
    return lax.dot_general(a, b, (((0,), (0,)), ((), ())), preferred_element_type=F32)


def _silu(v):
    return v * jax.nn.sigmoid(v)


def _row_gate(tab_ref, m, tm):
    start = m * tm
    b = start // ROWS
    rows = start % ROWS + lax.broadcasted_iota(jnp.int32, (tm, 1), 0)
    per_batch = tab_ref[pl.ds(b, 1), :]
    per_ctx = tab_ref[CTX_GROUP:CTX_GROUP + 1, :]
    return jnp.where(rows < CTX, per_ctx, per_batch)


def _ada_kernel(c_ref, w_ref, b_ref, o_ref):
    a = _silu(c_ref[...]).astype(BF16)
    o_ref[...] = _dot(a, w_ref[...].astype(BF16)) + b_ref[...]


def ada_tables(c8, ada_w, ada_b):
    tn = 1024
    return pl.pallas_call(
        _ada_kernel,
        grid=(DEPTH, 6 * D // tn),
        in_specs=[
            pl.BlockSpec((N_MOD, D), lambda l, n: (0, 0)),
            pl.BlockSpec((None, D, tn), lambda l, n: (l, 0, n)),
            pl.BlockSpec((None, 1, tn), lambda l, n: (l, 0, n)),
        ],
        out_specs=pl.BlockSpec((None, N_MOD, tn), lambda l, n: (l, 0, n)),
        out_shape=jax.ShapeDtypeStruct((DEPTH, N_MOD, 6 * D), F32),
        compiler_params=_params("arbitrary", "arbitrary"),
        name="ada_tables",
    )(c8, ada_w, ada_b.reshape(DEPTH, 1, 6 * D))


def _modulated(x_ref, g_ref, sh_ref, sc_ref):
    x = x_ref[...]
    y = x * lax.rsqrt(jnp.mean(x * x, axis=-1, keepdims=True) + EPS) * g_ref[...]
    i = pl.program_id(0)
    return y * (1.0 + _row_gate(sc_ref, i, TM_MOD)) + _row_gate(sh_ref, i, TM_MOD)


def _modulate_kernel(x_ref, g_ref, sh_ref, sc_ref, o_ref):
    o_ref[...] = _modulated(x_ref, g_ref, sh_ref, sc_ref).astype(o_ref.dtype)


def _mod_specs(layer, sh_col, sc_col):
    return [
        pl.BlockSpec((TM_MOD, D), lambda i: (i, 0)),
        pl.BlockSpec((None, 1, D), lambda i: (layer, 0, 0)),
        pl.BlockSpec((None, N_MOD, D), lambda i: (layer, 0, sh_col)),
        pl.BlockSpec((None, N_MOD, D), lambda i: (layer, 0, sc_col)),
    ]


def modulate(x, norm_g, mods, layer, sh_col, sc_col):
    return pl.pallas_call(
        _modulate_kernel,
        grid=(T // TM_MOD,),
        in_specs=_mod_specs(layer, sh_col, sc_col),
        out_specs=pl.BlockSpec((TM_MOD, D), lambda i: (i, 0)),
        out_shape=jax.ShapeDtypeStruct((T, D), BF16),
        compiler_params=_params("arbitrary"),
        name="modulate",
    )(x, norm_g.reshape(DEPTH, 1, D), mods, mods)


def _modulate_route_kernel(x_ref, g_ref, sh_ref, sc_ref, rt_ref, h_ref, idx_ref, gate_ref):
    h = _modulated(x_ref, g_ref, sh_ref, sc_ref)
    h_ref[...] = h
    logits = jnp.dot(h, rt_ref[...], precision=HIGHEST, preferred_element_type=F32)
    lt = logits.T[:N_EXP]
    eid = lax.broadcasted_iota(jnp.int32, lt.shape, 0)
    m1 = jnp.max(lt, axis=0, keepdims=True)
    i1 = jnp.min(jnp.where(lt == m1, eid, N_EXP), axis=0, keepdims=True)
    lt2 = jnp.where(eid == i1, -jnp.inf, lt)
    m2 = jnp.max(lt2, axis=0, keepdims=True)
    i2 = jnp.min(jnp.where(lt2 == m2, eid, N_EXP), axis=0, keepdims=True)
    e = jnp.exp(m2 - m1)
    idx_ref[...] = jnp.concatenate([i1, i2], axis=0)
    gate_ref[...] = jnp.concatenate([1.0 / (1.0 + e), e / (1.0 + e)], axis=0)


def modulate_route(x, norm_g, mods, layer, sh_col, sc_col, router_pad, j):
    return pl.pallas_call(
        _modulate_route_kernel,
        grid=(T // TM_MOD,),
        in_specs=_mod_specs(layer, sh_col, sc_col) + [
            pl.BlockSpec((None, D, LANES), lambda i: (j, 0, 0)),
        ],
        out_specs=[
            pl.BlockSpec((TM_MOD, D), lambda i: (i, 0)),
            pl.BlockSpec((2, TM_MOD), lambda i: (0, i)),
            pl.BlockSpec((2, TM_MOD), lambda i: (0, i)),
        ],
        out_shape=[
            jax.ShapeDtypeStruct((T, D), F32),
            jax.ShapeDtypeStruct((2, T), jnp.int32),
            jax.ShapeDtypeStruct((2, T), F32),
        ],
        compiler_params=_params("arbitrary"),
        name="modulate_route",
    )(x, norm_g.reshape(DEPTH, 1, D), mods, mods, router_pad)


def _cast_weight(w_ref, wb_ref):
    @pl.when(pl.program_id(1) == 0)
    def _():
        wb_ref[...] = w_ref[...].astype(BF16)


def _mm_plain_kernel(a_ref, w_ref, o_ref, wb_ref, *, scale, n_scaled):
    _cast_weight(w_ref, wb_ref)
    acc = _dot(a_ref[...], wb_ref[...])
    if n_scaled:
        acc = acc * jnp.where(pl.program_id(0) < n_scaled, scale, 1.0)
    o_ref[...] = acc.astype(o_ref.dtype)


def mm_plain(a, w, layer, col0, n_out, out_dtype, tm, tn, scale=1.0, n_scaled=0):
    k = a.shape[1]
    c0 = col0 // tn
    return pl.pallas_call(
        functools.partial(_mm_plain_kernel, scale=scale, n_scaled=n_scaled),
        grid=(n_out // tn, T // tm),
        in_specs=[
            pl.BlockSpec((tm, k), lambda n, m: (m, 0)),
            pl.BlockSpec((None, k, tn), lambda n, m: (layer, 0, n + c0)),
        ],
        out_specs=pl.BlockSpec((tm, tn), lambda n, m: (m, n)),
        out_shape=jax.ShapeDtypeStruct((T, n_out), out_dtype),
        scratch_shapes=[pltpu.VMEM((k, tn), BF16)],
        compiler_params=_params("arbitrary", "arbitrary"),
        name="mm_plain",
    )(a, w)


def _mm_rope_kernel(a_ref, w_ref, cos_ref, sin_ref, o_ref, wb_ref, *, tn):
    _cast_weight(w_ref, wb_ref)
    acc = _dot(a_ref[...], wb_ref[...])
    scale = jnp.where(pl.program_id(0) == 0, GLA_DK ** -0.5, 1.0)
    for s in range(tn // LANES):
        xs = acc[:, s * LANES:(s + 1) * LANES]
        t = (s % 2) * LANES
        rot = xs * cos_ref[:, t:t + LANES] + pltpu.roll(xs, LANES // 2, 1) * sin_ref[:, t:t + LANES]
        o_ref[:, s * LANES:(s + 1) * LANES] = rot * scale


def mm_rope(a, w, layer, cos_t, sin_t, tm):
    tn = GLA_QK
    per_batch = ROWS // tm
    return pl.pallas_call(
        functools.partial(_mm_rope_kernel, tn=tn),
        grid=(2, T // tm),
        in_specs=[
            pl.BlockSpec((tm, D), lambda n, m: (m, 0)),
            pl.BlockSpec((None, D, tn), lambda n, m: (layer, 0, n)),
            pl.BlockSpec((tm, GLA_DK), lambda n, m: (m % per_batch, 0)),
            pl.BlockSpec((tm, GLA_DK), lambda n, m: (m % per_batch, 0)),
        ],
        out_specs=pl.BlockSpec((tm, tn), lambda n, m: (m, n)),
        out_shape=jax.ShapeDtypeStruct((T, 2 * GLA_QK), F32),
        scratch_shapes=[pltpu.VMEM((D, tn), BF16)],
        compiler_params=_params("arbitrary", "arbitrary"),
        name="mm_rope",
    )(a, w, cos_t, sin_t)


def _mm_resid_kernel(a_ref, w_ref, x_ref, gt_ref, o_ref, wb_ref, *, tm):
    _cast_weight(w_ref, wb_ref)
    acc = _dot(a_ref[...], wb_ref[...])
    o_ref[...] = x_ref[...] + _row_gate(gt_ref, pl.program_id(1), tm) * acc


def mm_resid(a, w, layer_w, x, mods, layer, gate_col, tm, tn):
    k = a.shape[1]
    g0 = gate_col * D // tn
    return pl.pallas_call(
        functools.partial(_mm_resid_kernel, tm=tm),
        grid=(D // tn, T // tm),
        in_specs=[
            pl.BlockSpec((tm, k), lambda n, m: (m, 0)),
            pl.BlockSpec((None, k, tn), lambda n, m: (layer_w, 0, n)),
            pl.BlockSpec((tm, tn), lambda n, m: (m, n)),
            pl.BlockSpec((None, N_MOD, tn), lambda n, m: (layer, 0, g0 + n)),
        ],
        out_specs=pl.BlockSpec((tm, tn), lambda n, m: (m, n)),
        out_shape=jax.ShapeDtypeStruct((T, D), F32),
        scratch_shapes=[pltpu.VMEM((k, tn), BF16)],
        compiler_params=_params("arbitrary", "arbitrary"),
        name="mm_resid",
    )(a, w, x, mods)


def _mm_swiglu_kernel(a_ref, wg_ref, wu_ref, o_ref, wgb_ref, wub_ref):
    _cast_weight(wg_ref, wgb_ref)
    _cast_weight(wu_ref, wub_ref)
    a = a_ref[...]
    o_ref[...] = (_silu(_dot(a, wgb_ref[...])) * _dot(a, wub_ref[...])).astype(o_ref.dtype)


def mm_swiglu(a, wg, wu, layer, tm, tn):
    return pl.pallas_call(
        _mm_swiglu_kernel,
        grid=(D_FF // tn, T // tm),
        in_specs=[
            pl.BlockSpec((tm, D), lambda n, m: (m, 0)),
            pl.BlockSpec((None, D, tn), lambda n, m: (layer, 0, n)),
            pl.BlockSpec((None, D, tn), lambda n, m: (layer, 0, n)),
        ],
        out_specs=pl.BlockSpec((tm, tn), lambda n, m: (m, n)),
        out_shape=jax.ShapeDtypeStruct((T, D_FF), BF16),
        scratch_shapes=[pltpu.VMEM((D, tn), BF16), pltpu.VMEM((D, tn), BF16)],
        compiler_params=_params("arbitrary", "arbitrary"),
        name="mm_swiglu",
    )(a, wg, wu)


def _log_sigmoid(v):
    return jnp.minimum(v, 0.0) - jnp.log(1.0 + jnp.exp(-jnp.abs(v)))


def _gla_gate_kernel(h_ref, wz_ref, wf_ref, bf_ref, wb_ref, bb_ref, gf_ref, gb_ref):
    z = _dot(h_ref[...], wz_ref[...].astype(BF16))
    af = jnp.dot(z, wf_ref[...], precision=HIGHEST, preferred_element_type=F32) + bf_ref[...]
    ab = jnp.dot(z, wb_ref[...], precision=HIGHEST, preferred_element_type=F32) + bb_ref[...]
    gf_ref[...] = _log_sigmoid(af) / GLA_TAU
    gb_ref[...] = _log_sigmoid(ab) / GLA_TAU


def gla_gates(h, wz_pad, wf_pad, bg_f, wb_pad, bg_b, j, tm):
    row = lambda i: (i, 0)
    return pl.pallas_call(
        _gla_gate_kernel,
        grid=(T // tm,),
        in_specs=[
            pl.BlockSpec((tm, D), row),
            pl.BlockSpec((D, LANES), lambda i: (0, 0)),
            pl.BlockSpec((LANES, GLA_QK), lambda i: (0, 0)),
            pl.BlockSpec((None, 1, GLA_QK), lambda i: (j, 0, 0)),
            pl.BlockSpec((LANES, GLA_QK), lambda i: (0, 0)),
            pl.BlockSpec((None, 1, GLA_QK), lambda i: (j, 0, 0)),
        ],
        out_specs=[pl.BlockSpec((tm, GLA_QK), row), pl.BlockSpec((tm, GLA_QK), row)],
        out_shape=[jax.ShapeDtypeStruct((T, GLA_QK), F32)] * 2,
        compiler_params=_params("arbitrary"),
        name="gla_gates",
    )(h, wz_pad, wf_pad, bg_f.reshape(-1, 1, GLA_QK), wb_pad, bg_b.reshape(-1, 1, GLA_QK))


def _gla_chunk(q, k, v, g, st_ref, rev):
    c, sub, nsub = GLA_C, GLA_SUB, GLA_C // GLA_SUB
    ri = lax.broadcasted_iota(jnp.int32, (c, c), 0)
    ci = lax.broadcasted_iota(jnp.int32, (c, c), 1)
    tri = (ci >= ri if rev else ci <= ri).astype(F32)
    bc = jnp.dot(tri, g, precision=HIGHEST, preferred_element_type=F32)
    b_end = bc[0:1] if rev else bc[c - 1:c]
    st = st_ref[...]
    o = _dot_nt((q * jnp.exp(bc)).astype(BF16), st.astype(BF16))

    lane = lax.broadcasted_iota(jnp.int32, (sub, c), 1)
    srow = lax.broadcasted_iota(jnp.int32, (sub, 1), 0)
    blocks = []
    for bi in range(nsub):
        lo, hi = sub * bi, sub * (bi + 1)
        q_i, k_i, b_i = q[lo:hi], k[lo:hi], bc[lo:hi]
        a_i = jnp.zeros((sub, c), F32)
        if not rev and bi > 0:
            ref = bc[lo - 1:lo]
            kt = jnp.concatenate([k[:lo] * jnp.exp(ref - bc[:lo]), jnp.zeros((c - lo, GLA_DK), F32)], axis=0)
            a_i = _dot_nt((q_i * jnp.exp(b_i - ref)).astype(BF16), kt.astype(BF16))
        if rev and bi < nsub - 1:
            ref = bc[hi:hi + 1]
            kt = jnp.concatenate([jnp.zeros((hi, GLA_DK), F32), k[hi:] * jnp.exp(ref - bc[hi:])], axis=0)
            a_i = _dot_nt((q_i * jnp.exp(b_i - ref)).astype(BF16), kt.astype(BF16))
        for jj in range(sub):
            t = q_i * k_i[jj:jj + 1] * jnp.exp(b_i - b_i[jj:jj + 1])
            s = jnp.sum(t, axis=-1, keepdims=True)
            seen = srow <= jj if rev else srow >= jj
            a_i = jnp.where(lane == lo + jj, jnp.where(seen, s, 0.0), a_i)
        blocks.append(a_i)
    a = jnp.concatenate(blocks, axis=0)
    o = o + _dot(a.astype(BF16), v)

    kd = (k * jnp.exp(b_end - bc)).astype(BF16)
    st_ref[...] = st * jnp.exp(b_end) + _dot_tn(v, kd)
    return o


def _gla_kernel(q_ref, k_ref, v_ref, r_ref, gf_ref, gb_ref, ng_ref, y_ref, st_ref, o_ref):
    nch = ROWS // GLA_C
    nctx = CTX // GLA_C

    def load(ch, g_ref):
        r0 = pl.multiple_of(ch * GLA_C, GLA_C)
        rows = pl.ds(r0, GLA_C)
        return rows, q_ref[rows, :], k_ref[rows, :], v_ref[rows, :], g_ref[rows, :]

    st_ref[...] = jnp.zeros_like(st_ref)

    def fwd(s, carry):
        rows, q, k, v, g = load(s, gf_ref)
        o_ref[rows, :] = _gla_chunk(q, k, v, g, st_ref, False)
        return carry

    lax.fori_loop(0, nch, fwd, 0)
    st_ref[...] = jnp.zeros_like(st_ref)

    def bwd(s, carry):
        ch = jnp.where(s < nctx, nctx - 1 - s, nch + nctx - 1 - s)
        rows, q, k, v, g = load(ch, gb_ref)
        o = o_ref[rows, :] + _gla_chunk(q, k, v, g, st_ref, True)
        on = o * lax.rsqrt(jnp.mean(o * o, axis=-1, keepdims=True) + EPS) * ng_ref[...]
        y_ref[rows, :] = (on * _silu(r_ref[rows, :].astype(F32))).astype(y_ref.dtype)
        return carry

    lax.fori_loop(0, nch, bwd, 0)


def gla_scan(qk, vr, gf, gb, norm_g, j):
    return pl.pallas_call(
        _gla_kernel,
        grid=(BATCH, GLA_H),
        in_specs=[
            pl.BlockSpec((ROWS, GLA_DK), lambda b, h: (b, h)),
            pl.BlockSpec((ROWS, GLA_DK), lambda b, h: (b, GLA_H + h)),
            pl.BlockSpec((ROWS, GLA_DV), lambda b, h: (b, h)),
            pl.BlockSpec((ROWS, GLA_DV), lambda b, h: (b, GLA_H + h)),
            pl.BlockSpec((ROWS, GLA_DK), lambda b, h: (b, h)),
            pl.BlockSpec((ROWS, GLA_DK), lambda b, h: (b, h)),
            pl.BlockSpec((None, 1, GLA_DV), lambda b, h: (j, 0, 0)),
        ],
        out_specs=pl.BlockSpec((ROWS, GLA_DV), lambda b, h: (b, h)),
        out_shape=jax.ShapeDtypeStruct((T, D), BF16),
        scratch_shapes=[pltpu.VMEM((GLA_DV, GLA_DK), F32), pltpu.VMEM((ROWS, GLA_DV), F32)],
        compiler_params=_params("arbitrary", "arbitrary"),
        name="gla_scan",
    )(qk, qk, vr, vr, gf, gb, norm_g.reshape(-1, 1, GLA_DV))


def rope_tables():
    half = GLA_DK // 2
    freqs = ROPE_BASE ** (-np.arange(0, half, 2, dtype=np.float32) / half)
    t = np.arange(SEQ)
    pos = np.stack([t // GRID_W, t % GRID_W], axis=-1).astype(np.float32)
    ang = pos[:, :, None] * freqs
    cos, sin = np.cos(ang), np.sin(ang)
    cos_t = np.concatenate([cos, cos], axis=-1).reshape(SEQ, GLA_DK)
    sin_t = np.concatenate([-sin, sin], axis=-1).reshape(SEQ, GLA_DK)
    cos_t = np.concatenate([np.ones((CTX, GLA_DK), np.float32), cos_t], axis=0)
    sin_t = np.concatenate([np.zeros((CTX, GLA_DK), np.float32), sin_t], axis=0)
    return jnp.asarray(cos_t, F32), jnp.asarray(sin_t, F32)


def _nat_kernel(q_ref, k_ref, v_ref, bias_ref, y_ref, *, need_ctx):
    kc = k_ref[0:CTX, :]
    vc = v_ref[0:CTX, :]
    if need_ctx:
        s = _dot_nt(q_ref[0:CTX, :], kc)
        p = jnp.exp(s - jnp.max(s, axis=-1, keepdims=True))
        o = _dot(p.astype(BF16), vc) / jnp.sum(p, axis=-1, keepdims=True)
        y_ref[0:CTX, :] = o.astype(y_ref.dtype)
    else:
        y_ref[0:CTX, :] = jnp.zeros((CTX, NAT_DH), y_ref.dtype)
    n_keys = WIN_R * GRID_W

    def body(r, carry):
        rs = jnp.clip(r - WIN_R // 2, 0, GRID_H - WIN_R)
        q0 = pl.multiple_of(CTX + r * GRID_W, GRID_W)
        k0 = pl.multiple_of(CTX + rs * GRID_W, GRID_W)
        q = q_ref[pl.ds(q0, GRID_W), :]
        s_lat = _dot_nt(q, k_ref[pl.ds(k0, n_keys), :]) + bias_ref[r - rs]
        s_ctx = _dot_nt(q, kc)
        m = jnp.maximum(jnp.max(s_lat, axis=-1, keepdims=True), jnp.max(s_ctx, axis=-1, keepdims=True))
        p_lat = jnp.exp(s_lat - m)
        p_ctx = jnp.exp(s_ctx - m)
        denom = jnp.sum(p_lat, axis=-1, keepdims=True) + jnp.sum(p_ctx, axis=-1, keepdims=True)
        o = _dot(p_lat.astype(BF16), v_ref[pl.ds(k0, n_keys), :]) + _dot(p_ctx.astype(BF16), vc)
        y_ref[pl.ds(q0, GRID_W), :] = (o / denom).astype(y_ref.dtype)
        return carry

    lax.fori_loop(0, GRID_H, body, 0)


def nat_attention(qkv, bias_tbl, need_ctx):
    return pl.pallas_call(
        functools.partial(_nat_kernel, need_ctx=need_ctx),
        grid=(NAT_H, BATCH),
        in_specs=[
            pl.BlockSpec((ROWS, NAT_DH), lambda h, b: (b, h)),
            pl.BlockSpec((ROWS, NAT_DH), lambda h, b: (b, NAT_H + h)),
            pl.BlockSpec((ROWS, NAT_DH), lambda h, b: (b, 2 * NAT_H + h)),
            pl.BlockSpec((None, WIN_R, GRID_W, WIN_R * GRID_W), lambda h, b: (h, 0, 0, 0)),
        ],
        out_specs=pl.BlockSpec((ROWS, NAT_DH), lambda h, b: (b, h)),
        out_shape=jax.ShapeDtypeStruct((T, D), BF16),
        compiler_params=_params("arbitrary", "arbitrary"),
        name="nat_attention",
    )(qkv, qkv, qkv, bias_tbl)


def nat_bias_table(rpb):
    off = np.arange(WIN_R)[:, None]
    dr = np.arange(WIN_R)[None, :] - off + (WIN_R - 1)
    qc = np.arange(GRID_W)[:, None]
    kcol = np.arange(GRID_W)[None, :]
    start = np.clip(qc - WIN_C // 2, 0, GRID_W - WIN_C)
    in_win = (kcol >= start) & (kcol < start + WIN_C)
    dc = np.clip(kcol - qc + WIN_C - 1, 0, 2 * WIN_C - 2)
    tbl = rpb[:, dr][:, :, :, dc]
    tbl = jnp.where(in_win, tbl, -jnp.inf)
    return jnp.transpose(tbl, (0, 1, 3, 2, 4)).reshape(NAT_H, WIN_R, GRID_W, WIN_R * GRID_W)


def _gather_kernel(tok_ref, src_ref, o_ref, buf_ref, sem):
    base = pl.program_id(0) * TG

    def row_copy(r):
        return pltpu.make_async_copy(src_ref.at[pl.ds(tok_ref[base + r], 1)], buf_ref.at[pl.ds(r, 1)], sem)

    def start(r, carry):
        row_copy(r).start()
        return carry

    def wait(r, carry):
        row_copy(r).wait()
        return carry

    lax.fori_loop(0, TG, start, 0)
    lax.fori_loop(0, TG, wait, 0)
    o_ref[...] = buf_ref[...].astype(o_ref.dtype)


def moe_gather(slot_token, h):
    n_slots = slot_token.shape[0]
    return pl.pallas_call(
        _gather_kernel,
        grid_spec=pltpu.PrefetchScalarGridSpec(
            num_scalar_prefetch=1,
            grid=(n_slots // TG,),
            in_specs=[pl.BlockSpec(memory_space=pl.ANY)],
            out_specs=pl.BlockSpec((TG, D), lambda i, tok: (i, 0)),
            scratch_shapes=[pltpu.VMEM((TG, D), F32), pltpu.SemaphoreType.DMA(())],
        ),
        out_shape=jax.ShapeDtypeStruct((n_slots, D), BF16),
        compiler_params=_params("arbitrary"),
        name="moe_gather",
    )(slot_token, h)


def _expert_changed(te_ref, i):
    return (i == 0) | (te_ref[i] != te_ref[jnp.maximum(i - 1, 0)])


def _moe_up_kernel(te_ref, tv_ref, a_ref, wg_ref, wu_ref, o_ref, wgb_ref, wub_ref):
    i = pl.program_id(1)

    @pl.when(_expert_changed(te_ref, i))
    def _():
        wgb_ref[...] = wg_ref[...].astype(BF16)
        wub_ref[...] = wu_ref[...].astype(BF16)

    @pl.when(tv_ref[i] == 1)
    def _():
        a = a_ref[...]
        o_ref[...] = (_silu(_dot(a, wgb_ref[...])) * _dot(a, wub_ref[...])).astype(o_ref.dtype)

    @pl.when(tv_ref[i] == 0)
    def _():
        o_ref[...] = jnp.zeros_like(o_ref)


def moe_up(tile_expert, tile_valid, xs, wg, wu, j):
    n_slots = xs.shape[0]
    w_spec = pl.BlockSpec((None, None, D, TF_MOE), lambda f, i, te, tv: (j, te[i], 0, f))
    return pl.pallas_call(
        _moe_up_kernel,
        grid_spec=pltpu.PrefetchScalarGridSpec(
            num_scalar_prefetch=2,
            grid=(D_FF // TF_MOE, n_slots // TM_MOE),
            in_specs=[pl.BlockSpec((TM_MOE, D), lambda f, i, te, tv: (i, 0)), w_spec, w_spec],
            out_specs=pl.BlockSpec((TM_MOE, TF_MOE), lambda f, i, te, tv: (i, f)),
            scratch_shapes=[pltpu.VMEM((D, TF_MOE), BF16), pltpu.VMEM((D, TF_MOE), BF16)],
        ),
        out_shape=jax.ShapeDtypeStruct((n_slots, D_FF), BF16),
        compiler_params=_params("arbitrary", "arbitrary"),
        name="moe_up",
    )(tile_expert, tile_valid, xs, wg, wu)


def _moe_down_kernel(te_ref, tv_ref, a_ref, w_ref, sg_ref, o_ref, wb_ref):
    i = pl.program_id(1)

    @pl.when(_expert_changed(te_ref, i))
    def _():
        wb_ref[...] = w_ref[...].astype(BF16)

    @pl.when(tv_ref[i] == 1)
    def _():
        o_ref[...] = sg_ref[...] * _dot(a_ref[...], wb_ref[...])

    @pl.when(tv_ref[i] == 0)
    def _():
        o_ref[...] = jnp.zeros_like(o_ref)


def moe_down(tile_expert, tile_valid, act, wd, slot_gate, j):
    n_slots = act.shape[0]
    tn = TN_MOE_DOWN
    return pl.pallas_call(
        _moe_down_kernel,
        grid_spec=pltpu.PrefetchScalarGridSpec(
            num_scalar_prefetch=2,
            grid=(D // tn, n_slots // TM_MOE),
            in_specs=[
                pl.BlockSpec((TM_MOE, D_FF), lambda n, i, te, tv: (i, 0)),
                pl.BlockSpec((None, None, D_FF, tn), lambda n, i, te, tv: (j, te[i], 0, n)),
                pl.BlockSpec((TM_MOE, 1), lambda n, i, te, tv: (i, 0)),
            ],
            out_specs=pl.BlockSpec((TM_MOE, tn), lambda n, i, te, tv: (i, n)),
            scratch_shapes=[pltpu.VMEM((D_FF, tn), BF16)],
        ),
        out_shape=jax.ShapeDtypeStruct((n_slots, D), F32),
        compiler_params=_params("arbitrary", "arbitrary"),
        name="moe_down",
    )(tile_expert, tile_valid, act, wd, slot_gate)


def _combine_kernel(p0_ref, p1_ref, ys_ref, x_ref, gt_ref, o_ref, b0_ref, b1_ref, sem):
    i = pl.program_id(0)
    base = i * TG

    def copies(r):
        return (pltpu.make_async_copy(ys_ref.at[pl.ds(p0_ref[base + r], 1)], b0_ref.at[pl.ds(r, 1)], sem.at[0]),
                pltpu.make_async_copy(ys_ref.at[pl.ds(p1_ref[base + r], 1)], b1_ref.at[pl.ds(r, 1)], sem.at[1]))

    def start(r, carry):
        c0, c1 = copies(r)
        c0.start()
        c1.start()
        return carry

    def wait(r, carry):
        c0, c1 = copies(r)
        c0.wait()
        c1.wait()
        return carry

    lax.fori_loop(0, TG, start, 0)
    lax.fori_loop(0, TG, wait, 0)
    o_ref[...] = x_ref[...] + _row_gate(gt_ref, i, TG) * (b0_ref[...] + b1_ref[...])


def moe_combine(pos0, pos1, ys, x, mods, layer, gate_col):
    return pl.pallas_call(
        _combine_kernel,
        grid_spec=pltpu.PrefetchScalarGridSpec(
            num_scalar_prefetch=2,
            grid=(T // TG,),
            in_specs=[
                pl.BlockSpec(memory_space=pl.ANY),
                pl.BlockSpec((TG, D), lambda i, p0, p1: (i, 0)),
                pl.BlockSpec((None, N_MOD, D), lambda i, p0, p1: (layer, 0, gate_col)),
            ],
            out_specs=pl.BlockSpec((TG, D), lambda i, p0, p1: (i, 0)),
            scratch_shapes=[pltpu.VMEM((TG, D), F32), pltpu.VMEM((TG, D), F32), pltpu.SemaphoreType.DMA((2,))],
        ),
        out_shape=jax.ShapeDtypeStruct((T, D), F32),
        compiler_params=_params("arbitrary"),
        name="moe_combine",
    )(pos0, pos1, ys, x, mods)


def moe_plan(idx, gates, tokens):
    n_tok = tokens.shape[0]
    n_pairs = 2 * n_tok
    n_tiles = n_pairs // TM_MOE + N_EXP
    n_slots = n_tiles * TM_MOE
    e = idx[:, tokens].reshape(-1)
    w = gates[:, tokens].reshape(-1)
    tok = jnp.tile(jnp.asarray(tokens, jnp.int32), 2)
    onehot = (e[:, None] == jnp.arange(N_EXP)[None, :]).astype(jnp.int32)
    rank = jnp.sum((jnp.cumsum(onehot, axis=0) - onehot) * onehot, axis=1)
    counts = jnp.sum(onehot, axis=0)
    tiles = (counts + TM_MOE - 1) // TM_MOE
    tile_end = jnp.cumsum(tiles)
    start = (tile_end - tiles) * TM_MOE
    dest = start[e] + rank
    slot_token = jnp.zeros((n_slots,), jnp.int32).at[dest].set(tok)
    slot_gate = jnp.zeros((n_slots,), F32).at[dest].set(w)
    tile_ids = jnp.arange(n_tiles)
    used = tile_end[-1]
    tile_expert = jnp.searchsorted(tile_end, jnp.minimum(tile_ids, used - 1), side="right").astype(jnp.int32)
    tile_valid = (tile_ids < used).astype(jnp.int32)
    pos = jnp.zeros((2, T), jnp.int32).at[:, tokens].set(dest.reshape(2, n_tok))
    return slot_token, slot_gate.reshape(n_slots, 1), tile_expert, tile_valid, pos[0], pos[1]


def _final_norm_kernel(x_ref, g_ref, o_ref):
    x = x_ref[...]
    o_ref[...] = x * lax.rsqrt(jnp.mean(x * x, axis=-1, keepdims=True) + EPS) * g_ref[...]


def final_norm(x, g):
    per_batch = SEQ // TM_MOD
    tiles = ROWS // TM_MOD
    first = CTX // TM_MOD
    out = pl.pallas_call(
        _final_norm_kernel,
        grid=(BATCH * per_batch,),
        in_specs=[
            pl.BlockSpec((TM_MOD, D), lambda i: ((i // per_batch) * tiles + first + i % per_batch, 0)),
            pl.BlockSpec((1, D), lambda i: (0, 0)),
        ],
        out_specs=pl.BlockSpec((TM_MOD, D), lambda i: (i, 0)),
        out_shape=jax.ShapeDtypeStruct((BATCH * SEQ, D), F32),
        compiler_params=_params("arbitrary"),
        name="final_norm",
    )(x, g.reshape(1, D))
    return out.reshape(BATCH, SEQ, D)


def _pad_cols(w, n):
    return jnp.pad(w, ((0, 0), (0, n - w.shape[1])))


def kernel(x, c, ctx, c_ctx, ada_w, ada_b, norm_mix_g, norm_ffn_g, gla_w_in, gla_wg_fwd, gla_bg_fwd, gla_wg_bwd, gla_bg_bwd, gla_norm_g, gla_w_out, nat_w_in, nat_rpb, nat_w_out, ffn_w_gate, ffn_w_up, ffn_w_down, moe_router, moe_w_gate, moe_w_up, moe_w_down, final_norm_g):
    xs = jnp.concatenate([ctx, x], axis=1).reshape(T, D)
    c8 = jnp.concatenate([c, c_ctx[None, :], jnp.zeros((N_MOD - BATCH - 1, D), F32)], axis=0)
    mods = ada_tables(c8, ada_w, ada_b)
    cos_t, sin_t = rope_tables()
    router_pad = jnp.pad(moe_router, ((0, 0), (0, 0), (0, LANES - N_EXP)))
    all_tokens = np.arange(T)
    latent_tokens = all_tokens.reshape(BATCH, ROWS)[:, CTX:].reshape(-1)

    for i in range(DEPTH):
        j = i // 2
        last = i == DEPTH - 1
        h = modulate(xs, norm_mix_g, mods, i, 0, 1)
        if i % 2 == 0:
            qk = mm_rope(h, gla_w_in, j, cos_t, sin_t, tm=1152)
            vr = mm_plain(h, gla_w_in, j, 2 * GLA_QK, 2 * D, BF16, tm=1152, tn=1024)
            z0 = 2 * GLA_QK + 2 * D
            wz_pad = _pad_cols(gla_w_in[j, :, z0:z0 + 2 * GLA_RANK], LANES)
            wf_pad = jnp.pad(gla_wg_fwd[j], ((0, LANES - GLA_RANK), (0, 0)))
            wb_pad = jnp.pad(gla_wg_bwd[j], ((GLA_RANK, LANES - 2 * GLA_RANK), (0, 0)))
            gf, gb = gla_gates(h, wz_pad, wf_pad, gla_bg_fwd, wb_pad, gla_bg_bwd, j, tm=1152)
            y = gla_scan(qk, vr, gf, gb, gla_norm_g, j)
            xs = mm_resid(y, gla_w_out, j, xs, mods, i, 2, tm=1152, tn=1024)
        else:
            qkv = mm_plain(h, nat_w_in, j, 0, 3 * D, BF16, tm=1152, tn=1024,
                           scale=NAT_DH ** -0.5, n_scaled=D // 1024)
            y = nat_attention(qkv, nat_bias_table(nat_rpb[j]), not last)
            xs = mm_resid(y, nat_w_out, j, xs, mods, i, 2, tm=1152, tn=1024)
        if i % 2 == 0:
            h2 = modulate(xs, norm_ffn_g, mods, i, 3, 4)
            act = mm_swiglu(h2, ffn_w_gate, ffn_w_up, j, tm=1152, tn=512)
            xs = mm_resid(act, ffn_w_down, j, xs, mods, i, 5, tm=576, tn=512)
        else:
            h2, idx, gates = modulate_route(xs, norm_ffn_g, mods, i, 3, 4, router_pad, j)
            tokens = latent_tokens if last else all_tokens
            slot_token, slot_gate, tile_expert, tile_valid, pos0, pos1 = moe_plan(idx, gates, tokens)
            xg = moe_gather(slot_token, h2)
            act = moe_up(tile_expert, tile_valid, xg, moe_w_gate, moe_w_up, j)
            ys = moe_down(tile_expert, tile_valid, act, moe_w_down, slot_gate, j)
            xs = moe_combine(pos0, pos1, ys, xs, mods, i, 5)
    return final_norm(xs, final_norm_g)
```

```python
import functools

import numpy as np
import jax
import jax.numpy as jnp
from jax import lax
from jax.experimental import pallas as pl
from jax.experimental.pallas import tpu as pltpu

F32 = jnp.float32
BF16 = jnp.bfloat16
HIGHEST = lax.Precision.HIGHEST

D = 2048
BATCH = 4
SEQ = 2048
CTX = 256
ROWS = CTX + SEQ
T = BATCH * ROWS
DEPTH = 4
GRID_W = 64
GRID_H = SEQ // GRID_W
EPS = 1e-6
ROPE_BASE = 10000.0
CTX_GROUP = BATCH
N_MOD = 8

GLA_H = 4
GLA_DK = 256
GLA_DV = 512
GLA_QK = GLA_H * GLA_DK
GLA_RANK = 16
GLA_TAU = 16.0
GLA_C = 64
GLA_SUB = 16

NAT_H = 16
NAT_DH = 128
WIN_R = 8
WIN_C = 16
NAT_ROWS_PER_STEP = 8

D_FF = 5632
N_EXP = 8

LANES = 128
VMEM_LIMIT = 56 * 1024 * 1024

TM_MOD = 256
TM_MOE = 512
TM_PART = 256
TF_MOE = 512
TN_MOE_DOWN = 512
TG = 256


def _params(*sem):
    return pltpu.CompilerParams(dimension_semantics=sem, vmem_limit_bytes=VMEM_LIMIT)


def _dot(a, b):
    return jnp.dot(a, b, preferred_element_type=F32)


def _dot_nt(a, b):
    return lax.dot_general(a, b, (((1,), (1,)), ((), ())), preferred_element_type=F32)


def _dot_tn(a, b):
    return lax.dot_general(a, b, (((0,), (0,)), ((), ())), preferred_element_type=F32)


def _silu(v):
    return v * jax.nn.sigmoid(v)


def _row_gate(tab_ref, m, tm):
    start = m * tm
    b = start // ROWS
    rows = start % ROWS + lax.broadcasted_iota(jnp.int32, (tm, 1), 0)
    per_batch = tab_ref[pl.ds(b, 1), :]
    per_ctx = tab_ref[CTX_GROUP:CTX_GROUP + 1, :]
    return jnp.where(rows < CTX, per_ctx, per_batch)


def _ada_kernel(c_ref, w_ref, b_ref, o_ref):
    a = _silu(c_ref[...]).astype(BF16)
    o_ref[...] = _dot(a, w_ref[...].astype(BF16)) + b_ref[...]


def ada_tables(c8, ada_w, ada_b):
    tn = 1024
    return pl.pallas_call(
        _ada_kernel,
        grid=(DEPTH, 6 * D // tn),
        in_specs=[
            pl.BlockSpec((N_MOD, D), lambda l, n: (0, 0)),
            pl.BlockSpec((None, D, tn), lambda l, n: (l, 0, n)),
            pl.BlockSpec((None, 1, tn), lambda l, n: (l, 0, n)),
        ],
        out_specs=pl.BlockSpec((None, N_MOD, tn), lambda l, n: (l, 0, n)),
        out_shape=jax.ShapeDtypeStruct((DEPTH, N_MOD, 6 * D), F32),
        compiler_params=_params("arbitrary", "arbitrary"),
        name="ada_tables",
    )(c8, ada_w, ada_b.reshape(DEPTH, 1, 6 * D))


def _modulated(x_ref, g_ref, sh_ref, sc_ref):
    x = x_ref[...]
    y = x * lax.rsqrt(jnp.mean(x * x, axis=-1, keepdims=True) + EPS) * g_ref[...]
    i = pl.program_id(0)
    return y * (1.0 + _row_gate(sc_ref, i, TM_MOD)) + _row_gate(sh_ref, i, TM_MOD)


def _modulate_kernel(x_ref, g_ref, sh_ref, sc_ref, o_ref):
    o_ref[...] = _modulated(x_ref, g_ref, sh_ref, sc_ref).astype(o_ref.dtype)


def _mod_specs(layer, sh_col, sc_col):
    return [
        pl.BlockSpec((TM_MOD, D), lambda i: (i, 0)),
        pl.BlockSpec((None, 1, D), lambda i: (layer, 0, 0)),
        pl.BlockSpec((None, N_MOD, D), lambda i: (layer, 0, sh_col)),
        pl.BlockSpec((None, N_MOD, D), lambda i: (layer, 0, sc_col)),
    ]


def modulate(x, norm_g, mods, layer, sh_col, sc_col):
    return pl.pallas_call(
        _modulate_kernel,
        grid=(T // TM_MOD,),
        in_specs=_mod_specs(layer, sh_col, sc_col),
        out_specs=pl.BlockSpec((TM_MOD, D), lambda i: (i, 0)),
        out_shape=jax.ShapeDtypeStruct((T, D), BF16),
        compiler_params=_params("arbitrary"),
        name="modulate",
    )(x, norm_g.reshape(DEPTH, 1, D), mods, mods)


def _modulate_route_kernel(x_ref, g_ref, sh_ref, sc_ref, rt_ref, h_ref, idx_ref, gate_ref):
    h = _modulated(x_ref, g_ref, sh_ref, sc_ref)
    h_ref[...] = h
    logits = jnp.dot(h, rt_ref[...], precision=HIGHEST, preferred_element_type=F32)
    lt = logits.T[:N_EXP]
    eid = lax.broadcasted_iota(jnp.int32, lt.shape, 0)
    m1 = jnp.max(lt, axis=0, keepdims=True)
    i1 = jnp.min(jnp.where(lt == m1, eid, N_EXP), axis=0, keepdims=True)
    lt2 = jnp.where(eid == i1, -jnp.inf, lt)
    m2 = jnp.max(lt2, axis=0, keepdims=True)
    i2 = jnp.min(jnp.where(lt2 == m2, eid, N_EXP), axis=0, keepdims=True)
    e = jnp.exp(m2 - m1)
    idx_ref[...] = jnp.concatenate([i1, i2], axis=0)
    gate_ref[...] = jnp.concatenate([1.0 / (1.0 + e), e / (1.0 + e)], axis=0)


def modulate_route(x, norm_g, mods, layer, sh_col, sc_col, router_pad, j):
    return pl.pallas_call(
        _modulate_route_kernel,
        grid=(T // TM_MOD,),
        in_specs=_mod_specs(layer, sh_col, sc_col) + [
            pl.BlockSpec((None, D, LANES), lambda i: (j, 0, 0)),
        ],
        out_specs=[
            pl.BlockSpec((TM_MOD, D), lambda i: (i, 0)),
            pl.BlockSpec((2, TM_MOD), lambda i: (0, i)),
            pl.BlockSpec((2, TM_MOD), lambda i: (0, i)),
        ],
        out_shape=[
            jax.ShapeDtypeStruct((T, D), F32),
            jax.ShapeDtypeStruct((2, T), jnp.int32),
            jax.ShapeDtypeStruct((2, T), F32),
        ],
        compiler_params=_params("arbitrary"),
        name="modulate_route",
    )(x, norm_g.reshape(DEPTH, 1, D), mods, mods, router_pad)


def _cast_weight(w_ref, wb_ref):
    @pl.when(pl.program_id(1) == 0)
    def _():
        wb_ref[...] = w_ref[...].astype(BF16)


def _mm_plain_kernel(a_ref, w_ref, o_ref, wb_ref, *, scale, n_scaled):
    _cast_weight(w_ref, wb_ref)
    acc = _dot(a_ref[...], wb_ref[...])
    if n_scaled:
        acc = acc * jnp.where(pl.program_id(0) < n_scaled, scale, 1.0)
    o_ref[...] = acc.astype(o_ref.dtype)


def mm_plain(a, w, layer, col0, n_out, out_dtype, tm, tn, scale=1.0, n_scaled=0):
    k = a.shape[1]
    c0 = col0 // tn
    return pl.pallas_call(
        functools.partial(_mm_plain_kernel, scale=scale, n_scaled=n_scaled),
        grid=(n_out // tn, T // tm),
        in_specs=[
            pl.BlockSpec((tm, k), lambda n, m: (m, 0)),
            pl.BlockSpec((None, k, tn), lambda n, m: (layer, 0, n + c0)),
        ],
        out_specs=pl.BlockSpec((tm, tn), lambda n, m: (m, n)),
        out_shape=jax.ShapeDtypeStruct((T, n_out), out_dtype),
        scratch_shapes=[pltpu.VMEM((k, tn), BF16)],
        compiler_params=_params("arbitrary", "arbitrary"),
        name="mm_plain",
    )(a, w)


def _mm_rope_kernel(a_ref, w_ref, cos_ref, sin_ref, o_ref, wb_ref, *, tn):
    _cast_weight(w_ref, wb_ref)
    acc = _dot(a_ref[...], wb_ref[...])
    scale = jnp.where(pl.program_id(0) == 0, GLA_DK ** -0.5, 1.0)
    for s in range(tn // LANES):
        xs = acc[:, s * LANES:(s + 1) * LANES]
        t = (s % 2) * LANES
        rot = xs * cos_ref[:, t:t + LANES] + pltpu.roll(xs, LANES // 2, 1) * sin_ref[:, t:t + LANES]
        o_ref[:, s * LANES:(s + 1) * LANES] = rot * scale


def mm_rope(a, w, layer, cos_t, sin_t, tm):
    tn = GLA_QK
    per_batch = ROWS // tm
    return pl.pallas_call(
        functools.partial(_mm_rope_kernel, tn=tn),
        grid=(2, T // tm),
        in_specs=[
            pl.BlockSpec((tm, D), lambda n, m: (m, 0)),
            pl.BlockSpec((None, D, tn), lambda n, m: (layer, 0, n)),
            pl.BlockSpec((tm, GLA_DK), lambda n, m: (m % per_batch, 0)),
            pl.BlockSpec((tm, GLA_DK), lambda n, m: (m % per_batch, 0)),
        ],
        out_specs=pl.BlockSpec((tm, tn), lambda n, m: (m, n)),
        out_shape=jax.ShapeDtypeStruct((T, 2 * GLA_QK), F32),
        scratch_shapes=[pltpu.VMEM((D, tn), BF16)],
        compiler_params=_params("arbitrary", "arbitrary"),
        name="mm_rope",
    )(a, w, cos_t, sin_t)


def _mm_resid_kernel(a_ref, w_ref, x_ref, gt_ref, o_ref, wb_ref, *, tm):
    _cast_weight(w_ref, wb_ref)
    acc = _dot(a_ref[...], wb_ref[...])
    o_ref[...] = x_ref[...] + _row_gate(gt_ref, pl.program_id(1), tm) * acc


def mm_resid(a, w, layer_w, x, mods, layer, gate_col, tm, tn):
    k = a.shape[1]
    g0 = gate_col * D // tn
    return pl.pallas_call(
        functools.partial(_mm_resid_kernel, tm=tm),
        grid=(D // tn, T // tm),
        in_specs=[
            pl.BlockSpec((tm, k), lambda n, m: (m, 0)),
            pl.BlockSpec((None, k, tn), lambda n, m: (layer_w, 0, n)),
            pl.BlockSpec((tm, tn), lambda n, m: (m, n)),
            pl.BlockSpec((None, N_MOD, tn), lambda n, m: (layer, 0, g0 + n)),
        ],
        out_specs=pl.BlockSpec((tm, tn), lambda n, m: (m, n)),
        out_shape=jax.ShapeDtypeStruct((T, D), F32),
        scratch_shapes=[pltpu.VMEM((k, tn), BF16)],
        compiler_params=_params("arbitrary", "arbitrary"),
        name="mm_resid",
    )(a, w, x, mods)


def _mm_swiglu_kernel(a_ref, wg_ref, wu_ref, o_ref, wgb_ref, wub_ref):
    _cast_weight(wg_ref, wgb_ref)
    _cast_weight(wu_ref, wub_ref)
    a = a_ref[...]
    o_ref[...] = (_silu(_dot(a, wgb_ref[...])) * _dot(a, wub_ref[...])).astype(o_ref.dtype)


def mm_swiglu(a, wg, wu, layer, tm, tn):
    return pl.pallas_call(
        _mm_swiglu_kernel,
        grid=(D_FF // tn, T // tm),
        in_specs=[
            pl.BlockSpec((tm, D), lambda n, m: (m, 0)),
            pl.BlockSpec((None, D, tn), lambda n, m: (layer, 0, n)),
            pl.BlockSpec((None, D, tn), lambda n, m: (layer, 0, n)),
        ],
        out_specs=pl.BlockSpec((tm, tn), lambda n, m: (m, n)),
        out_shape=jax.ShapeDtypeStruct((T, D_FF), BF16),
        scratch_shapes=[pltpu.VMEM((D, tn), BF16), pltpu.VMEM((D, tn), BF16)],
        compiler_params=_params("arbitrary", "arbitrary"),
        name="mm_swiglu",
    )(a, wg, wu)


def _log_sigmoid(v):
    return jnp.minimum(v, 0.0) - jnp.log(1.0 + jnp.exp(-jnp.abs(v)))


def _gla_gate_kernel(h_ref, wz_ref, wf_ref, bf_ref, wb_ref, bb_ref, gf_ref, gb_ref):
    z = _dot(h_ref[...], wz_ref[...].astype(BF16))
    af = jnp.dot(z, wf_ref[...], precision=HIGHEST, preferred_element_type=F32) + bf_ref[...]
    ab = jnp.dot(z, wb_ref[...], precision=HIGHEST, preferred_element_type=F32) + bb_ref[...]
    gf_ref[...] = _log_sigmoid(af) / GLA_TAU
    gb_ref[...] = _log_sigmoid(ab) / GLA_TAU


def gla_gates(h, wz_pad, wf_pad, bg_f, wb_pad, bg_b, j, tm):
    row = lambda i: (i, 0)
    return pl.pallas_call(
        _gla_gate_kernel,
        grid=(T // tm,),
        in_specs=[
            pl.BlockSpec((tm, D), row),
            pl.BlockSpec((D, LANES), lambda i: (0, 0)),
            pl.BlockSpec((LANES, GLA_QK), lambda i: (0, 0)),
            pl.BlockSpec((None, 1, GLA_QK), lambda i: (j, 0, 0)),
            pl.BlockSpec((LANES, GLA_QK), lambda i: (0, 0)),
            pl.BlockSpec((None, 1, GLA_QK), lambda i: (j, 0, 0)),
        ],
        out_specs=[pl.BlockSpec((tm, GLA_QK), row), pl.BlockSpec((tm, GLA_QK), row)],
        out_shape=[jax.ShapeDtypeStruct((T, GLA_QK), F32)] * 2,
        compiler_params=_params("arbitrary"),
        name="gla_gates",
    )(h, wz_pad, wf_pad, bg_f.reshape(-1, 1, GLA_QK), wb_pad, bg_b.reshape(-1, 1, GLA_QK))


def _gla_chunks(streams):
    c, sub, nsub = GLA_C, GLA_SUB, GLA_C // GLA_SUB
    ri = lax.broadcasted_iota(jnp.int32, (c, c), 0)
    ci = lax.broadcasted_iota(jnp.int32, (c, c), 1)
    lane = lax.broadcasted_iota(jnp.int32, (sub, c), 1)
    srow = lax.broadcasted_iota(jnp.int32, (sub, 1), 0)

    bcs = []
    for q, k, v, g, st, rev in streams:
        tri = (ci >= ri if rev else ci <= ri).astype(F32)
        bcs.append(jnp.dot(tri, g, precision=HIGHEST, preferred_element_type=F32))

    partial = []
    for (q, k, v, g, st, rev), bc in zip(streams, bcs):
        b_end = bc[0:1] if rev else bc[c - 1:c]
        o = _dot_nt((q * jnp.exp(bc)).astype(BF16), st.astype(BF16))
        st_new = st * jnp.exp(b_end) + _dot_tn(v, (k * jnp.exp(b_end - bc)).astype(BF16))
        off = []
        for bi in range(nsub):
            lo, hi = sub * bi, sub * (bi + 1)
            q_i, b_i = q[lo:hi], bc[lo:hi]
            if not rev and bi > 0:
                ref = bc[lo - 1:lo]
                kt = jnp.concatenate([k[:lo] * jnp.exp(ref - bc[:lo]), jnp.zeros((c - lo, GLA_DK), F32)], axis=0)
                off.append(_dot_nt((q_i * jnp.exp(b_i - ref)).astype(BF16), kt.astype(BF16)))
            elif rev and bi < nsub - 1:
                ref = bc[hi:hi + 1]
                kt = jnp.concatenate([jnp.zeros((hi, GLA_DK), F32), k[hi:] * jnp.exp(ref - bc[hi:])], axis=0)
                off.append(_dot_nt((q_i * jnp.exp(b_i - ref)).astype(BF16), kt.astype(BF16)))
            else:
                off.append(jnp.zeros((sub, c), F32))
        partial.append((o, st_new, off))

    intra = []
    for (q, k, v, g, st, rev), bc, (o, st_new, off) in zip(streams, bcs, partial):
        blocks = []
        for bi in range(nsub):
            lo, hi = sub * bi, sub * (bi + 1)
            q_i, k_i, b_i = q[lo:hi], k[lo:hi], bc[lo:hi]
            a_i = off[bi]
            for jj in range(sub):
                t = q_i * k_i[jj:jj + 1] * jnp.exp(b_i - b_i[jj:jj + 1])
                s = jnp.sum(t, axis=-1, keepdims=True)
                seen = srow <= jj if rev else srow >= jj
                a_i = jnp.where(lane == lo + jj, jnp.where(seen, s, 0.0), a_i)
            blocks.append(a_i)
        intra.append(jnp.concatenate(blocks, axis=0).astype(BF16))

    return [(o + _dot(a, v), st_new) for (q, k, v, g, st, rev), (o, st_new, off), a in zip(streams, partial, intra)]


def _gla_kernel(q_ref, k_ref, v_ref, r_ref, gf_ref, gb_ref, ng_ref, y_ref, stf_ref, stb_ref, of_ref, ob_ref):
    nch = ROWS // GLA_C
    nctx = CTX // GLA_C

    def chunk_rows(ch):
        return pl.ds(pl.multiple_of(ch * GLA_C, GLA_C), GLA_C)

    def stream(rows, g_ref, st_ref, rev):
        return q_ref[rows, :], k_ref[rows, :], v_ref[rows, :], g_ref[rows, :], st_ref[...], rev

    stf_ref[...] = jnp.zeros_like(stf_ref)
    stb_ref[...] = jnp.zeros_like(stb_ref)

    def scan(s, carry):
        rows_f = chunk_rows(s)
        rows_b = chunk_rows(jnp.where(s < nctx, nctx - 1 - s, nch + nctx - 1 - s))
        (o_f, st_f), (o_b, st_b) = _gla_chunks([stream(rows_f, gf_ref, stf_ref, False),
                                                stream(rows_b, gb_ref, stb_ref, True)])
        of_ref[rows_f, :] = o_f
        ob_ref[rows_b, :] = o_b
        stf_ref[...] = st_f
        stb_ref[...] = st_b
        return carry

    lax.fori_loop(0, nch, scan, 0)

    def readout(ch, carry):
        rows = chunk_rows(ch)
        o = of_ref[rows, :] + ob_ref[rows, :]
        on = o * lax.rsqrt(jnp.mean(o * o, axis=-1, keepdims=True) + EPS) * ng_ref[...]
        y_ref[rows, :] = (on * _silu(r_ref[rows, :].astype(F32))).astype(y_ref.dtype)
        return carry

    lax.fori_loop(0, nch, readout, 0)


def gla_scan(qk, vr, gf, gb, norm_g, j):
    return pl.pallas_call(
        _gla_kernel,
        grid=(BATCH, GLA_H),
        in_specs=[
            pl.BlockSpec((ROWS, GLA_DK), lambda b, h: (b, h)),
            pl.BlockSpec((ROWS, GLA_DK), lambda b, h: (b, GLA_H + h)),
            pl.BlockSpec((ROWS, GLA_DV), lambda b, h: (b, h)),
            pl.BlockSpec((ROWS, GLA_DV), lambda b, h: (b, GLA_H + h)),
            pl.BlockSpec((ROWS, GLA_DK), lambda b, h: (b, h)),
            pl.BlockSpec((ROWS, GLA_DK), lambda b, h: (b, h)),
            pl.BlockSpec((None, 1, GLA_DV), lambda b, h: (j, 0, 0)),
        ],
        out_specs=pl.BlockSpec((ROWS, GLA_DV), lambda b, h: (b, h)),
        out_shape=jax.ShapeDtypeStruct((T, D), BF16),
        scratch_shapes=[pltpu.VMEM((GLA_DV, GLA_DK), F32), pltpu.VMEM((GLA_DV, GLA_DK), F32),
                        pltpu.VMEM((ROWS, GLA_DV), F32), pltpu.VMEM((ROWS, GLA_DV), F32)],
        compiler_params=_params("arbitrary", "arbitrary"),
        name="gla_scan",
    )(qk, qk, vr, vr, gf, gb, norm_g.reshape(-1, 1, GLA_DV))


def rope_tables():
    half = GLA_DK // 2
    freqs = ROPE_BASE ** (-np.arange(0, half, 2, dtype=np.float32) / half)
    t = np.arange(SEQ)
    pos = np.stack([t // GRID_W, t % GRID_W], axis=-1).astype(np.float32)
    ang = pos[:, :, None] * freqs
    cos, sin = np.cos(ang), np.sin(ang)
    cos_t = np.concatenate([cos, cos], axis=-1).reshape(SEQ, GLA_DK)
    sin_t = np.concatenate([-sin, sin], axis=-1).reshape(SEQ, GLA_DK)
    cos_t = np.concatenate([np.ones((CTX, GLA_DK), np.float32), cos_t], axis=0)
    sin_t = np.concatenate([np.zeros((CTX, GLA_DK), np.float32), sin_t], axis=0)
    return jnp.asarray(cos_t, F32), jnp.asarray(sin_t, F32)


def _nat_kernel(q_ref, k_ref, v_ref, bias_ref, y_ref, *, need_ctx):
    kc = k_ref[0:CTX, :]
    vc = v_ref[0:CTX, :]
    if need_ctx:
        s = _dot_nt(q_ref[0:CTX, :], kc)
        p = jnp.exp(s - jnp.max(s, axis=-1, keepdims=True))
        o = _dot(p.astype(BF16), vc) / jnp.sum(p, axis=-1, keepdims=True)
        y_ref[0:CTX, :] = o.astype(y_ref.dtype)
    else:
        y_ref[0:CTX, :] = jnp.zeros((CTX, NAT_DH), y_ref.dtype)
    n_keys = WIN_R * GRID_W

    def scores(r):
        rs = jnp.clip(r - WIN_R // 2, 0, GRID_H - WIN_R)
        q0 = pl.multiple_of(CTX + r * GRID_W, GRID_W)
        k0 = pl.multiple_of(CTX + rs * GRID_W, GRID_W)
        q = q_ref[pl.ds(q0, GRID_W), :]
        s_lat = _dot_nt(q, k_ref[pl.ds(k0, n_keys), :]) + bias_ref[r - rs]
        s_ctx = _dot_nt(q, kc)
        return q0, k0, s_lat, s_ctx

    def probs(q0, k0, s_lat, s_ctx):
        m = jnp.maximum(jnp.max(s_lat, axis=-1, keepdims=True), jnp.max(s_ctx, axis=-1, keepdims=True))
        p_lat = jnp.exp(s_lat - m)
        p_ctx = jnp.exp(s_ctx - m)
        denom = jnp.sum(p_lat, axis=-1, keepdims=True) + jnp.sum(p_ctx, axis=-1, keepdims=True)
        return q0, k0, p_lat.astype(BF16), p_ctx.astype(BF16), denom

    def values(q0, k0, p_lat, p_ctx, denom):
        o = _dot(p_lat, v_ref[pl.ds(k0, n_keys), :]) + _dot(p_ctx, vc)
        return q0, (o / denom).astype(y_ref.dtype)

    def body(it, carry):
        rows = [it * NAT_ROWS_PER_STEP + u for u in range(NAT_ROWS_PER_STEP)]
        outs = [values(*pr) for pr in [probs(*sc) for sc in [scores(r) for r in rows]]]
        for q0, o in outs:
            y_ref[pl.ds(q0, GRID_W), :] = o
        return carry

    lax.fori_loop(0, GRID_H // NAT_ROWS_PER_STEP, body, 0)


def nat_attention(qkv, bias_tbl, need_ctx):
    return pl.pallas_call(
        functools.partial(_nat_kernel, need_ctx=need_ctx),
        grid=(NAT_H, BATCH),
        in_specs=[
            pl.BlockSpec((ROWS, NAT_DH), lambda h, b: (b, h)),
            pl.BlockSpec((ROWS, NAT_DH), lambda h, b: (b, NAT_H + h)),
            pl.BlockSpec((ROWS, NAT_DH), lambda h, b: (b, 2 * NAT_H + h)),
            pl.BlockSpec((None, WIN_R, GRID_W, WIN_R * GRID_W), lambda h, b: (h, 0, 0, 0)),
        ],
        out_specs=pl.BlockSpec((ROWS, NAT_DH), lambda h, b: (b, h)),
        out_shape=jax.ShapeDtypeStruct((T, D), BF16),
        compiler_params=_params("arbitrary", "arbitrary"),
        name="nat_attention",
    )(qkv, qkv, qkv, bias_tbl)


def nat_bias_table(rpb):
    off = np.arange(WIN_R)[:, None]
    dr = np.arange(WIN_R)[None, :] - off + (WIN_R - 1)
    qc = np.arange(GRID_W)[:, None]
    kcol = np.arange(GRID_W)[None, :]
    start = np.clip(qc - WIN_C // 2, 0, GRID_W - WIN_C)
    in_win = (kcol >= start) & (kcol < start + WIN_C)
    dc = np.clip(kcol - qc + WIN_C - 1, 0, 2 * WIN_C - 2)
    tbl = rpb[:, dr][:, :, :, dc]
    tbl = jnp.where(in_win, tbl, -jnp.inf)
    return jnp.transpose(tbl, (0, 1, 3, 2, 4)).reshape(NAT_H, WIN_R, GRID_W, WIN_R * GRID_W)


def _gather_kernel(tok_ref, tr_ref, src_ref, o_ref, buf_ref, sem):
    blk = pl.program_id(0)
    base = blk * TG
    per_tile = TM_MOE // TG
    n = jnp.clip(tr_ref[blk // per_tile] - (blk % per_tile) * TG, 0, TG)

    def row_copy(r):
        return pltpu.make_async_copy(src_ref.at[pl.ds(tok_ref[base + r], 1)], buf_ref.at[pl.ds(r, 1)], sem)

    def start(r, carry):
        row_copy(r).start()
        return carry

    def wait(r, carry):
        row_copy(r).wait()
        return carry

    @pl.when(n < TG)
    def _():
        buf_ref[...] = jnp.zeros_like(buf_ref)

    lax.fori_loop(0, n, start, 0)
    lax.fori_loop(0, n, wait, 0)
    o_ref[...] = buf_ref[...].astype(o_ref.dtype)


def moe_gather(slot_token, tile_rows, h):
    n_slots = slot_token.shape[0]
    return pl.pallas_call(
        _gather_kernel,
        grid_spec=pltpu.PrefetchScalarGridSpec(
            num_scalar_prefetch=2,
            grid=(n_slots // TG,),
            in_specs=[pl.BlockSpec(memory_space=pl.ANY)],
            out_specs=pl.BlockSpec((TG, D), lambda i, tok, tr: (i, 0)),
            scratch_shapes=[pltpu.VMEM((TG, D), F32), pltpu.SemaphoreType.DMA(())],
        ),
        out_shape=jax.ShapeDtypeStruct((n_slots, D), BF16),
        compiler_params=_params("arbitrary"),
        name="moe_gather",
    )(slot_token, tile_rows, h)


def _expert_changed(te_ref, i):
    return (i == 0) | (te_ref[i] != te_ref[jnp.maximum(i - 1, 0)])


def _for_occupied_parts(tr_ref, i, o_ref, compute):
    for part in range(TM_MOE // TM_PART):
        rows = slice(part * TM_PART, (part + 1) * TM_PART)

        @pl.when(tr_ref[i] > part * TM_PART)
        def _():
            o_ref[rows, :] = compute(rows)

        @pl.when(tr_ref[i] <= part * TM_PART)
        def _():
            o_ref[rows, :] = jnp.zeros((TM_PART, o_ref.shape[1]), o_ref.dtype)


def _moe_up_kernel(te_ref, tr_ref, a_ref, wg_ref, wu_ref, o_ref, wgb_ref, wub_ref):
    i = pl.program_id(1)

    @pl.when(_expert_changed(te_ref, i))
    def _():
        wgb_ref[...] = wg_ref[...].astype(BF16)
        wub_ref[...] = wu_ref[...].astype(BF16)

    def compute(rows):
        a = a_ref[rows, :]
        return (_silu(_dot(a, wgb_ref[...])) * _dot(a, wub_ref[...])).astype(o_ref.dtype)

    _for_occupied_parts(tr_ref, i, o_ref, compute)


def moe_up(tile_expert, tile_rows, xs, wg, wu, j):
    n_slots = xs.shape[0]
    w_spec = pl.BlockSpec((None, None, D, TF_MOE), lambda f, i, te, tv: (j, te[i], 0, f))
    return pl.pallas_call(
        _moe_up_kernel,
        grid_spec=pltpu.PrefetchScalarGridSpec(
            num_scalar_prefetch=2,
            grid=(D_FF // TF_MOE, n_slots // TM_MOE),
            in_specs=[pl.BlockSpec((TM_MOE, D), lambda f, i, te, tv: (i, 0)), w_spec, w_spec],
            out_specs=pl.BlockSpec((TM_MOE, TF_MOE), lambda f, i, te, tv: (i, f)),
            scratch_shapes=[pltpu.VMEM((D, TF_MOE), BF16), pltpu.VMEM((D, TF_MOE), BF16)],
        ),
        out_shape=jax.ShapeDtypeStruct((n_slots, D_FF), BF16),
        compiler_params=_params("arbitrary", "arbitrary"),
        name="moe_up",
    )(tile_expert, tile_rows, xs, wg, wu)


def _moe_down_kernel(te_ref, tr_ref, a_ref, w_ref, sg_ref, o_ref, wb_ref):
    i = pl.program_id(1)

    @pl.when(_expert_changed(te_ref, i))
    def _():
        wb_ref[...] = w_ref[...].astype(BF16)

    def compute(rows):
        return sg_ref[rows, :] * _dot(a_ref[rows, :], wb_ref[...])

    _for_occupied_parts(tr_ref, i, o_ref, compute)


def moe_down(tile_expert, tile_rows, act, wd, slot_gate, j):
    n_slots = act.shape[0]
    tn = TN_MOE_DOWN
    return pl.pallas_call(
        _moe_down_kernel,
        grid_spec=pltpu.PrefetchScalarGridSpec(
            num_scalar_prefetch=2,
            grid=(D // tn, n_slots // TM_MOE),
            in_specs=[
                pl.BlockSpec((TM_MOE, D_FF), lambda n, i, te, tv: (i, 0)),
                pl.BlockSpec((None, None, D_FF, tn), lambda n, i, te, tv: (j, te[i], 0, n)),
                pl.BlockSpec((TM_MOE, 1), lambda n, i, te, tv: (i, 0)),
            ],
            out_specs=pl.BlockSpec((TM_MOE, tn), lambda n, i, te, tv: (i, n)),
            scratch_shapes=[pltpu.VMEM((D_FF, tn), BF16)],
        ),
        out_shape=jax.ShapeDtypeStruct((n_slots, D), F32),
        compiler_params=_params("arbitrary", "arbitrary"),
        name="moe_down",
    )(tile_expert, tile_rows, act, wd, slot_gate)


def _combine_kernel(p0_ref, p1_ref, ys_ref, x_ref, gt_ref, o_ref, b0_ref, b1_ref, sem):
    i = pl.program_id(0)
    base = i * TG

    def copies(r):
        return (pltpu.make_async_copy(ys_ref.at[pl.ds(p0_ref[base + r], 1)], b0_ref.at[pl.ds(r, 1)], sem.at[0]),
                pltpu.make_async_copy(ys_ref.at[pl.ds(p1_ref[base + r], 1)], b1_ref.at[pl.ds(r, 1)], sem.at[1]))

    def start(r, carry):
        c0, c1 = copies(r)
        c0.start()
        c1.start()
        return carry

    def wait(r, carry):
        c0, c1 = copies(r)
        c0.wait()
        c1.wait()
        return carry

    lax.fori_loop(0, TG, start, 0)
    lax.fori_loop(0, TG, wait, 0)
    o_ref[...] = x_ref[...] + _row_gate(gt_ref, i, TG) * (b0_ref[...] + b1_ref[...])


def moe_combine(pos0, pos1, ys, x, mods, layer, gate_col):
    return pl.pallas_call(
        _combine_kernel,
        grid_spec=pltpu.PrefetchScalarGridSpec(
            num_scalar_prefetch=2,
            grid=(T // TG,),
            in_specs=[
                pl.BlockSpec(memory_space=pl.ANY),
                pl.BlockSpec((TG, D), lambda i, p0, p1: (i, 0)),
                pl.BlockSpec((None, N_MOD, D), lambda i, p0, p1: (layer, 0, gate_col)),
            ],
            out_specs=pl.BlockSpec((TG, D), lambda i, p0, p1: (i, 0)),
            scratch_shapes=[pltpu.VMEM((TG, D), F32), pltpu.VMEM((TG, D), F32), pltpu.SemaphoreType.DMA((2,))],
        ),
        out_shape=jax.ShapeDtypeStruct((T, D), F32),
        compiler_params=_params("arbitrary"),
        name="moe_combine",
    )(pos0, pos1, ys, x, mods)


def moe_plan(idx, gates, tokens):
    n_tok = tokens.shape[0]
    n_pairs = 2 * n_tok
    n_tiles = n_pairs // TM_MOE + N_EXP
    n_slots = n_tiles * TM_MOE
    e = idx[:, tokens].reshape(-1)
    w = gates[:, tokens].reshape(-1)
    tok = jnp.tile(jnp.asarray(tokens, jnp.int32), 2)
    onehot = (e[:, None] == jnp.arange(N_EXP)[None, :]).astype(jnp.int32)
    rank = jnp.sum((jnp.cumsum(onehot, axis=0) - onehot) * onehot, axis=1)
    counts = jnp.sum(onehot, axis=0)
    tiles = (counts + TM_MOE - 1) // TM_MOE
    tile_end = jnp.cumsum(tiles)
    start = (tile_end - tiles) * TM_MOE
    dest = start[e] + rank
    slot_token = jnp.zeros((n_slots,), jnp.int32).at[dest].set(tok)
    slot_gate = jnp.zeros((n_slots,), F32).at[dest].set(w)
    tile_ids = jnp.arange(n_tiles)
    last_used = jnp.minimum(tile_ids, tile_end[-1] - 1)
    tile_expert = jnp.sum(tile_end[None, :] <= last_used[:, None], axis=1).astype(jnp.int32)
    tile_rows = jnp.clip((start + counts)[tile_expert] - tile_ids * TM_MOE, 0, TM_MOE).astype(jnp.int32)
    pos = jnp.zeros((2, T), jnp.int32).at[:, tokens].set(dest.reshape(2, n_tok))
    return slot_token, slot_gate.reshape(n_slots, 1), tile_expert, tile_rows, pos[0], pos[1]


def _final_norm_kernel(x_ref, g_ref, o_ref):
    x = x_ref[...]
    o_ref[...] = x * lax.rsqrt(jnp.mean(x * x, axis=-1, keepdims=True) + EPS) * g_ref[...]


def final_norm(x, g):
    per_batch = SEQ // TM_MOD
    tiles = ROWS // TM_MOD
    first = CTX // TM_MOD
    out = pl.pallas_call(
        _final_norm_kernel,
        grid=(BATCH * per_batch,),
        in_specs=[
            pl.BlockSpec((TM_MOD, D), lambda i: ((i // per_batch) * tiles + first + i % per_batch, 0)),
            pl.BlockSpec((1, D), lambda i: (0, 0)),
        ],
        out_specs=pl.BlockSpec((TM_MOD, D), lambda i: (i, 0)),
        out_shape=jax.ShapeDtypeStruct((BATCH * SEQ, D), F32),
        compiler_params=_params("arbitrary"),
        name="final_norm",
    )(x, g.reshape(1, D))
    return out.reshape(BATCH, SEQ, D)


def _pad_cols(w, n):
    return jnp.pad(w, ((0, 0), (0, n - w.shape[1])))


def kernel(x, c, ctx, c_ctx, ada_w, ada_b, norm_mix_g, norm_ffn_g, gla_w_in, gla_wg_fwd, gla_bg_fwd, gla_wg_bwd, gla_bg_bwd, gla_norm_g, gla_w_out, nat_w_in, nat_rpb, nat_w_out, ffn_w_gate, ffn_w_up, ffn_w_down, moe_router, moe_w_gate, moe_w_up, moe_w_down, final_norm_g):
    xs = jnp.concatenate([ctx, x], axis=1).reshape(T, D)
    c8 = jnp.concatenate([c, c_ctx[None, :], jnp.zeros((N_MOD - BATCH - 1, D), F32)], axis=0)
    mods = ada_tables(c8, ada_w, ada_b)
    cos_t, sin_t = rope_tables()
    router_pad = jnp.pad(moe_router, ((0, 0), (0, 0), (0, LANES - N_EXP)))
    all_tokens = np.arange(T)
    latent_tokens = all_tokens.reshape(BATCH, ROWS)[:, CTX:].reshape(-1)

    for i in range(DEPTH):
        j = i // 2
        last = i == DEPTH - 1
        h = modulate(xs, norm_mix_g, mods, i, 0, 1)
        if i % 2 == 0:
            qk = mm_rope(h, gla_w_in, j, cos_t, sin_t, tm=1152)
            vr = mm_plain(h, gla_w_in, j, 2 * GLA_QK, 2 * D, BF16, tm=1152, tn=1024)
            z0 = 2 * GLA_QK + 2 * D
            wz_pad = _pad_cols(gla_w_in[j, :, z0:z0 + 2 * GLA_RANK], LANES)
            wf_pad = jnp.pad(gla_wg_fwd[j], ((0, LANES - GLA_RANK), (0, 0)))
            wb_pad = jnp.pad(gla_wg_bwd[j], ((GLA_RANK, LANES - 2 * GLA_RANK), (0, 0)))
            gf, gb = gla_gates(h, wz_pad, wf_pad, gla_bg_fwd, wb_pad, gla_bg_bwd, j, tm=1152)
            y = gla_scan(qk, vr, gf, gb, gla_norm_g, j)
            xs = mm_resid(y, gla_w_out, j, xs, mods, i, 2, tm=1152, tn=1024)
        else:
            qkv = mm_plain(h, nat_w_in, j, 0, 3 * D, BF16, tm=1152, tn=1024,
                           scale=NAT_DH ** -0.5, n_scaled=D // 1024)
            y = nat_attention(qkv, nat_bias_table(nat_rpb[j]), not last)
            xs = mm_resid(y, nat_w_out, j, xs, mods, i, 2, tm=1152, tn=1024)
        if i % 2 == 0:
            h2 = modulate(xs, norm_ffn_g, mods, i, 3, 4)
            act = mm_swiglu(h2, ffn_w_gate, ffn_w_up, j, tm=1152, tn=512)
            xs = mm_resid(act, ffn_w_down, j, xs, mods, i, 5, tm=576, tn=512)
        else:
            h2, idx, gates = modulate_route(xs, norm_ffn_g, mods, i, 3, 4, router_pad, j)
            tokens = latent_tokens if last else all_tokens
            slot_token, slot_gate, tile_expert, tile_rows, pos0, pos1 = moe_plan(idx, gates, tokens)
            xg = moe_gather(slot_token, tile_rows, h2)
            act = moe_up(tile_expert, tile_rows, xg, moe_w_gate, moe_w_up, j)
            ys = moe_down(tile_expert, tile_rows, act, moe_w_down, slot_gate, j)
            xs = moe_combine(pos0, pos1, ys, xs, mods, i, 5)
    return final_norm(xs, final_norm_g)
```

```python
import functools

import numpy as np
import jax
import jax.numpy as jnp
from jax import lax
from jax.experimental import pallas as pl
from jax.experimental.pallas import tpu as pltpu

F32 = jnp.float32
BF16 = jnp.bfloat16
HIGHEST = lax.Precision.HIGHEST

D = 2048
BATCH = 4
SEQ = 2048
CTX = 256
ROWS = CTX + SEQ
T = BATCH * ROWS
DEPTH = 4
GRID_W = 64
GRID_H = SEQ // GRID_W
EPS = 1e-6
ROPE_BASE = 10000.0
CTX_GROUP = BATCH
N_MOD = 8

GLA_H = 4
GLA_DK = 256
GLA_DV = 512
GLA_QK = GLA_H * GLA_DK
GLA_RANK = 16
GLA_TAU = 16.0
GLA_C = 64
GLA_SUB = 16

NAT_H = 16
NAT_DH = 128
WIN_R = 8
WIN_C = 16
NAT_ROWS_PER_STEP = 8

D_FF = 5632
N_EXP = 8

LANES = 128
VMEM_LIMIT = 56 * 1024 * 1024

TM_MOD = 256
TM_MODULATE = 768
TM_UP = 1024
TM_DOWN = 512
TM_PART = 256
TF_MOE = 512
TN_MOE_DOWN = 512
TG = 256
DMA_UNROLL = 8


def _params(*sem):
    return pltpu.CompilerParams(dimension_semantics=sem, vmem_limit_bytes=VMEM_LIMIT)


def _dot(a, b):
    return jnp.dot(a, b, preferred_element_type=F32)


def _dot_nt(a, b):
    return lax.dot_general(a, b, (((1,), (1,)), ((), ())), preferred_element_type=F32)


def _dot_tn(a, b):
    return lax.dot_general(a, b, (((0,), (0,)), ((), ())), preferred_element_type=F32)


def _silu(v):
    return v * jax.nn.sigmoid(v)


def _row_gate(tab_ref, m, tm):
    start = m * tm
    b = start // ROWS
    rows = start % ROWS + lax.broadcasted_iota(jnp.int32, (tm, 1), 0)
    per_batch = tab_ref[pl.ds(b, 1), :]
    per_ctx = tab_ref[CTX_GROUP:CTX_GROUP + 1, :]
    return jnp.where(rows < CTX, per_ctx, per_batch)


def _ada_kernel(c_ref, w_ref, b_ref, o_ref):
    a = _silu(c_ref[...]).astype(BF16)
    o_ref[...] = _dot(a, w_ref[...].astype(BF16)) + b_ref[...]


def ada_tables(c8, ada_w, ada_b):
    tn = 1024
    return pl.pallas_call(
        _ada_kernel,
        grid=(DEPTH, 6 * D // tn),
        in_specs=[
            pl.BlockSpec((N_MOD, D), lambda l, n: (0, 0)),
            pl.BlockSpec((None, D, tn), lambda l, n: (l, 0, n)),
            pl.BlockSpec((None, 1, tn), lambda l, n: (l, 0, n)),
        ],
        out_specs=pl.BlockSpec((None, N_MOD, tn), lambda l, n: (l, 0, n)),
        out_shape=jax.ShapeDtypeStruct((DEPTH, N_MOD, 6 * D), F32),
        compiler_params=_params("arbitrary", "arbitrary"),
        name="ada_tables",
    )(c8, ada_w, ada_b.reshape(DEPTH, 1, 6 * D))


def _modulated(x_ref, g_ref, sh_ref, sc_ref):
    x = x_ref[...]
    tm = x.shape[0]
    y = x * lax.rsqrt(jnp.mean(x * x, axis=-1, keepdims=True) + EPS) * g_ref[...]
    i = pl.program_id(0)
    return y * (1.0 + _row_gate(sc_ref, i, tm)) + _row_gate(sh_ref, i, tm)


def _modulate_kernel(x_ref, g_ref, sh_ref, sc_ref, o_ref):
    o_ref[...] = _modulated(x_ref, g_ref, sh_ref, sc_ref).astype(o_ref.dtype)


def _mod_specs(layer, sh_col, sc_col, tm):
    return [
        pl.BlockSpec((tm, D), lambda i: (i, 0)),
        pl.BlockSpec((None, 1, D), lambda i: (layer, 0, 0)),
        pl.BlockSpec((None, N_MOD, D), lambda i: (layer, 0, sh_col)),
        pl.BlockSpec((None, N_MOD, D), lambda i: (layer, 0, sc_col)),
    ]


def modulate(x, norm_g, mods, layer, sh_col, sc_col):
    tm = TM_MODULATE
    return pl.pallas_call(
        _modulate_kernel,
        grid=(T // tm,),
        in_specs=_mod_specs(layer, sh_col, sc_col, tm),
        out_specs=pl.BlockSpec((tm, D), lambda i: (i, 0)),
        out_shape=jax.ShapeDtypeStruct((T, D), BF16),
        compiler_params=_params("arbitrary"),
        name="modulate",
    )(x, norm_g.reshape(DEPTH, 1, D), mods, mods)


def _modulate_route_kernel(x_ref, g_ref, sh_ref, sc_ref, rt_ref, h_ref, idx_ref, gate_ref, rank_ref, cnt_ref,
                           base_ref, *, latent_only):
    i = pl.program_id(0)

    @pl.when(i == 0)
    def _():
        base_ref[...] = jnp.zeros_like(base_ref)

    h = _modulated(x_ref, g_ref, sh_ref, sc_ref)
    h_ref[...] = h.astype(h_ref.dtype)
    logits = jnp.dot(h, rt_ref[...], precision=HIGHEST, preferred_element_type=F32)
    lt = logits.T[:N_EXP]
    eid = lax.broadcasted_iota(jnp.int32, lt.shape, 0)
    m1 = jnp.max(lt, axis=0, keepdims=True)
    i1 = jnp.min(jnp.where(lt == m1, eid, N_EXP), axis=0, keepdims=True)
    lt2 = jnp.where(eid == i1, -jnp.inf, lt)
    m2 = jnp.max(lt2, axis=0, keepdims=True)
    i2 = jnp.min(jnp.where(lt2 == m2, eid, N_EXP), axis=0, keepdims=True)
    e = jnp.exp(m2 - m1)
    idx_ref[...] = jnp.concatenate([i1, i2], axis=0)
    gate_ref[...] = jnp.concatenate([1.0 / (1.0 + e), e / (1.0 + e)], axis=0)

    routed = 1.0
    if latent_only:
        routed = jnp.where(i % (ROWS // TM_MOD) < CTX // TM_MOD, 0.0, 1.0)
    oh1 = jnp.where(eid == i1, routed, 0.0)
    oh2 = jnp.where(eid == i2, routed, 0.0)
    both = oh1 + oh2
    tj = lax.broadcasted_iota(jnp.int32, (TM_MOD, TM_MOD), 0)
    tt = lax.broadcasted_iota(jnp.int32, (TM_MOD, TM_MOD), 1)
    before = _dot(both.astype(BF16), jnp.where(tj < tt, 1.0, 0.0).astype(BF16))
    seen = base_ref[:, 0:1] + before
    r1 = jnp.sum(oh1 * seen, axis=0, keepdims=True)
    r2 = jnp.sum(oh2 * seen, axis=0, keepdims=True)
    rank_ref[...] = jnp.concatenate([r1, r2], axis=0).astype(jnp.int32)
    total = base_ref[...] + jnp.sum(both, axis=1, keepdims=True)
    base_ref[...] = total
    cnt_ref[...] = total.astype(jnp.int32)


def modulate_route(x, norm_g, mods, layer, sh_col, sc_col, router_pad, j, latent_only):
    pairs = pl.BlockSpec((2, TM_MOD), lambda i: (0, i))
    return pl.pallas_call(
        functools.partial(_modulate_route_kernel, latent_only=latent_only),
        grid=(T // TM_MOD,),
        in_specs=_mod_specs(layer, sh_col, sc_col, TM_MOD) + [
            pl.BlockSpec((None, D, LANES), lambda i: (j, 0, 0)),
        ],
        out_specs=[
            pl.BlockSpec((TM_MOD, D), lambda i: (i, 0)),
            pairs, pairs, pairs,
            pl.BlockSpec((N_EXP, LANES), lambda i: (0, 0)),
        ],
        out_shape=[
            jax.ShapeDtypeStruct((T, D), BF16),
            jax.ShapeDtypeStruct((2, T), jnp.int32),
            jax.ShapeDtypeStruct((2, T), F32),
            jax.ShapeDtypeStruct((2, T), jnp.int32),
            jax.ShapeDtypeStruct((N_EXP, LANES), jnp.int32),
        ],
        scratch_shapes=[pltpu.VMEM((N_EXP, LANES), F32)],
        compiler_params=_params("arbitrary"),
        name="modulate_route",
    )(x, norm_g.reshape(DEPTH, 1, D), mods, mods, router_pad)


def _cast_weight(w_ref, wb_ref):
    @pl.when(pl.program_id(1) == 0)
    def _():
        wb_ref[...] = w_ref[...].astype(BF16)


def _mm_plain_kernel(a_ref, w_ref, o_ref, wb_ref, *, scale, n_scaled):
    _cast_weight(w_ref, wb_ref)
    acc = _dot(a_ref[...], wb_ref[...])
    if n_scaled:
        acc = acc * jnp.where(pl.program_id(0) < n_scaled, scale, 1.0)
    o_ref[...] = acc.astype(o_ref.dtype)


def mm_plain(a, w, layer, col0, n_out, out_dtype, tm, tn, scale=1.0, n_scaled=0):
    k = a.shape[1]
    c0 = col0 // tn
    return pl.pallas_call(
        functools.partial(_mm_plain_kernel, scale=scale, n_scaled=n_scaled),
        grid=(n_out // tn, T // tm),
        in_specs=[
            pl.BlockSpec((tm, k), lambda n, m: (m, 0)),
            pl.BlockSpec((None, k, tn), lambda n, m: (layer, 0, n + c0)),
        ],
        out_specs=pl.BlockSpec((tm, tn), lambda n, m: (m, n)),
        out_shape=jax.ShapeDtypeStruct((T, n_out), out_dtype),
        scratch_shapes=[pltpu.VMEM((k, tn), BF16)],
        compiler_params=_params("arbitrary", "arbitrary"),
        name="mm_plain",
    )(a, w)


def _mm_rope_kernel(a_ref, w_ref, cos_ref, sin_ref, o_ref, wb_ref, *, tn):
    _cast_weight(w_ref, wb_ref)
    acc = _dot(a_ref[...], wb_ref[...])
    scale = jnp.where(pl.program_id(0) == 0, GLA_DK ** -0.5, 1.0)
    for s in range(tn // LANES):
        xs = acc[:, s * LANES:(s + 1) * LANES]
        t = (s % 2) * LANES
        rot = xs * cos_ref[:, t:t + LANES] + pltpu.roll(xs, LANES // 2, 1) * sin_ref[:, t:t + LANES]
        o_ref[:, s * LANES:(s + 1) * LANES] = rot * scale


def mm_rope(a, w, layer, cos_t, sin_t, tm):
    tn = GLA_QK
    per_batch = ROWS // tm
    return pl.pallas_call(
        functools.partial(_mm_rope_kernel, tn=tn),
        grid=(2, T // tm),
        in_specs=[
            pl.BlockSpec((tm, D), lambda n, m: (m, 0)),
            pl.BlockSpec((None, D, tn), lambda n, m: (layer, 0, n)),
            pl.BlockSpec((tm, GLA_DK), lambda n, m: (m % per_batch, 0)),
            pl.BlockSpec((tm, GLA_DK), lambda n, m: (m % per_batch, 0)),
        ],
        out_specs=pl.BlockSpec((tm, tn), lambda n, m: (m, n)),
        out_shape=jax.ShapeDtypeStruct((T, 2 * GLA_QK), F32),
        scratch_shapes=[pltpu.VMEM((D, tn), BF16)],
        compiler_params=_params("arbitrary", "arbitrary"),
        name="mm_rope",
    )(a, w, cos_t, sin_t)


def _mm_resid_kernel(a_ref, w_ref, x_ref, gt_ref, o_ref, wb_ref, *, tm):
    _cast_weight(w_ref, wb_ref)
    acc = _dot(a_ref[...], wb_ref[...])
    o_ref[...] = x_ref[...] + _row_gate(gt_ref, pl.program_id(1), tm) * acc


def mm_resid(a, w, layer_w, x, mods, layer, gate_col, tm, tn):
    k = a.shape[1]
    g0 = gate_col * D // tn
    return pl.pallas_call(
        functools.partial(_mm_resid_kernel, tm=tm),
        grid=(D // tn, T // tm),
        in_specs=[
            pl.BlockSpec((tm, k), lambda n, m: (m, 0)),
            pl.BlockSpec((None, k, tn), lambda n, m: (layer_w, 0, n)),
            pl.BlockSpec((tm, tn), lambda n, m: (m, n)),
            pl.BlockSpec((None, N_MOD, tn), lambda n, m: (layer, 0, g0 + n)),
        ],
        out_specs=pl.BlockSpec((tm, tn), lambda n, m: (m, n)),
        out_shape=jax.ShapeDtypeStruct((T, D), F32),
        scratch_shapes=[pltpu.VMEM((k, tn), BF16)],
        compiler_params=_params("arbitrary", "arbitrary"),
        name="mm_resid",
    )(a, w, x, mods)


def _mm_swiglu_kernel(a_ref, wg_ref, wu_ref, o_ref, wgb_ref, wub_ref):
    _cast_weight(wg_ref, wgb_ref)
    _cast_weight(wu_ref, wub_ref)
    a = a_ref[...]
    o_ref[...] = (_silu(_dot(a, wgb_ref[...])) * _dot(a, wub_ref[...])).astype(o_ref.dtype)


def mm_swiglu(a, wg, wu, layer, tm, tn):
    return pl.pallas_call(
        _mm_swiglu_kernel,
        grid=(D_FF // tn, T // tm),
        in_specs=[
            pl.BlockSpec((tm, D), lambda n, m: (m, 0)),
            pl.BlockSpec((None, D, tn), lambda n, m: (layer, 0, n)),
            pl.BlockSpec((None, D, tn), lambda n, m: (layer, 0, n)),
        ],
        out_specs=pl.BlockSpec((tm, tn), lambda n, m: (m, n)),
        out_shape=jax.ShapeDtypeStruct((T, D_FF), BF16),
        scratch_shapes=[pltpu.VMEM((D, tn), BF16), pltpu.VMEM((D, tn), BF16)],
        compiler_params=_params("arbitrary", "arbitrary"),
        name="mm_swiglu",
    )(a, wg, wu)


def _log_sigmoid(v):
    return jnp.minimum(v, 0.0) - jnp.log(1.0 + jnp.exp(-jnp.abs(v)))


def _gla_gate_kernel(h_ref, wz_ref, wf_ref, bf_ref, wb_ref, bb_ref, gf_ref, gb_ref):
    z = _dot(h_ref[...], wz_ref[...].astype(BF16))
    af = jnp.dot(z, wf_ref[...], precision=HIGHEST, preferred_element_type=F32) + bf_ref[...]
    ab = jnp.dot(z, wb_ref[...], precision=HIGHEST, preferred_element_type=F32) + bb_ref[...]
    gf_ref[...] = _log_sigmoid(af) / GLA_TAU
    gb_ref[...] = _log_sigmoid(ab) / GLA_TAU


def gla_gates(h, wz_pad, wf_pad, bg_f, wb_pad, bg_b, j, tm):
    row = lambda i: (i, 0)
    return pl.pallas_call(
        _gla_gate_kernel,
        grid=(T // tm,),
        in_specs=[
            pl.BlockSpec((tm, D), row),
            pl.BlockSpec((D, LANES), lambda i: (0, 0)),
            pl.BlockSpec((LANES, GLA_QK), lambda i: (0, 0)),
            pl.BlockSpec((None, 1, GLA_QK), lambda i: (j, 0, 0)),
            pl.BlockSpec((LANES, GLA_QK), lambda i: (0, 0)),
            pl.BlockSpec((None, 1, GLA_QK), lambda i: (j, 0, 0)),
        ],
        out_specs=[pl.BlockSpec((tm, GLA_QK), row), pl.BlockSpec((tm, GLA_QK), row)],
        out_shape=[jax.ShapeDtypeStruct((T, GLA_QK), F32)] * 2,
        compiler_params=_params("arbitrary"),
        name="gla_gates",
    )(h, wz_pad, wf_pad, bg_f.reshape(-1, 1, GLA_QK), wb_pad, bg_b.reshape(-1, 1, GLA_QK))


def _gla_chunks(streams):
    c, sub, nsub = GLA_C, GLA_SUB, GLA_C // GLA_SUB
    ri = lax.broadcasted_iota(jnp.int32, (c, c), 0)
    ci = lax.broadcasted_iota(jnp.int32, (c, c), 1)
    lane = lax.broadcasted_iota(jnp.int32, (sub, c), 1)
    srow = lax.broadcasted_iota(jnp.int32, (sub, 1), 0)

    bcs = []
    for q, k, v, g, st, rev in streams:
        tri = (ci >= ri if rev else ci <= ri).astype(F32)
        bcs.append(jnp.dot(tri, g, precision=HIGHEST, preferred_element_type=F32))

    partial = []
    for (q, k, v, g, st, rev), bc in zip(streams, bcs):
        b_end = bc[0:1] if rev else bc[c - 1:c]
        o = _dot_nt((q * jnp.exp(bc)).astype(BF16), st.astype(BF16))
        st_new = st * jnp.exp(b_end) + _dot_tn(v, (k * jnp.exp(b_end - bc)).astype(BF16))
        off = []
        for bi in range(nsub):
            lo, hi = sub * bi, sub * (bi + 1)
            q_i, b_i = q[lo:hi], bc[lo:hi]
            if not rev and bi > 0:
                ref = bc[lo - 1:lo]
                kt = jnp.concatenate([k[:lo] * jnp.exp(ref - bc[:lo]), jnp.zeros((c - lo, GLA_DK), F32)], axis=0)
                off.append(_dot_nt((q_i * jnp.exp(b_i - ref)).astype(BF16), kt.astype(BF16)))
            elif rev and bi < nsub - 1:
                ref = bc[hi:hi + 1]
                kt = jnp.concatenate([jnp.zeros((hi, GLA_DK), F32), k[hi:] * jnp.exp(ref - bc[hi:])], axis=0)
                off.append(_dot_nt((q_i * jnp.exp(b_i - ref)).astype(BF16), kt.astype(BF16)))
            else:
                off.append(jnp.zeros((sub, c), F32))
        partial.append((o, st_new, off))

    intra = []
    for (q, k, v, g, st, rev), bc, (o, st_new, off) in zip(streams, bcs, partial):
        blocks = []
        for bi in range(nsub):
            lo, hi = sub * bi, sub * (bi + 1)
            q_i, k_i, b_i = q[lo:hi], k[lo:hi], bc[lo:hi]
            a_i = off[bi]
            for jj in range(sub):
                t = q_i * k_i[jj:jj + 1] * jnp.exp(b_i - b_i[jj:jj + 1])
                s = jnp.sum(t, axis=-1, keepdims=True)
                seen = srow <= jj if rev else srow >= jj
                a_i = jnp.where(lane == lo + jj, jnp.where(seen, s, 0.0), a_i)
            blocks.append(a_i)
        intra.append(jnp.concatenate(blocks, axis=0).astype(BF16))

    return [(o + _dot(a, v), st_new) for (q, k, v, g, st, rev), (o, st_new, off), a in zip(streams, partial, intra)]


def _gla_kernel(q_ref, k_ref, v_ref, r_ref, gf_ref, gb_ref, ng_ref, y_ref, stf_ref, stb_ref, of_ref, ob_ref):
    nch = ROWS // GLA_C
    nctx = CTX // GLA_C

    def chunk_rows(ch):
        return pl.ds(pl.multiple_of(ch * GLA_C, GLA_C), GLA_C)

    def stream(rows, g_ref, st_ref, rev):
        return q_ref[rows, :], k_ref[rows, :], v_ref[rows, :], g_ref[rows, :], st_ref[...], rev

    stf_ref[...] = jnp.zeros_like(stf_ref)
    stb_ref[...] = jnp.zeros_like(stb_ref)

    def scan(s, carry):
        rows_f = chunk_rows(s)
        rows_b = chunk_rows(jnp.where(s < nctx, nctx - 1 - s, nch + nctx - 1 - s))
        (o_f, st_f), (o_b, st_b) = _gla_chunks([stream(rows_f, gf_ref, stf_ref, False),
                                                stream(rows_b, gb_ref, stb_ref, True)])
        of_ref[rows_f, :] = o_f
        ob_ref[rows_b, :] = o_b
        stf_ref[...] = st_f
        stb_ref[...] = st_b
        return carry

    lax.fori_loop(0, nch, scan, 0)

    def readout(ch, carry):
        rows = chunk_rows(ch)
        o = of_ref[rows, :] + ob_ref[rows, :]
        on = o * lax.rsqrt(jnp.mean(o * o, axis=-1, keepdims=True) + EPS) * ng_ref[...]
        y_ref[rows, :] = (on * _silu(r_ref[rows, :].astype(F32))).astype(y_ref.dtype)
        return carry

    lax.fori_loop(0, nch, readout, 0)


def gla_scan(qk, vr, gf, gb, norm_g, j):
    return pl.pallas_call(
        _gla_kernel,
        grid=(BATCH, GLA_H),
        in_specs=[
            pl.BlockSpec((ROWS, GLA_DK), lambda b, h: (b, h)),
            pl.BlockSpec((ROWS, GLA_DK), lambda b, h: (b, GLA_H + h)),
            pl.BlockSpec((ROWS, GLA_DV), lambda b, h: (b, h)),
            pl.BlockSpec((ROWS, GLA_DV), lambda b, h: (b, GLA_H + h)),
            pl.BlockSpec((ROWS, GLA_DK), lambda b, h: (b, h)),
            pl.BlockSpec((ROWS, GLA_DK), lambda b, h: (b, h)),
            pl.BlockSpec((None, 1, GLA_DV), lambda b, h: (j, 0, 0)),
        ],
        out_specs=pl.BlockSpec((ROWS, GLA_DV), lambda b, h: (b, h)),
        out_shape=jax.ShapeDtypeStruct((T, D), BF16),
        scratch_shapes=[pltpu.VMEM((GLA_DV, GLA_DK), F32), pltpu.VMEM((GLA_DV, GLA_DK), F32),
                        pltpu.VMEM((ROWS, GLA_DV), F32), pltpu.VMEM((ROWS, GLA_DV), F32)],
        compiler_params=_params("arbitrary", "arbitrary"),
        name="gla_scan",
    )(qk, qk, vr, vr, gf, gb, norm_g.reshape(-1, 1, GLA_DV))


def rope_tables():
    half = GLA_DK // 2
    freqs = ROPE_BASE ** (-np.arange(0, half, 2, dtype=np.float32) / half)
    t = np.arange(SEQ)
    pos = np.stack([t // GRID_W, t % GRID_W], axis=-1).astype(np.float32)
    ang = pos[:, :, None] * freqs
    cos, sin = np.cos(ang), np.sin(ang)
    cos_t = np.concatenate([cos, cos], axis=-1).reshape(SEQ, GLA_DK)
    sin_t = np.concatenate([-sin, sin], axis=-1).reshape(SEQ, GLA_DK)
    cos_t = np.concatenate([np.ones((CTX, GLA_DK), np.float32), cos_t], axis=0)
    sin_t = np.concatenate([np.zeros((CTX, GLA_DK), np.float32), sin_t], axis=0)
    return jnp.asarray(cos_t, F32), jnp.asarray(sin_t, F32)


def _nat_kernel(q_ref, k_ref, v_ref, bias_ref, y_ref, *, need_ctx):
    kc = k_ref[0:CTX, :]
    vc = v_ref[0:CTX, :]
    if need_ctx:
        s = _dot_nt(q_ref[0:CTX, :], kc)
        p = jnp.exp(s - jnp.max(s, axis=-1, keepdims=True))
        o = _dot(p.astype(BF16), vc) / jnp.sum(p, axis=-1, keepdims=True)
        y_ref[0:CTX, :] = o.astype(y_ref.dtype)
    else:
        y_ref[0:CTX, :] = jnp.zeros((CTX, NAT_DH), y_ref.dtype)
    n_keys = WIN_R * GRID_W

    def scores(r):
        rs = jnp.clip(r - WIN_R // 2, 0, GRID_H - WIN_R)
        q0 = pl.multiple_of(CTX + r * GRID_W, GRID_W)
        k0 = pl.multiple_of(CTX + rs * GRID_W, GRID_W)
        q = q_ref[pl.ds(q0, GRID_W), :]
        s_lat = _dot_nt(q, k_ref[pl.ds(k0, n_keys), :]) + bias_ref[r - rs]
        s_ctx = _dot_nt(q, kc)
        return q0, k0, s_lat, s_ctx

    def probs(q0, k0, s_lat, s_ctx):
        m = jnp.maximum(jnp.max(s_lat, axis=-1, keepdims=True), jnp.max(s_ctx, axis=-1, keepdims=True))
        p_lat = jnp.exp(s_lat - m)
        p_ctx = jnp.exp(s_ctx - m)
        denom = jnp.sum(p_lat, axis=-1, keepdims=True) + jnp.sum(p_ctx, axis=-1, keepdims=True)
        return q0, k0, p_lat.astype(BF16), p_ctx.astype(BF16), denom

    def values(q0, k0, p_lat, p_ctx, denom):
        o = _dot(p_lat, v_ref[pl.ds(k0, n_keys), :]) + _dot(p_ctx, vc)
        return q0, (o / denom).astype(y_ref.dtype)

    def body(it, carry):
        rows = [it * NAT_ROWS_PER_STEP + u for u in range(NAT_ROWS_PER_STEP)]
        outs = [values(*pr) for pr in [probs(*sc) for sc in [scores(r) for r in rows]]]
        for q0, o in outs:
            y_ref[pl.ds(q0, GRID_W), :] = o
        return carry

    lax.fori_loop(0, GRID_H // NAT_ROWS_PER_STEP, body, 0)


def nat_attention(qkv, bias_tbl, need_ctx):
    return pl.pallas_call(
        functools.partial(_nat_kernel, need_ctx=need_ctx),
        grid=(NAT_H, BATCH),
        in_specs=[
            pl.BlockSpec((ROWS, NAT_DH), lambda h, b: (b, h)),
            pl.BlockSpec((ROWS, NAT_DH), lambda h, b: (b, NAT_H + h)),
            pl.BlockSpec((ROWS, NAT_DH), lambda h, b: (b, 2 * NAT_H + h)),
            pl.BlockSpec((None, WIN_R, GRID_W, WIN_R * GRID_W), lambda h, b: (h, 0, 0, 0)),
        ],
        out_specs=pl.BlockSpec((ROWS, NAT_DH), lambda h, b: (b, h)),
        out_shape=jax.ShapeDtypeStruct((T, D), BF16),
        compiler_params=_params("arbitrary", "arbitrary"),
        name="nat_attention",
    )(qkv, qkv, qkv, bias_tbl)


def nat_bias_table(rpb):
    off = np.arange(WIN_R)[:, None]
    dr = np.arange(WIN_R)[None, :] - off + (WIN_R - 1)
    qc = np.arange(GRID_W)[:, None]
    kcol = np.arange(GRID_W)[None, :]
    start = np.clip(qc - WIN_C // 2, 0, GRID_W - WIN_C)
    in_win = (kcol >= start) & (kcol < start + WIN_C)
    dc = np.clip(kcol - qc + WIN_C - 1, 0, 2 * WIN_C - 2)
    tbl = rpb[:, dr][:, :, :, dc]
    tbl = jnp.where(in_win, tbl, -jnp.inf)
    return jnp.transpose(tbl, (0, 1, 3, 2, 4)).reshape(NAT_H, WIN_R, GRID_W, WIN_R * GRID_W)


def _dispatch_kernel(d0_ref, d1_ref, h_ref, zero_ref, o_ref, sem, *, latent_only):
    del zero_ref
    i = pl.program_id(0)
    base = i * TG

    def copies(r):
        row = h_ref.at[pl.ds(r, 1)]
        return (pltpu.make_async_copy(row, o_ref.at[pl.ds(d0_ref[base + r], 1)], sem.at[0]),
                pltpu.make_async_copy(row, o_ref.at[pl.ds(d1_ref[base + r], 1)], sem.at[1]))

    def start(r, carry):
        c0, c1 = copies(r)
        c0.start()
        c1.start()
        return carry

    def run():
        lax.fori_loop(0, TG, start, 0, unroll=DMA_UNROLL)
        for s in range(2):
            pltpu.make_async_copy(h_ref, o_ref.at[pl.ds(0, TG)], sem.at[s]).wait()

    if latent_only:
        pl.when(i % (ROWS // TG) >= CTX // TG)(run)
    else:
        run()


def moe_dispatch(dest0, dest1, h_words, n_slots, latent_only):
    words = h_words.shape[1]
    return pl.pallas_call(
        functools.partial(_dispatch_kernel, latent_only=latent_only),
        grid_spec=pltpu.PrefetchScalarGridSpec(
            num_scalar_prefetch=2,
            grid=(T // TG,),
            in_specs=[pl.BlockSpec((TG, words), lambda i, d0, d1: (i, 0)), pl.BlockSpec(memory_space=pl.ANY)],
            out_specs=pl.BlockSpec(memory_space=pl.ANY),
            scratch_shapes=[pltpu.SemaphoreType.DMA((2,))],
        ),
        out_shape=jax.ShapeDtypeStruct((n_slots, words), jnp.uint32),
        input_output_aliases={3: 0},
        compiler_params=_params("arbitrary"),
        name="moe_dispatch",
    )(dest0, dest1, h_words, jnp.zeros((n_slots, words), jnp.uint32))


def _expert_changed(te_ref, i):
    return (i == 0) | (te_ref[i] != te_ref[jnp.maximum(i - 1, 0)])


def _on_occupied_prefix(n_rows, tile, o_ref, compute):
    for parts in range(tile // TM_PART + 1):
        p = parts * TM_PART

        @pl.when((n_rows > p - TM_PART) & (n_rows <= p))
        def _():
            if p > 0:
                o_ref[0:p, :] = compute(p)
            if p < tile:
                o_ref[p:tile, :] = jnp.zeros((tile - p, o_ref.shape[1]), o_ref.dtype)


def _moe_up_kernel(te_ref, tr_ref, a_ref, wg_ref, wu_ref, o_ref, wgb_ref, wub_ref):
    i = pl.program_id(1)

    @pl.when(_expert_changed(te_ref, i))
    def _():
        wgb_ref[...] = wg_ref[...].astype(BF16)
        wub_ref[...] = wu_ref[...].astype(BF16)

    def compute(p):
        a = a_ref[0:p, :]
        return (_silu(_dot(a, wgb_ref[...])) * _dot(a, wub_ref[...])).astype(o_ref.dtype)

    _on_occupied_prefix(tr_ref[i], TM_UP, o_ref, compute)


def moe_up(tile_expert, tile_rows, xs, wg, wu, j):
    n_slots = xs.shape[0]
    w_spec = pl.BlockSpec((None, None, D, TF_MOE), lambda f, i, te, tr: (j, te[i], 0, f))
    return pl.pallas_call(
        _moe_up_kernel,
        grid_spec=pltpu.PrefetchScalarGridSpec(
            num_scalar_prefetch=2,
            grid=(D_FF // TF_MOE, n_slots // TM_UP),
            in_specs=[pl.BlockSpec((TM_UP, D), lambda f, i, te, tr: (i, 0)), w_spec, w_spec],
            out_specs=pl.BlockSpec((TM_UP, TF_MOE), lambda f, i, te, tr: (i, f)),
            scratch_shapes=[pltpu.VMEM((D, TF_MOE), BF16), pltpu.VMEM((D, TF_MOE), BF16)],
        ),
        out_shape=jax.ShapeDtypeStruct((n_slots, D_FF), BF16),
        compiler_params=_params("arbitrary", "arbitrary"),
        name="moe_up",
    )(tile_expert, tile_rows, xs, wg, wu)


def _moe_down_kernel(te_ref, tr_ref, a_ref, w_ref, o_ref, wb_ref):
    i = pl.program_id(1)

    @pl.when(_expert_changed(te_ref, i))
    def _():
        wb_ref[...] = w_ref[...].astype(BF16)

    _on_occupied_prefix(tr_ref[i], TM_DOWN, o_ref, lambda p: _dot(a_ref[0:p, :], wb_ref[...]))


def moe_down(tile_expert, tile_rows, act, wd, j):
    n_slots = act.shape[0]
    tn = TN_MOE_DOWN
    return pl.pallas_call(
        _moe_down_kernel,
        grid_spec=pltpu.PrefetchScalarGridSpec(
            num_scalar_prefetch=2,
            grid=(D // tn, n_slots // TM_DOWN),
            in_specs=[
                pl.BlockSpec((TM_DOWN, D_FF), lambda n, i, te, tr: (i, 0)),
                pl.BlockSpec((None, None, D_FF, tn), lambda n, i, te, tr: (j, te[i], 0, n)),
            ],
            out_specs=pl.BlockSpec((TM_DOWN, tn), lambda n, i, te, tr: (i, n)),
            scratch_shapes=[pltpu.VMEM((D_FF, tn), BF16)],
        ),
        out_shape=jax.ShapeDtypeStruct((n_slots, D), F32),
        compiler_params=_params("arbitrary", "arbitrary"),
        name="moe_down",
    )(tile_expert, tile_rows, act, wd)


def _combine_kernel(p0_ref, p1_ref, ys_ref, x_ref, gt_ref, w_ref, o_ref, b0_ref, b1_ref, sem):
    i = pl.program_id(0)
    base = i * TG

    def copies(r):
        return (pltpu.make_async_copy(ys_ref.at[pl.ds(p0_ref[base + r], 1)], b0_ref.at[pl.ds(r, 1)], sem.at[0]),
                pltpu.make_async_copy(ys_ref.at[pl.ds(p1_ref[base + r], 1)], b1_ref.at[pl.ds(r, 1)], sem.at[1]))

    def start(r, carry):
        c0, c1 = copies(r)
        c0.start()
        c1.start()
        return carry

    lax.fori_loop(0, TG, start, 0, unroll=DMA_UNROLL)
    pltpu.make_async_copy(ys_ref.at[pl.ds(0, TG)], b0_ref, sem.at[0]).wait()
    pltpu.make_async_copy(ys_ref.at[pl.ds(0, TG)], b1_ref, sem.at[1]).wait()
    mix = w_ref[:, 0:1] * b0_ref[...] + w_ref[:, 1:2] * b1_ref[...]
    o_ref[...] = x_ref[...] + _row_gate(gt_ref, i, TG) * mix


def moe_combine(pos0, pos1, ys, x, mods, layer, gate_col, weights):
    return pl.pallas_call(
        _combine_kernel,
        grid_spec=pltpu.PrefetchScalarGridSpec(
            num_scalar_prefetch=2,
            grid=(T // TG,),
            in_specs=[
                pl.BlockSpec(memory_space=pl.ANY),
                pl.BlockSpec((TG, D), lambda i, p0, p1: (i, 0)),
                pl.BlockSpec((None, N_MOD, D), lambda i, p0, p1: (layer, 0, gate_col)),
                pl.BlockSpec((TG, 2), lambda i, p0, p1: (i, 0)),
            ],
            out_specs=pl.BlockSpec((TG, D), lambda i, p0, p1: (i, 0)),
            scratch_shapes=[pltpu.VMEM((TG, D), F32), pltpu.VMEM((TG, D), F32), pltpu.SemaphoreType.DMA((2,))],
        ),
        out_shape=jax.ShapeDtypeStruct((T, D), F32),
        compiler_params=_params("arbitrary"),
        name="moe_combine",
    )(pos0, pos1, ys, x, mods, weights)


def moe_slots(n_tokens):
    return (2 * n_tokens // TM_UP + N_EXP) * TM_UP


def moe_plan(idx, rank, counts, n_slots, latent_only):
    experts = jnp.arange(N_EXP)
    group = (counts + TM_UP - 1) // TM_UP * TM_UP
    start = jnp.cumsum(group) - group
    end = start + counts
    dest = jnp.sum(jnp.where(idx[:, :, None] == experts, start, 0), axis=-1) + rank
    if latent_only:
        is_ctx = (jnp.arange(T) % ROWS) < CTX
        dest = jnp.where(is_ctx, 0, dest)

    def tiles(tm):
        first = jnp.minimum(jnp.arange(n_slots // tm) * tm, jnp.max(jnp.where(counts > 0, end, 0)) - 1)
        expert = jnp.sum((start + group)[None, :] <= first[:, None], axis=1).astype(jnp.int32)
        rows = jnp.clip(end[expert] - jnp.arange(n_slots // tm) * tm, 0, tm).astype(jnp.int32)
        return expert, rows

    return dest.astype(jnp.int32), tiles(TM_UP), tiles(TM_DOWN)


def _final_norm_kernel(x_ref, g_ref, o_ref):
    x = x_ref[...]
    o_ref[...] = x * lax.rsqrt(jnp.mean(x * x, axis=-1, keepdims=True) + EPS) * g_ref[...]


def final_norm(x, g):
    per_batch = SEQ // TM_MOD
    tiles = ROWS // TM_MOD
    first = CTX // TM_MOD
    out = pl.pallas_call(
        _final_norm_kernel,
        grid=(BATCH * per_batch,),
        in_specs=[
            pl.BlockSpec((TM_MOD, D), lambda i: ((i // per_batch) * tiles + first + i % per_batch, 0)),
            pl.BlockSpec((1, D), lambda i: (0, 0)),
        ],
        out_specs=pl.BlockSpec((TM_MOD, D), lambda i: (i, 0)),
        out_shape=jax.ShapeDtypeStruct((BATCH * SEQ, D), F32),
        compiler_params=_params("arbitrary"),
        name="final_norm",
    )(x, g.reshape(1, D))
    return out.reshape(BATCH, SEQ, D)


def _pad_cols(w, n):
    return jnp.pad(w, ((0, 0), (0, n - w.shape[1])))


def kernel(x, c, ctx, c_ctx, ada_w, ada_b, norm_mix_g, norm_ffn_g, gla_w_in, gla_wg_fwd, gla_bg_fwd, gla_wg_bwd, gla_bg_bwd, gla_norm_g, gla_w_out, nat_w_in, nat_rpb, nat_w_out, ffn_w_gate, ffn_w_up, ffn_w_down, moe_router, moe_w_gate, moe_w_up, moe_w_down, final_norm_g):
    xs = jnp.concatenate([ctx, x], axis=1).reshape(T, D)
    c8 = jnp.concatenate([c, c_ctx[None, :], jnp.zeros((N_MOD - BATCH - 1, D), F32)], axis=0)
    mods = ada_tables(c8, ada_w, ada_b)
    cos_t, sin_t = rope_tables()
    router_pad = jnp.pad(moe_router, ((0, 0), (0, 0), (0, LANES - N_EXP)))

    for i in range(DEPTH):
        j = i // 2
        last = i == DEPTH - 1
        h = modulate(xs, norm_mix_g, mods, i, 0, 1)
        if i % 2 == 0:
            qk = mm_rope(h, gla_w_in, j, cos_t, sin_t, tm=1152)
            vr = mm_plain(h, gla_w_in, j, 2 * GLA_QK, 2 * D, BF16, tm=1152, tn=1024)
            z0 = 2 * GLA_QK + 2 * D
            wz_pad = _pad_cols(gla_w_in[j, :, z0:z0 + 2 * GLA_RANK], LANES)
            wf_pad = jnp.pad(gla_wg_fwd[j], ((0, LANES - GLA_RANK), (0, 0)))
            wb_pad = jnp.pad(gla_wg_bwd[j], ((GLA_RANK, LANES - 2 * GLA_RANK), (0, 0)))
            gf, gb = gla_gates(h, wz_pad, wf_pad, gla_bg_fwd, wb_pad, gla_bg_bwd, j, tm=1152)
            y = gla_scan(qk, vr, gf, gb, gla_norm_g, j)
            xs = mm_resid(y, gla_w_out, j, xs, mods, i, 2, tm=1152, tn=1024)
        else:
            qkv = mm_plain(h, nat_w_in, j, 0, 3 * D, BF16, tm=1152, tn=1024,
                           scale=NAT_DH ** -0.5, n_scaled=D // 1024)
            y = nat_attention(qkv, nat_bias_table(nat_rpb[j]), not last)
            xs = mm_resid(y, nat_w_out, j, xs, mods, i, 2, tm=1152, tn=1024)
        if i % 2 == 0:
            h2 = modulate(xs, norm_ffn_g, mods, i, 3, 4)
            act = mm_swiglu(h2, ffn_w_gate, ffn_w_up, j, tm=1152, tn=512)
            xs = mm_resid(act, ffn_w_down, j, xs, mods, i, 5, tm=576, tn=512)
        else:
            h2, idx, gates, rank, counts = modulate_route(xs, norm_ffn_g, mods, i, 3, 4, router_pad, j, last)
            n_slots = moe_slots(BATCH * SEQ if last else T)
            dest, (up_expert, up_rows), (down_expert, down_rows) = moe_plan(idx, rank, counts[:, 0], n_slots, last)
            h_words = lax.bitcast_convert_type(h2.reshape(T, D // 2, 2), jnp.uint32)
            xg = moe_dispatch(dest[0], dest[1], h_words, n_slots, last)
            xg = lax.bitcast_convert_type(xg, BF16).reshape(n_slots, D)
            act = moe_up(up_expert, up_rows, xg, moe_w_gate, moe_w_up, j)
            ys = moe_down(down_expert, down_rows, act, moe_w_down, j)
            xs = moe_combine(dest[0], dest[1], ys, xs, mods, i, 5, gates.T)
    return final_norm(xs, final_norm_g)
```

```python
import functools

import numpy as np
import jax
import jax.numpy as jnp
from jax import lax
from jax.experimental import pallas as pl
from jax.experimental.pallas import tpu as pltpu

F32 = jnp.float32
BF16 = jnp.bfloat16
HIGHEST = lax.Precision.HIGHEST

D = 2048
BATCH = 4
SEQ = 2048
CTX = 256
ROWS = CTX + SEQ
T = BATCH * ROWS
DEPTH = 4
GRID_W = 64
GRID_H = SEQ // GRID_W
EPS = 1e-6
ROPE_BASE = 10000.0
CTX_GROUP = BATCH
N_MOD = 8

GLA_H = 4
GLA_DK = 256
GLA_DV = 512
GLA_QK = GLA_H * GLA_DK
GLA_RANK = 16
GLA_TAU = 16.0
GLA_C = 64
GLA_SUB = 16

NAT_H = 16
NAT_DH = 128
WIN_R = 8
WIN_C = 16
NAT_ROWS_PER_STEP = 8

D_FF = 5632
N_EXP = 8

LANES = 128
VMEM_LIMIT = 56 * 1024 * 1024

TM_MOD = 256
TM_MODULATE = 768
TM_UP = 1024
TM_DOWN = 512
TM_PART = 256
TF_MOE = 512
TN_MOE_DOWN = 512
TG = 256
DMA_UNROLL = 8


def _params(*sem):
    return pltpu.CompilerParams(dimension_semantics=sem, vmem_limit_bytes=VMEM_LIMIT)


def _dot(a, b):
    return jnp.dot(a, b, preferred_element_type=F32)


def _dot_nt(a, b):
    return lax.dot_general(a, b, (((1,), (1,)), ((), ())), preferred_element_type=F32)


def _dot_tn(a, b):
    return lax.dot_general(a, b, (((0,), (0,)), ((), ())), preferred_element_type=F32)


def _silu(v):
    return v * jax.nn.sigmoid(v)


def _row_gate(tab_ref, m, tm):
    start = m * tm
    b = start // ROWS
    rows = start % ROWS + lax.broadcasted_iota(jnp.int32, (tm, 1), 0)
    per_batch = tab_ref[pl.ds(b, 1), :]
    per_ctx = tab_ref[CTX_GROUP:CTX_GROUP + 1, :]
    return jnp.where(rows < CTX, per_ctx, per_batch)


def _ada_kernel(c_ref, w_ref, b_ref, o_ref):
    a = _silu(c_ref[...]).astype(BF16)
    o_ref[...] = _dot(a, w_ref[...].astype(BF16)) + b_ref[...]


def ada_tables(c8, ada_w, ada_b):
    tn = 1024
    return pl.pallas_call(
        _ada_kernel,
        grid=(DEPTH, 6 * D // tn),
        in_specs=[
            pl.BlockSpec((N_MOD, D), lambda l, n: (0, 0)),
            pl.BlockSpec((None, D, tn), lambda l, n: (l, 0, n)),
            pl.BlockSpec((None, 1, tn), lambda l, n: (l, 0, n)),
        ],
        out_specs=pl.BlockSpec((None, N_MOD, tn), lambda l, n: (l, 0, n)),
        out_shape=jax.ShapeDtypeStruct((DEPTH, N_MOD, 6 * D), F32),
        compiler_params=_params("arbitrary", "arbitrary"),
        name="ada_tables",
    )(c8, ada_w, ada_b.reshape(DEPTH, 1, 6 * D))


def _modulated(x_ref, g_ref, sh_ref, sc_ref):
    x = x_ref[...]
    tm = x.shape[0]
    y = x * lax.rsqrt(jnp.mean(x * x, axis=-1, keepdims=True) + EPS) * g_ref[...]
    i = pl.program_id(0)
    return y * (1.0 + _row_gate(sc_ref, i, tm)) + _row_gate(sh_ref, i, tm)


def _modulate_kernel(x_ref, g_ref, sh_ref, sc_ref, o_ref):
    o_ref[...] = _modulated(x_ref, g_ref, sh_ref, sc_ref).astype(o_ref.dtype)


def _mod_specs(layer, sh_col, sc_col, tm):
    return [
        pl.BlockSpec((tm, D), lambda i: (i, 0)),
        pl.BlockSpec((None, 1, D), lambda i: (layer, 0, 0)),
        pl.BlockSpec((None, N_MOD, D), lambda i: (layer, 0, sh_col)),
        pl.BlockSpec((None, N_MOD, D), lambda i: (layer, 0, sc_col)),
    ]


def modulate(x, norm_g, mods, layer, sh_col, sc_col):
    tm = TM_MODULATE
    return pl.pallas_call(
        _modulate_kernel,
        grid=(T // tm,),
        in_specs=_mod_specs(layer, sh_col, sc_col, tm),
        out_specs=pl.BlockSpec((tm, D), lambda i: (i, 0)),
        out_shape=jax.ShapeDtypeStruct((T, D), BF16),
        compiler_params=_params("arbitrary"),
        name="modulate",
    )(x, norm_g.reshape(DEPTH, 1, D), mods, mods)


def _pack_words(h):
    half = h.shape[1] // 2
    lo = lax.bitcast_convert_type(h[:, :half].astype(BF16).astype(F32), jnp.uint32)
    hi = lax.bitcast_convert_type(h[:, half:].astype(BF16).astype(F32), jnp.uint32)
    return (lo >> 16) | (hi & jnp.uint32(0xFFFF0000))


def _unpack_words(w):
    lo = lax.bitcast_convert_type(w << 16, F32).astype(BF16)
    hi = lax.bitcast_convert_type(w & jnp.uint32(0xFFFF0000), F32).astype(BF16)
    return lo, hi


def _modulate_route_kernel(x_ref, g_ref, sh_ref, sc_ref, rt_ref, hw_ref, idx_ref, rank_ref, wcol_ref, cnt_ref,
                           base_ref, *, latent_only):
    i = pl.program_id(0)

    @pl.when(i == 0)
    def _():
        base_ref[...] = jnp.zeros_like(base_ref)

    h = _modulated(x_ref, g_ref, sh_ref, sc_ref)
    hw_ref[...] = _pack_words(h)
    logits = jnp.dot(h, rt_ref[...], precision=HIGHEST, preferred_element_type=F32)

    lane = lax.broadcasted_iota(jnp.int32, logits.shape, 1)
    lm = jnp.where(lane < N_EXP, logits, -jnp.inf)
    c1 = jnp.max(lm, axis=1, keepdims=True)
    j1 = jnp.min(jnp.where(lm == c1, lane, LANES), axis=1, keepdims=True)
    c2 = jnp.max(jnp.where(lane == j1, -jnp.inf, lm), axis=1, keepdims=True)
    e = jnp.exp(c2 - c1)
    wcol_ref[...] = jnp.where(lane == 0, 1.0 / (1.0 + e), jnp.where(lane == 1, e / (1.0 + e), 0.0))

    lt = logits.T[:N_EXP]
    eid = lax.broadcasted_iota(jnp.int32, lt.shape, 0)
    m1 = jnp.max(lt, axis=0, keepdims=True)
    i1 = jnp.min(jnp.where(lt == m1, eid, N_EXP), axis=0, keepdims=True)
    lt2 = jnp.where(eid == i1, -jnp.inf, lt)
    m2 = jnp.max(lt2, axis=0, keepdims=True)
    i2 = jnp.min(jnp.where(lt2 == m2, eid, N_EXP), axis=0, keepdims=True)
    idx_ref[...] = jnp.concatenate([i1, i2], axis=0)

    routed = 1.0
    if latent_only:
        routed = jnp.where(i % (ROWS // TM_MOD) < CTX // TM_MOD, 0.0, 1.0)
    oh1 = jnp.where(eid == i1, routed, 0.0)
    oh2 = jnp.where(eid == i2, routed, 0.0)
    both = oh1 + oh2
    tj = lax.broadcasted_iota(jnp.int32, (TM_MOD, TM_MOD), 0)
    tt = lax.broadcasted_iota(jnp.int32, (TM_MOD, TM_MOD), 1)
    before = _dot(both.astype(BF16), jnp.where(tj < tt, 1.0, 0.0).astype(BF16))
    seen = base_ref[:, 0:1] + before
    r1 = jnp.sum(oh1 * seen, axis=0, keepdims=True)
    r2 = jnp.sum(oh2 * seen, axis=0, keepdims=True)
    rank_ref[...] = jnp.concatenate([r1, r2], axis=0).astype(jnp.int32)
    total = base_ref[...] + jnp.sum(both, axis=1, keepdims=True)
    base_ref[...] = total
    cnt_ref[...] = total.astype(jnp.int32)


def modulate_route(x, norm_g, mods, layer, sh_col, sc_col, router_pad, j, latent_only):
    pairs = pl.BlockSpec((2, TM_MOD), lambda i: (0, i))
    return pl.pallas_call(
        functools.partial(_modulate_route_kernel, latent_only=latent_only),
        grid=(T // TM_MOD,),
        in_specs=_mod_specs(layer, sh_col, sc_col, TM_MOD) + [
            pl.BlockSpec((None, D, LANES), lambda i: (j, 0, 0)),
        ],
        out_specs=[
            pl.BlockSpec((TM_MOD, D // 2), lambda i: (i, 0)),
            pairs, pairs,
            pl.BlockSpec((TM_MOD, LANES), lambda i: (i, 0)),
            pl.BlockSpec((N_EXP, LANES), lambda i: (0, 0)),
        ],
        out_shape=[
            jax.ShapeDtypeStruct((T, D // 2), jnp.uint32),
            jax.ShapeDtypeStruct((2, T), jnp.int32),
            jax.ShapeDtypeStruct((2, T), jnp.int32),
            jax.ShapeDtypeStruct((T, LANES), F32),
            jax.ShapeDtypeStruct((N_EXP, LANES), jnp.int32),
        ],
        scratch_shapes=[pltpu.VMEM((N_EXP, LANES), F32)],
        compiler_params=_params("arbitrary"),
        name="modulate_route",
    )(x, norm_g.reshape(DEPTH, 1, D), mods, mods, router_pad)


def _cast_weight(w_ref, wb_ref):
    @pl.when(pl.program_id(1) == 0)
    def _():
        wb_ref[...] = w_ref[...].astype(BF16)


def _mm_plain_kernel(a_ref, w_ref, o_ref, wb_ref, *, scale, n_scaled):
    _cast_weight(w_ref, wb_ref)
    acc = _dot(a_ref[...], wb_ref[...])
    if n_scaled:
        acc = acc * jnp.where(pl.program_id(0) < n_scaled, scale, 1.0)
    o_ref[...] = acc.astype(o_ref.dtype)


def mm_plain(a, w, layer, col0, n_out, out_dtype, tm, tn, scale=1.0, n_scaled=0):
    k = a.shape[1]
    c0 = col0 // tn
    return pl.pallas_call(
        functools.partial(_mm_plain_kernel, scale=scale, n_scaled=n_scaled),
        grid=(n_out // tn, T // tm),
        in_specs=[
            pl.BlockSpec((tm, k), lambda n, m: (m, 0)),
            pl.BlockSpec((None, k, tn), lambda n, m: (layer, 0, n + c0)),
        ],
        out_specs=pl.BlockSpec((tm, tn), lambda n, m: (m, n)),
        out_shape=jax.ShapeDtypeStruct((T, n_out), out_dtype),
        scratch_shapes=[pltpu.VMEM((k, tn), BF16)],
        compiler_params=_params("arbitrary", "arbitrary"),
        name="mm_plain",
    )(a, w)


def _mm_rope_kernel(a_ref, w_ref, cos_ref, sin_ref, o_ref, wb_ref, *, tn):
    _cast_weight(w_ref, wb_ref)
    acc = _dot(a_ref[...], wb_ref[...])
    scale = jnp.where(pl.program_id(0) == 0, GLA_DK ** -0.5, 1.0)
    for s in range(tn // LANES):
        xs = acc[:, s * LANES:(s + 1) * LANES]
        t = (s % 2) * LANES
        rot = xs * cos_ref[:, t:t + LANES] + pltpu.roll(xs, LANES // 2, 1) * sin_ref[:, t:t + LANES]
        o_ref[:, s * LANES:(s + 1) * LANES] = rot * scale


def mm_rope(a, w, layer, cos_t, sin_t, tm):
    tn = GLA_QK
    per_batch = ROWS // tm
    return pl.pallas_call(
        functools.partial(_mm_rope_kernel, tn=tn),
        grid=(2, T // tm),
        in_specs=[
            pl.BlockSpec((tm, D), lambda n, m: (m, 0)),
            pl.BlockSpec((None, D, tn), lambda n, m: (layer, 0, n)),
            pl.BlockSpec((tm, GLA_DK), lambda n, m: (m % per_batch, 0)),
            pl.BlockSpec((tm, GLA_DK), lambda n, m: (m % per_batch, 0)),
        ],
        out_specs=pl.BlockSpec((tm, tn), lambda n, m: (m, n)),
        out_shape=jax.ShapeDtypeStruct((T, 2 * GLA_QK), F32),
        scratch_shapes=[pltpu.VMEM((D, tn), BF16)],
        compiler_params=_params("arbitrary", "arbitrary"),
        name="mm_rope",
    )(a, w, cos_t, sin_t)


def _mm_resid_kernel(a_ref, w_ref, x_ref, gt_ref, o_ref, wb_ref, *, tm):
    _cast_weight(w_ref, wb_ref)
    acc = _dot(a_ref[...], wb_ref[...])
    o_ref[...] = x_ref[...] + _row_gate(gt_ref, pl.program_id(1), tm) * acc


def mm_resid(a, w, layer_w, x, mods, layer, gate_col, tm, tn):
    k = a.shape[1]
    g0 = gate_col * D // tn
    return pl.pallas_call(
        functools.partial(_mm_resid_kernel, tm=tm),
        grid=(D // tn, T // tm),
        in_specs=[
            pl.BlockSpec((tm, k), lambda n, m: (m, 0)),
            pl.BlockSpec((None, k, tn), lambda n, m: (layer_w, 0, n)),
            pl.BlockSpec((tm, tn), lambda n, m: (m, n)),
            pl.BlockSpec((None, N_MOD, tn), lambda n, m: (layer, 0, g0 + n)),
        ],
        out_specs=pl.BlockSpec((tm, tn), lambda n, m: (m, n)),
        out_shape=jax.ShapeDtypeStruct((T, D), F32),
        scratch_shapes=[pltpu.VMEM((k, tn), BF16)],
        compiler_params=_params("arbitrary", "arbitrary"),
        name="mm_resid",
    )(a, w, x, mods)


def _mm_swiglu_kernel(a_ref, wg_ref, wu_ref, o_ref, wgb_ref, wub_ref):
    _cast_weight(wg_ref, wgb_ref)
    _cast_weight(wu_ref, wub_ref)
    a = a_ref[...]
    o_ref[...] = (_silu(_dot(a, wgb_ref[...])) * _dot(a, wub_ref[...])).astype(o_ref.dtype)


def mm_swiglu(a, wg, wu, layer, tm, tn):
    return pl.pallas_call(
        _mm_swiglu_kernel,
        grid=(D_FF // tn, T // tm),
        in_specs=[
            pl.BlockSpec((tm, D), lambda n, m: (m, 0)),
            pl.BlockSpec((None, D, tn), lambda n, m: (layer, 0, n)),
            pl.BlockSpec((None, D, tn), lambda n, m: (layer, 0, n)),
        ],
        out_specs=pl.BlockSpec((tm, tn), lambda n, m: (m, n)),
        out_shape=jax.ShapeDtypeStruct((T, D_FF), BF16),
        scratch_shapes=[pltpu.VMEM((D, tn), BF16), pltpu.VMEM((D, tn), BF16)],
        compiler_params=_params("arbitrary", "arbitrary"),
        name="mm_swiglu",
    )(a, wg, wu)


def _log_sigmoid(v):
    return jnp.minimum(v, 0.0) - jnp.log(1.0 + jnp.exp(-jnp.abs(v)))


def _gla_gate_kernel(h_ref, wz_ref, wf_ref, bf_ref, wb_ref, bb_ref, gf_ref, gb_ref):
    z = _dot(h_ref[...], wz_ref[...].astype(BF16))
    af = jnp.dot(z, wf_ref[...], precision=HIGHEST, preferred_element_type=F32) + bf_ref[...]
    ab = jnp.dot(z, wb_ref[...], precision=HIGHEST, preferred_element_type=F32) + bb_ref[...]
    gf_ref[...] = _log_sigmoid(af) / GLA_TAU
    gb_ref[...] = _log_sigmoid(ab) / GLA_TAU


def gla_gates(h, wz_pad, wf_pad, bg_f, wb_pad, bg_b, j, tm):
    row = lambda i: (i, 0)
    return pl.pallas_call(
        _gla_gate_kernel,
        grid=(T // tm,),
        in_specs=[
            pl.BlockSpec((tm, D), row),
            pl.BlockSpec((D, LANES), lambda i: (0, 0)),
            pl.BlockSpec((LANES, GLA_QK), lambda i: (0, 0)),
            pl.BlockSpec((None, 1, GLA_QK), lambda i: (j, 0, 0)),
            pl.BlockSpec((LANES, GLA_QK), lambda i: (0, 0)),
            pl.BlockSpec((None, 1, GLA_QK), lambda i: (j, 0, 0)),
        ],
        out_specs=[pl.BlockSpec((tm, GLA_QK), row), pl.BlockSpec((tm, GLA_QK), row)],
        out_shape=[jax.ShapeDtypeStruct((T, GLA_QK), F32)] * 2,
        compiler_params=_params("arbitrary"),
        name="gla_gates",
    )(h, wz_pad, wf_pad, bg_f.reshape(-1, 1, GLA_QK), wb_pad, bg_b.reshape(-1, 1, GLA_QK))


def _gla_chunks(streams):
    c, sub, nsub = GLA_C, GLA_SUB, GLA_C // GLA_SUB
    ri = lax.broadcasted_iota(jnp.int32, (c, c), 0)
    ci = lax.broadcasted_iota(jnp.int32, (c, c), 1)
    lane = lax.broadcasted_iota(jnp.int32, (sub, c), 1)
    srow = lax.broadcasted_iota(jnp.int32, (sub, 1), 0)

    bcs = []
    for q, k, v, g, st, rev in streams:
        tri = (ci >= ri if rev else ci <= ri).astype(F32)
        bcs.append(jnp.dot(tri, g, precision=HIGHEST, preferred_element_type=F32))

    partial = []
    for (q, k, v, g, st, rev), bc in zip(streams, bcs):
        b_end = bc[0:1] if rev else bc[c - 1:c]
        o = _dot_nt((q * jnp.exp(bc)).astype(BF16), st.astype(BF16))
        st_new = st * jnp.exp(b_end) + _dot_tn(v, (k * jnp.exp(b_end - bc)).astype(BF16))
        off = []
        for bi in range(nsub):
            lo, hi = sub * bi, sub * (bi + 1)
            q_i, b_i = q[lo:hi], bc[lo:hi]
            if not rev and bi > 0:
                ref = bc[lo - 1:lo]
                kt = jnp.concatenate([k[:lo] * jnp.exp(ref - bc[:lo]), jnp.zeros((c - lo, GLA_DK), F32)], axis=0)
                off.append(_dot_nt((q_i * jnp.exp(b_i - ref)).astype(BF16), kt.astype(BF16)))
            elif rev and bi < nsub - 1:
                ref = bc[hi:hi + 1]
                kt = jnp.concatenate([jnp.zeros((hi, GLA_DK), F32), k[hi:] * jnp.exp(ref - bc[hi:])], axis=0)
                off.append(_dot_nt((q_i * jnp.exp(b_i - ref)).astype(BF16), kt.astype(BF16)))
            else:
                off.append(jnp.zeros((sub, c), F32))
        partial.append((o, st_new, off))

    intra = []
    for (q, k, v, g, st, rev), bc, (o, st_new, off) in zip(streams, bcs, partial):
        blocks = []
        for bi in range(nsub):
            lo, hi = sub * bi, sub * (bi + 1)
            q_i, k_i, b_i = q[lo:hi], k[lo:hi], bc[lo:hi]
            a_i = off[bi]
            for jj in range(sub):
                t = q_i * k_i[jj:jj + 1] * jnp.exp(b_i - b_i[jj:jj + 1])
                s = jnp.sum(t, axis=-1, keepdims=True)
                seen = srow <= jj if rev else srow >= jj
                a_i = jnp.where(lane == lo + jj, jnp.where(seen, s, 0.0), a_i)
            blocks.append(a_i)
        intra.append(jnp.concatenate(blocks, axis=0).astype(BF16))

    return [(o + _dot(a, v), st_new) for (q, k, v, g, st, rev), (o, st_new, off), a in zip(streams, partial, intra)]


def _gla_kernel(q_ref, k_ref, v_ref, r_ref, gf_ref, gb_ref, ng_ref, y_ref, stf_ref, stb_ref, of_ref, ob_ref):
    nch = ROWS // GLA_C
    nctx = CTX // GLA_C

    def chunk_rows(ch):
        return pl.ds(pl.multiple_of(ch * GLA_C, GLA_C), GLA_C)

    def stream(rows, g_ref, st_ref, rev):
        return q_ref[rows, :], k_ref[rows, :], v_ref[rows, :], g_ref[rows, :], st_ref[...], rev

    stf_ref[...] = jnp.zeros_like(stf_ref)
    stb_ref[...] = jnp.zeros_like(stb_ref)

    def scan(s, carry):
        rows_f = chunk_rows(s)
        rows_b = chunk_rows(jnp.where(s < nctx, nctx - 1 - s, nch + nctx - 1 - s))
        (o_f, st_f), (o_b, st_b) = _gla_chunks([stream(rows_f, gf_ref, stf_ref, False),
                                                stream(rows_b, gb_ref, stb_ref, True)])
        of_ref[rows_f, :] = o_f
        ob_ref[rows_b, :] = o_b
        stf_ref[...] = st_f
        stb_ref[...] = st_b
        return carry

    lax.fori_loop(0, nch, scan, 0)

    def readout(ch, carry):
        rows = chunk_rows(ch)
        o = of_ref[rows, :] + ob_ref[rows, :]
        on = o * lax.rsqrt(jnp.mean(o * o, axis=-1, keepdims=True) + EPS) * ng_ref[...]
        y_ref[rows, :] = (on * _silu(r_ref[rows, :].astype(F32))).astype(y_ref.dtype)
        return carry

    lax.fori_loop(0, nch, readout, 0)


def gla_scan(qk, vr, gf, gb, norm_g, j):
    return pl.pallas_call(
        _gla_kernel,
        grid=(BATCH, GLA_H),
        in_specs=[
            pl.BlockSpec((ROWS, GLA_DK), lambda b, h: (b, h)),
            pl.BlockSpec((ROWS, GLA_DK), lambda b, h: (b, GLA_H + h)),
            pl.BlockSpec((ROWS, GLA_DV), lambda b, h: (b, h)),
            pl.BlockSpec((ROWS, GLA_DV), lambda b, h: (b, GLA_H + h)),
            pl.BlockSpec((ROWS, GLA_DK), lambda b, h: (b, h)),
            pl.BlockSpec((ROWS, GLA_DK), lambda b, h: (b, h)),
            pl.BlockSpec((None, 1, GLA_DV), lambda b, h: (j, 0, 0)),
        ],
        out_specs=pl.BlockSpec((ROWS, GLA_DV), lambda b, h: (b, h)),
        out_shape=jax.ShapeDtypeStruct((T, D), BF16),
        scratch_shapes=[pltpu.VMEM((GLA_DV, GLA_DK), F32), pltpu.VMEM((GLA_DV, GLA_DK), F32),
                        pltpu.VMEM((ROWS, GLA_DV), F32), pltpu.VMEM((ROWS, GLA_DV), F32)],
        compiler_params=_params("arbitrary", "arbitrary"),
        name="gla_scan",
    )(qk, qk, vr, vr, gf, gb, norm_g.reshape(-1, 1, GLA_DV))


def rope_tables():
    half = GLA_DK // 2
    freqs = ROPE_BASE ** (-np.arange(0, half, 2, dtype=np.float32) / half)
    t = np.arange(SEQ)
    pos = np.stack([t // GRID_W, t % GRID_W], axis=-1).astype(np.float32)
    ang = pos[:, :, None] * freqs
    cos, sin = np.cos(ang), np.sin(ang)
    cos_t = np.concatenate([cos, cos], axis=-1).reshape(SEQ, GLA_DK)
    sin_t = np.concatenate([-sin, sin], axis=-1).reshape(SEQ, GLA_DK)
    cos_t = np.concatenate([np.ones((CTX, GLA_DK), np.float32), cos_t], axis=0)
    sin_t = np.concatenate([np.zeros((CTX, GLA_DK), np.float32), sin_t], axis=0)
    return jnp.asarray(cos_t, F32), jnp.asarray(sin_t, F32)


def _nat_kernel(q_ref, k_ref, v_ref, bias_ref, y_ref, *, need_ctx):
    kc = k_ref[0:CTX, :]
    vc = v_ref[0:CTX, :]
    if need_ctx:
        s = _dot_nt(q_ref[0:CTX, :], kc)
        p = jnp.exp(s - jnp.max(s, axis=-1, keepdims=True))
        o = _dot(p.astype(BF16), vc) / jnp.sum(p, axis=-1, keepdims=True)
        y_ref[0:CTX, :] = o.astype(y_ref.dtype)
    else:
        y_ref[0:CTX, :] = jnp.zeros((CTX, NAT_DH), y_ref.dtype)
    n_keys = WIN_R * GRID_W

    def scores(r):
        rs = jnp.clip(r - WIN_R // 2, 0, GRID_H - WIN_R)
        q0 = pl.multiple_of(CTX + r * GRID_W, GRID_W)
        k0 = pl.multiple_of(CTX + rs * GRID_W, GRID_W)
        q = q_ref[pl.ds(q0, GRID_W), :]
        s_lat = _dot_nt(q, k_ref[pl.ds(k0, n_keys), :]) + bias_ref[r - rs]
        s_ctx = _dot_nt(q, kc)
        return q0, k0, s_lat, s_ctx

    def probs(q0, k0, s_lat, s_ctx):
        m = jnp.maximum(jnp.max(s_lat, axis=-1, keepdims=True), jnp.max(s_ctx, axis=-1, keepdims=True))
        p_lat = jnp.exp(s_lat - m)
        p_ctx = jnp.exp(s_ctx - m)
        denom = jnp.sum(p_lat, axis=-1, keepdims=True) + jnp.sum(p_ctx, axis=-1, keepdims=True)
        return q0, k0, p_lat.astype(BF16), p_ctx.astype(BF16), denom

    def values(q0, k0, p_lat, p_ctx, denom):
        o = _dot(p_lat, v_ref[pl.ds(k0, n_keys), :]) + _dot(p_ctx, vc)
        return q0, (o / denom).astype(y_ref.dtype)

    def body(it, carry):
        rows = [it * NAT_ROWS_PER_STEP + u for u in range(NAT_ROWS_PER_STEP)]
        outs = [values(*pr) for pr in [probs(*sc) for sc in [scores(r) for r in rows]]]
        for q0, o in outs:
            y_ref[pl.ds(q0, GRID_W), :] = o
        return carry

    lax.fori_loop(0, GRID_H // NAT_ROWS_PER_STEP, body, 0)


def nat_attention(qkv, bias_tbl, need_ctx):
    return pl.pallas_call(
        functools.partial(_nat_kernel, need_ctx=need_ctx),
        grid=(NAT_H, BATCH),
        in_specs=[
            pl.BlockSpec((ROWS, NAT_DH), lambda h, b: (b, h)),
            pl.BlockSpec((ROWS, NAT_DH), lambda h, b: (b, NAT_H + h)),
            pl.BlockSpec((ROWS, NAT_DH), lambda h, b: (b, 2 * NAT_H + h)),
            pl.BlockSpec((None, WIN_R, GRID_W, WIN_R * GRID_W), lambda h, b: (h, 0, 0, 0)),
        ],
        out_specs=pl.BlockSpec((ROWS, NAT_DH), lambda h, b: (b, h)),
        out_shape=jax.ShapeDtypeStruct((T, D), BF16),
        compiler_params=_params("arbitrary", "arbitrary"),
        name="nat_attention",
    )(qkv, qkv, qkv, bias_tbl)


def nat_bias_table(rpb):
    off = np.arange(WIN_R)[:, None]
    dr = np.arange(WIN_R)[None, :] - off + (WIN_R - 1)
    qc = np.arange(GRID_W)[:, None]
    kcol = np.arange(GRID_W)[None, :]
    start = np.clip(qc - WIN_C // 2, 0, GRID_W - WIN_C)
    in_win = (kcol >= start) & (kcol < start + WIN_C)
    dc = np.clip(kcol - qc + WIN_C - 1, 0, 2 * WIN_C - 2)
    tbl = rpb[:, dr][:, :, :, dc]
    tbl = jnp.where(in_win, tbl, -jnp.inf)
    return jnp.transpose(tbl, (0, 1, 3, 2, 4)).reshape(NAT_H, WIN_R, GRID_W, WIN_R * GRID_W)


def _pair_slots(e0_ref, e1_ref, r0_ref, r1_ref, start_ref, t):
    return start_ref[e0_ref[t]] + r0_ref[t], start_ref[e1_ref[t]] + r1_ref[t]


def _is_routed_tile(i, latent_only):
    return (i % (ROWS // TG) >= CTX // TG) if latent_only else None


def _dispatch_kernel(e0_ref, e1_ref, r0_ref, r1_ref, start_ref, part_ref, h_ref, o_ref, zero_ref, sem, *,
                     latent_only):
    i = pl.program_id(0)
    base = i * TG

    @pl.when(i == 0)
    def _():
        zero_ref[...] = jnp.zeros_like(zero_ref)

        def zero_copy(p):
            return pltpu.make_async_copy(zero_ref, o_ref.at[pl.ds(pl.multiple_of(p * TM_PART, TM_PART), TM_PART)],
                                         sem.at[2])

        def start(p, carry):
            pl.when(part_ref[p] < TM_PART)(lambda: zero_copy(p).start())
            return carry

        def wait(p, carry):
            pl.when(part_ref[p] < TM_PART)(lambda: zero_copy(p).wait())
            return carry

        n_parts = o_ref.shape[0] // TM_PART
        lax.fori_loop(0, n_parts, start, 0)
        lax.fori_loop(0, n_parts, wait, 0)

    def start_rows(r, carry):
        row = h_ref.at[pl.ds(r, 1)]
        s0, s1 = _pair_slots(e0_ref, e1_ref, r0_ref, r1_ref, start_ref, base + r)
        pltpu.make_async_copy(row, o_ref.at[pl.ds(s0, 1)], sem.at[0]).start()
        pltpu.make_async_copy(row, o_ref.at[pl.ds(s1, 1)], sem.at[1]).start()
        return carry

    def run():
        lax.fori_loop(0, TG, start_rows, 0, unroll=DMA_UNROLL)
        for s in range(2):
            pltpu.make_async_copy(h_ref, o_ref.at[pl.ds(0, TG)], sem.at[s]).wait()

    routed = _is_routed_tile(i, latent_only)
    if routed is None:
        run()
    else:
        pl.when(routed)(run)


def moe_dispatch(routing, part_rows, h_words, n_slots, latent_only):
    words = h_words.shape[1]
    return pl.pallas_call(
        functools.partial(_dispatch_kernel, latent_only=latent_only),
        grid_spec=pltpu.PrefetchScalarGridSpec(
            num_scalar_prefetch=6,
            grid=(T // TG,),
            in_specs=[pl.BlockSpec((TG, words), lambda i, *_: (i, 0))],
            out_specs=pl.BlockSpec(memory_space=pl.ANY),
            scratch_shapes=[pltpu.VMEM((TM_PART, words), jnp.uint32), pltpu.SemaphoreType.DMA((3,))],
        ),
        out_shape=jax.ShapeDtypeStruct((n_slots, words), jnp.uint32),
        compiler_params=_params("arbitrary"),
        name="moe_dispatch",
    )(*routing, part_rows, h_words)


def _expert_changed(te_ref, i):
    return (i == 0) | (te_ref[i] != te_ref[jnp.maximum(i - 1, 0)])


def _on_occupied_prefix(n_rows, tile, o_ref, compute):
    for parts in range(tile // TM_PART + 1):
        p = parts * TM_PART

        @pl.when((n_rows > p - TM_PART) & (n_rows <= p))
        def _():
            if p > 0:
                o_ref[0:p, :] = compute(p)
            if p < tile:
                o_ref[p:tile, :] = jnp.zeros((tile - p, o_ref.shape[1]), o_ref.dtype)


def _moe_up_kernel(te_ref, tr_ref, ts_ref, a_ref, wg_ref, wu_ref, o_ref, wgb_ref, wub_ref):
    del ts_ref
    i = pl.program_id(1)

    @pl.when(_expert_changed(te_ref, i))
    def _():
        wgb_ref[...] = wg_ref[...].astype(BF16)
        wub_ref[...] = wu_ref[...].astype(BF16)

    half = D // 2

    def compute(p):
        lo, hi = _unpack_words(a_ref[0:p, :])
        gate = _dot(lo, wgb_ref[0:half, :]) + _dot(hi, wgb_ref[half:D, :])
        up = _dot(lo, wub_ref[0:half, :]) + _dot(hi, wub_ref[half:D, :])
        return (_silu(gate) * up).astype(o_ref.dtype)

    _on_occupied_prefix(tr_ref[i], TM_UP, o_ref, compute)


def moe_up(tiles, xw, wg, wu, j):
    n_slots = xw.shape[0]
    w_spec = pl.BlockSpec((None, None, D, TF_MOE), lambda f, i, te, tr, ts: (j, te[i], 0, f))
    return pl.pallas_call(
        _moe_up_kernel,
        grid_spec=pltpu.PrefetchScalarGridSpec(
            num_scalar_prefetch=3,
            grid=(D_FF // TF_MOE, n_slots // TM_UP),
            in_specs=[pl.BlockSpec((TM_UP, D // 2), lambda f, i, te, tr, ts: (ts[i], 0)), w_spec, w_spec],
            out_specs=pl.BlockSpec((TM_UP, TF_MOE), lambda f, i, te, tr, ts: (i, f)),
            scratch_shapes=[pltpu.VMEM((D, TF_MOE), BF16), pltpu.VMEM((D, TF_MOE), BF16)],
        ),
        out_shape=jax.ShapeDtypeStruct((n_slots, D_FF), BF16),
        compiler_params=_params("arbitrary", "arbitrary"),
        name="moe_up",
    )(*tiles, xw, wg, wu)


def _moe_down_kernel(te_ref, tr_ref, ts_ref, a_ref, w_ref, o_ref, wb_ref):
    del ts_ref
    i = pl.program_id(1)

    @pl.when(_expert_changed(te_ref, i))
    def _():
        wb_ref[...] = w_ref[...].astype(BF16)

    _on_occupied_prefix(tr_ref[i], TM_DOWN, o_ref, lambda p: _dot(a_ref[0:p, :], wb_ref[...]))


def moe_down(tiles, act, wd, j):
    n_slots = act.shape[0]
    tn = TN_MOE_DOWN
    return pl.pallas_call(
        _moe_down_kernel,
        grid_spec=pltpu.PrefetchScalarGridSpec(
            num_scalar_prefetch=3,
            grid=(D // tn, n_slots // TM_DOWN),
            in_specs=[
                pl.BlockSpec((TM_DOWN, D_FF), lambda n, i, te, tr, ts: (ts[i], 0)),
                pl.BlockSpec((None, None, D_FF, tn), lambda n, i, te, tr, ts: (j, te[i], 0, n)),
            ],
            out_specs=pl.BlockSpec((TM_DOWN, tn), lambda n, i, te, tr, ts: (i, n)),
            scratch_shapes=[pltpu.VMEM((D_FF, tn), BF16)],
        ),
        out_shape=jax.ShapeDtypeStruct((n_slots, D), F32),
        compiler_params=_params("arbitrary", "arbitrary"),
        name="moe_down",
    )(*tiles, act, wd)


def _combine_kernel(e0_ref, e1_ref, r0_ref, r1_ref, start_ref, ys_ref, x_ref, gt_ref, w_ref, o_ref,
                    b0_ref, b1_ref, sem, *, latent_only):
    i = pl.program_id(0)
    base = i * TG

    def start_rows(r, carry):
        s0, s1 = _pair_slots(e0_ref, e1_ref, r0_ref, r1_ref, start_ref, base + r)
        pltpu.make_async_copy(ys_ref.at[pl.ds(s0, 1)], b0_ref.at[pl.ds(r, 1)], sem.at[0]).start()
        pltpu.make_async_copy(ys_ref.at[pl.ds(s1, 1)], b1_ref.at[pl.ds(r, 1)], sem.at[1]).start()
        return carry

    def run():
        lax.fori_loop(0, TG, start_rows, 0, unroll=DMA_UNROLL)
        pltpu.make_async_copy(ys_ref.at[pl.ds(0, TG)], b0_ref, sem.at[0]).wait()
        pltpu.make_async_copy(ys_ref.at[pl.ds(0, TG)], b1_ref, sem.at[1]).wait()
        mix = w_ref[:, 0:1] * b0_ref[...] + w_ref[:, 1:2] * b1_ref[...]
        o_ref[...] = x_ref[...] + _row_gate(gt_ref, i, TG) * mix

    routed = _is_routed_tile(i, latent_only)
    if routed is None:
        run()
    else:
        pl.when(routed)(run)

        @pl.when(jnp.logical_not(routed))
        def _():
            o_ref[...] = x_ref[...]


def moe_combine(routing, ys, x, mods, layer, gate_col, wcol, latent_only):
    return pl.pallas_call(
        functools.partial(_combine_kernel, latent_only=latent_only),
        grid_spec=pltpu.PrefetchScalarGridSpec(
            num_scalar_prefetch=5,
            grid=(T // TG,),
            in_specs=[
                pl.BlockSpec(memory_space=pl.ANY),
                pl.BlockSpec((TG, D), lambda i, *_: (i, 0)),
                pl.BlockSpec((None, N_MOD, D), lambda i, *_: (layer, 0, gate_col)),
                pl.BlockSpec((TG, LANES), lambda i, *_: (i, 0)),
            ],
            out_specs=pl.BlockSpec((TG, D), lambda i, *_: (i, 0)),
            scratch_shapes=[pltpu.VMEM((TG, D), F32), pltpu.VMEM((TG, D), F32), pltpu.SemaphoreType.DMA((2,))],
        ),
        out_shape=jax.ShapeDtypeStruct((T, D), F32),
        compiler_params=_params("arbitrary"),
        name="moe_combine",
    )(*routing, ys, x, mods, wcol)


def moe_slots(n_tokens):
    return (2 * n_tokens // TM_UP + N_EXP) * TM_UP


def moe_plan(counts, n_slots):
    group = (counts + TM_UP - 1) // TM_UP * TM_UP
    start = jnp.cumsum(group) - group
    end = start + counts
    last_row = jnp.max(jnp.where(counts > 0, end, 0)) - 1

    def tiles(tm):
        first = jnp.arange(n_slots // tm) * tm
        expert = jnp.sum((start + group)[None, :] <= jnp.minimum(first, last_row)[:, None], axis=1)
        rows = jnp.clip(jnp.sum(jnp.where(expert[:, None] == jnp.arange(N_EXP), end, 0), axis=1) - first, 0, tm)
        ids = jnp.arange(n_slots // tm)
        block = jnp.max(jnp.where((ids[None, :] <= ids[:, None]) & (rows[None, :] > 0), ids[None, :], 0), axis=1)
        return expert.astype(jnp.int32), rows.astype(jnp.int32), block.astype(jnp.int32)

    return start.astype(jnp.int32), tiles(TM_UP), tiles(TM_DOWN), tiles(TM_PART)[1]


def _final_norm_kernel(x_ref, g_ref, o_ref):
    x = x_ref[...]
    o_ref[...] = x * lax.rsqrt(jnp.mean(x * x, axis=-1, keepdims=True) + EPS) * g_ref[...]


def final_norm(x, g):
    per_batch = SEQ // TM_MOD
    tiles = ROWS // TM_MOD
    first = CTX // TM_MOD
    out = pl.pallas_call(
        _final_norm_kernel,
        grid=(BATCH * per_batch,),
        in_specs=[
            pl.BlockSpec((TM_MOD, D), lambda i: ((i // per_batch) * tiles + first + i % per_batch, 0)),
            pl.BlockSpec((1, D), lambda i: (0, 0)),
        ],
        out_specs=pl.BlockSpec((TM_MOD, D), lambda i: (i, 0)),
        out_shape=jax.ShapeDtypeStruct((BATCH * SEQ, D), F32),
        compiler_params=_params("arbitrary"),
        name="final_norm",
    )(x, g.reshape(1, D))
    return out.reshape(BATCH, SEQ, D)


def _pad_cols(w, n):
    return jnp.pad(w, ((0, 0), (0, n - w.shape[1])))


def kernel(x, c, ctx, c_ctx, ada_w, ada_b, norm_mix_g, norm_ffn_g, gla_w_in, gla_wg_fwd, gla_bg_fwd, gla_wg_bwd, gla_bg_bwd, gla_norm_g, gla_w_out, nat_w_in, nat_rpb, nat_w_out, ffn_w_gate, ffn_w_up, ffn_w_down, moe_router, moe_w_gate, moe_w_up, moe_w_down, final_norm_g):
    xs = jnp.concatenate([ctx, x], axis=1).reshape(T, D)
    c8 = jnp.concatenate([c, c_ctx[None, :], jnp.zeros((N_MOD - BATCH - 1, D), F32)], axis=0)
    mods = ada_tables(c8, ada_w, ada_b)
    cos_t, sin_t = rope_tables()
    router_pad = jnp.pad(moe_router, ((0, 0), (0, 0), (0, LANES - N_EXP)))

    for i in range(DEPTH):
        j = i // 2
        last = i == DEPTH - 1
        h = modulate(xs, norm_mix_g, mods, i, 0, 1)
        if i % 2 == 0:
            qk = mm_rope(h, gla_w_in, j, cos_t, sin_t, tm=1152)
            vr = mm_plain(h, gla_w_in, j, 2 * GLA_QK, 2 * D, BF16, tm=1152, tn=1024)
            z0 = 2 * GLA_QK + 2 * D
            wz_pad = _pad_cols(gla_w_in[j, :, z0:z0 + 2 * GLA_RANK], LANES)
            wf_pad = jnp.pad(gla_wg_fwd[j], ((0, LANES - GLA_RANK), (0, 0)))
            wb_pad = jnp.pad(gla_wg_bwd[j], ((GLA_RANK, LANES - 2 * GLA_RANK), (0, 0)))
            gf, gb = gla_gates(h, wz_pad, wf_pad, gla_bg_fwd, wb_pad, gla_bg_bwd, j, tm=1152)
            y = gla_scan(qk, vr, gf, gb, gla_norm_g, j)
            xs = mm_resid(y, gla_w_out, j, xs, mods, i, 2, tm=1152, tn=1024)
        else:
            qkv = mm_plain(h, nat_w_in, j, 0, 3 * D, BF16, tm=1152, tn=1024,
                           scale=NAT_DH ** -0.5, n_scaled=D // 1024)
            y = nat_attention(qkv, nat_bias_table(nat_rpb[j]), not last)
            xs = mm_resid(y, nat_w_out, j, xs, mods, i, 2, tm=1152, tn=1024)
        if i % 2 == 0:
            h2 = modulate(xs, norm_ffn_g, mods, i, 3, 4)
            act = mm_swiglu(h2, ffn_w_gate, ffn_w_up, j, tm=1152, tn=512)
            xs = mm_resid(act, ffn_w_down, j, xs, mods, i, 5, tm=576, tn=512)
        else:
            h_words, idx, rank, wcol, counts = modulate_route(xs, norm_ffn_g, mods, i, 3, 4, router_pad, j, last)
            n_slots = moe_slots(BATCH * SEQ if last else T)
            start, up_tiles, down_tiles, part_rows = moe_plan(counts[:, 0], n_slots)
            routing = (idx[0], idx[1], rank[0], rank[1], start)
            xw = moe_dispatch(routing, part_rows, h_words, n_slots, last)
            act = moe_up(up_tiles, xw, moe_w_gate, moe_w_up, j)
            ys = moe_down(down_tiles, act, moe_w_down, j)
            xs = moe_combine(routing, ys, xs, mods, i, 5, wcol, last)
    return final_norm(xs, final_norm_g)
```

```python
import functools

import numpy as np
import jax
import jax.numpy as jnp
from jax import lax
from jax.experimental import pallas as pl
from jax.experimental.pallas import tpu as pltpu

F32 = jnp.float32
BF16 = jnp.bfloat16
HIGHEST = lax.Precision.HIGHEST

D = 2048
BATCH = 4
SEQ = 2048
CTX = 256
ROWS = CTX + SEQ
T = BATCH * ROWS
DEPTH = 4
GRID_W = 64
GRID_H = SEQ // GRID_W
EPS = 1e-6
ROPE_BASE = 10000.0
CTX_GROUP = BATCH
N_MOD = 8

GLA_H = 4
GLA_DK = 256
GLA_DV = 512
GLA_QK = GLA_H * GLA_DK
GLA_RANK = 16
GLA_TAU = 16.0
GLA_C = 64
GLA_SUB = 16

NAT_H = 16
NAT_DH = 128
WIN_R = 8
WIN_C = 16
NAT_ROWS_PER_STEP = 8

D_FF = 5632
N_EXP = 8

LANES = 128
VMEM_LIMIT = 56 * 1024 * 1024

TM_MOD = 256
TM_MODULATE = 768
TM_UP = 1024
TM_DOWN = 512
TM_PART = 256
TF_MOE = 512
TN_MOE_DOWN = 512
TG = 256
DMA_UNROLL = 8


def _params(*sem):
    return pltpu.CompilerParams(dimension_semantics=sem, vmem_limit_bytes=VMEM_LIMIT)


def _dot(a, b):
    return jnp.dot(a, b, preferred_element_type=F32)


def _dot_nt(a, b):
    return lax.dot_general(a, b, (((1,), (1,)), ((), ())), preferred_element_type=F32)


def _dot_tn(a, b):
    return lax.dot_general(a, b, (((0,), (0,)), ((), ())), preferred_element_type=F32)


def _silu(v):
    return v * jax.nn.sigmoid(v)


def _row_gate(tab_ref, m, tm):
    start = m * tm
    b = start // ROWS
    rows = start % ROWS + lax.broadcasted_iota(jnp.int32, (tm, 1), 0)
    per_batch = tab_ref[pl.ds(b, 1), :]
    per_ctx = tab_ref[CTX_GROUP:CTX_GROUP + 1, :]
    return jnp.where(rows < CTX, per_ctx, per_batch)


def _ada_kernel(c_ref, w_ref, b_ref, o_ref):
    a = _silu(c_ref[...]).astype(BF16)
    o_ref[...] = _dot(a, w_ref[...].astype(BF16)) + b_ref[...]


def ada_tables(c8, ada_w, ada_b):
    tn = 1024
    return pl.pallas_call(
        _ada_kernel,
        grid=(DEPTH, 6 * D // tn),
        in_specs=[
            pl.BlockSpec((N_MOD, D), lambda l, n: (0, 0)),
            pl.BlockSpec((None, D, tn), lambda l, n: (l, 0, n)),
            pl.BlockSpec((None, 1, tn), lambda l, n: (l, 0, n)),
        ],
        out_specs=pl.BlockSpec((None, N_MOD, tn), lambda l, n: (l, 0, n)),
        out_shape=jax.ShapeDtypeStruct((DEPTH, N_MOD, 6 * D), F32),
        compiler_params=_params("arbitrary", "arbitrary"),
        name="ada_tables",
    )(c8, ada_w, ada_b.reshape(DEPTH, 1, 6 * D))


def _modulated(x_ref, g_ref, sh_ref, sc_ref):
    x = x_ref[...]
    tm = x.shape[0]
    y = x * lax.rsqrt(jnp.mean(x * x, axis=-1, keepdims=True) + EPS) * g_ref[...]
    i = pl.program_id(0)
    return y * (1.0 + _row_gate(sc_ref, i, tm)) + _row_gate(sh_ref, i, tm)


def _modulate_kernel(x_ref, g_ref, sh_ref, sc_ref, o_ref):
    o_ref[...] = _modulated(x_ref, g_ref, sh_ref, sc_ref).astype(o_ref.dtype)


def _mod_specs(layer, sh_col, sc_col, tm):
    return [
        pl.BlockSpec((tm, D), lambda i: (i, 0)),
        pl.BlockSpec((None, 1, D), lambda i: (layer, 0, 0)),
        pl.BlockSpec((None, N_MOD, D), lambda i: (layer, 0, sh_col)),
        pl.BlockSpec((None, N_MOD, D), lambda i: (layer, 0, sc_col)),
    ]


def modulate(x, norm_g, mods, layer, sh_col, sc_col):
    tm = TM_MODULATE
    return pl.pallas_call(
        _modulate_kernel,
        grid=(T // tm,),
        in_specs=_mod_specs(layer, sh_col, sc_col, tm),
        out_specs=pl.BlockSpec((tm, D), lambda i: (i, 0)),
        out_shape=jax.ShapeDtypeStruct((T, D), BF16),
        compiler_params=_params("arbitrary"),
        name="modulate",
    )(x, norm_g.reshape(DEPTH, 1, D), mods, mods)


def _pack_words(h):
    half = h.shape[1] // 2
    lo = lax.bitcast_convert_type(h[:, :half].astype(BF16).astype(F32), jnp.uint32)
    hi = lax.bitcast_convert_type(h[:, half:].astype(BF16).astype(F32), jnp.uint32)
    return (lo >> 16) | (hi & jnp.uint32(0xFFFF0000))


def _unpack_words(w):
    lo = lax.bitcast_convert_type(w << 16, F32).astype(BF16)
    hi = lax.bitcast_convert_type(w & jnp.uint32(0xFFFF0000), F32).astype(BF16)
    return lo, hi


def _modulate_route_kernel(x_ref, g_ref, sh_ref, sc_ref, rt_ref, hw_ref, idx_ref, rank_ref, wcol_ref, cnt_ref,
                           base_ref, *, latent_only):
    i = pl.program_id(0)

    @pl.when(i == 0)
    def _():
        base_ref[...] = jnp.zeros_like(base_ref)

    h = _modulated(x_ref, g_ref, sh_ref, sc_ref)
    hw_ref[...] = _pack_words(h)
    logits = jnp.dot(h, rt_ref[...], precision=HIGHEST, preferred_element_type=F32)

    lane = lax.broadcasted_iota(jnp.int32, logits.shape, 1)
    lm = jnp.where(lane < N_EXP, logits, -jnp.inf)
    c1 = jnp.max(lm, axis=1, keepdims=True)
    j1 = jnp.min(jnp.where(lm == c1, lane, LANES), axis=1, keepdims=True)
    c2 = jnp.max(jnp.where(lane == j1, -jnp.inf, lm), axis=1, keepdims=True)
    e = jnp.exp(c2 - c1)
    wcol_ref[...] = jnp.where(lane == 0, 1.0 / (1.0 + e), jnp.where(lane == 1, e / (1.0 + e), 0.0))

    lt = logits.T[:N_EXP]
    eid = lax.broadcasted_iota(jnp.int32, lt.shape, 0)
    m1 = jnp.max(lt, axis=0, keepdims=True)
    i1 = jnp.min(jnp.where(lt == m1, eid, N_EXP), axis=0, keepdims=True)
    lt2 = jnp.where(eid == i1, -jnp.inf, lt)
    m2 = jnp.max(lt2, axis=0, keepdims=True)
    i2 = jnp.min(jnp.where(lt2 == m2, eid, N_EXP), axis=0, keepdims=True)
    idx_ref[...] = jnp.concatenate([i1, i2], axis=0)

    routed = 1.0
    if latent_only:
        routed = jnp.where(i % (ROWS // TM_MOD) < CTX // TM_MOD, 0.0, 1.0)
    oh1 = jnp.where(eid == i1, routed, 0.0)
    oh2 = jnp.where(eid == i2, routed, 0.0)
    both = oh1 + oh2
    tj = lax.broadcasted_iota(jnp.int32, (TM_MOD, TM_MOD), 0)
    tt = lax.broadcasted_iota(jnp.int32, (TM_MOD, TM_MOD), 1)
    before = _dot(both.astype(BF16), jnp.where(tj < tt, 1.0, 0.0).astype(BF16))
    seen = base_ref[:, 0:1] + before
    r1 = jnp.sum(oh1 * seen, axis=0, keepdims=True)
    r2 = jnp.sum(oh2 * seen, axis=0, keepdims=True)
    rank_ref[...] = jnp.concatenate([r1, r2], axis=0).astype(jnp.int32)
    total = base_ref[...] + jnp.sum(both, axis=1, keepdims=True)
    base_ref[...] = total
    cnt_ref[...] = total.astype(jnp.int32)


def modulate_route(x, norm_g, mods, layer, sh_col, sc_col, router_pad, j, latent_only):
    pairs = pl.BlockSpec((2, TM_MOD), lambda i: (0, i))
    return pl.pallas_call(
        functools.partial(_modulate_route_kernel, latent_only=latent_only),
        grid=(T // TM_MOD,),
        in_specs=_mod_specs(layer, sh_col, sc_col, TM_MOD) + [
            pl.BlockSpec((None, D, LANES), lambda i: (j, 0, 0)),
        ],
        out_specs=[
            pl.BlockSpec((TM_MOD, D // 2), lambda i: (i, 0)),
            pairs, pairs,
            pl.BlockSpec((TM_MOD, LANES), lambda i: (i, 0)),
            pl.BlockSpec((N_EXP, LANES), lambda i: (0, 0)),
        ],
        out_shape=[
            jax.ShapeDtypeStruct((T, D // 2), jnp.uint32),
            jax.ShapeDtypeStruct((2, T), jnp.int32),
            jax.ShapeDtypeStruct((2, T), jnp.int32),
            jax.ShapeDtypeStruct((T, LANES), F32),
            jax.ShapeDtypeStruct((N_EXP, LANES), jnp.int32),
        ],
        scratch_shapes=[pltpu.VMEM((N_EXP, LANES), F32)],
        compiler_params=_params("arbitrary"),
        name="modulate_route",
    )(x, norm_g.reshape(DEPTH, 1, D), mods, mods, router_pad)


def _cast_weight(w_ref, wb_ref):
    @pl.when(pl.program_id(1) == 0)
    def _():
        wb_ref[...] = w_ref[...].astype(BF16)


def _mm_plain_kernel(a_ref, w_ref, o_ref, wb_ref, *, scale, n_scaled):
    _cast_weight(w_ref, wb_ref)
    acc = _dot(a_ref[...], wb_ref[...])
    if n_scaled:
        acc = acc * jnp.where(pl.program_id(0) < n_scaled, scale, 1.0)
    o_ref[...] = acc.astype(o_ref.dtype)


def mm_plain(a, w, layer, col0, n_out, out_dtype, tm, tn, scale=1.0, n_scaled=0):
    k = a.shape[1]
    c0 = col0 // tn
    return pl.pallas_call(
        functools.partial(_mm_plain_kernel, scale=scale, n_scaled=n_scaled),
        grid=(n_out // tn, T // tm),
        in_specs=[
            pl.BlockSpec((tm, k), lambda n, m: (m, 0)),
            pl.BlockSpec((None, k, tn), lambda n, m: (layer, 0, n + c0)),
        ],
        out_specs=pl.BlockSpec((tm, tn), lambda n, m: (m, n)),
        out_shape=jax.ShapeDtypeStruct((T, n_out), out_dtype),
        scratch_shapes=[pltpu.VMEM((k, tn), BF16)],
        compiler_params=_params("arbitrary", "arbitrary"),
        name="mm_plain",
    )(a, w)


def _mm_rope_kernel(a_ref, w_ref, cos_ref, sin_ref, o_ref, wb_ref, *, tn):
    _cast_weight(w_ref, wb_ref)
    acc = _dot(a_ref[...], wb_ref[...])
    scale = jnp.where(pl.program_id(0) == 0, GLA_DK ** -0.5, 1.0)
    for s in range(tn // LANES):
        xs = acc[:, s * LANES:(s + 1) * LANES]
        t = (s % 2) * LANES
        rot = xs * cos_ref[:, t:t + LANES] + pltpu.roll(xs, LANES // 2, 1) * sin_ref[:, t:t + LANES]
        o_ref[:, s * LANES:(s + 1) * LANES] = rot * scale


def mm_rope(a, w, layer, cos_t, sin_t, tm):
    tn = GLA_QK
    per_batch = ROWS // tm
    return pl.pallas_call(
        functools.partial(_mm_rope_kernel, tn=tn),
        grid=(2, T // tm),
        in_specs=[
            pl.BlockSpec((tm, D), lambda n, m: (m, 0)),
            pl.BlockSpec((None, D, tn), lambda n, m: (layer, 0, n)),
            pl.BlockSpec((tm, GLA_DK), lambda n, m: (m % per_batch, 0)),
            pl.BlockSpec((tm, GLA_DK), lambda n, m: (m % per_batch, 0)),
        ],
        out_specs=pl.BlockSpec((tm, tn), lambda n, m: (m, n)),
        out_shape=jax.ShapeDtypeStruct((T, 2 * GLA_QK), F32),
        scratch_shapes=[pltpu.VMEM((D, tn), BF16)],
        compiler_params=_params("arbitrary", "arbitrary"),
        name="mm_rope",
    )(a, w, cos_t, sin_t)


def _mm_resid_kernel(a_ref, w_ref, x_ref, gt_ref, o_ref, wb_ref, *, tm):
    _cast_weight(w_ref, wb_ref)
    acc = _dot(a_ref[...], wb_ref[...])
    o_ref[...] = x_ref[...] + _row_gate(gt_ref, pl.program_id(1), tm) * acc


def mm_resid(a, w, layer_w, x, mods, layer, gate_col, tm, tn):
    k = a.shape[1]
    g0 = gate_col * D // tn
    return pl.pallas_call(
        functools.partial(_mm_resid_kernel, tm=tm),
        grid=(D // tn, T // tm),
        in_specs=[
            pl.BlockSpec((tm, k), lambda n, m: (m, 0)),
            pl.BlockSpec((None, k, tn), lambda n, m: (layer_w, 0, n)),
            pl.BlockSpec((tm, tn), lambda n, m: (m, n)),
            pl.BlockSpec((None, N_MOD, tn), lambda n, m: (layer, 0, g0 + n)),
        ],
        out_specs=pl.BlockSpec((tm, tn), lambda n, m: (m, n)),
        out_shape=jax.ShapeDtypeStruct((T, D), F32),
        scratch_shapes=[pltpu.VMEM((k, tn), BF16)],
        compiler_params=_params("arbitrary", "arbitrary"),
        name="mm_resid",
    )(a, w, x, mods)


def _mm_swiglu_kernel(a_ref, wg_ref, wu_ref, o_ref, wgb_ref, wub_ref):
    _cast_weight(wg_ref, wgb_ref)
    _cast_weight(wu_ref, wub_ref)
    a = a_ref[...]
    o_ref[...] = (_silu(_dot(a, wgb_ref[...])) * _dot(a, wub_ref[...])).astype(o_ref.dtype)


def mm_swiglu(a, wg, wu, layer, tm, tn):
    return pl.pallas_call(
        _mm_swiglu_kernel,
        grid=(D_FF // tn, T // tm),
        in_specs=[
            pl.BlockSpec((tm, D), lambda n, m: (m, 0)),
            pl.BlockSpec((None, D, tn), lambda n, m: (layer, 0, n)),
            pl.BlockSpec((None, D, tn), lambda n, m: (layer, 0, n)),
        ],
        out_specs=pl.BlockSpec((tm, tn), lambda n, m: (m, n)),
        out_shape=jax.ShapeDtypeStruct((T, D_FF), BF16),
        scratch_shapes=[pltpu.VMEM((D, tn), BF16), pltpu.VMEM((D, tn), BF16)],
        compiler_params=_params("arbitrary", "arbitrary"),
        name="mm_swiglu",
    )(a, wg, wu)


def _log_sigmoid(v):
    return jnp.minimum(v, 0.0) - jnp.log(1.0 + jnp.exp(-jnp.abs(v)))


def _gla_gate_kernel(h_ref, wz_ref, wf_ref, bf_ref, wb_ref, bb_ref, gf_ref, gb_ref):
    z = _dot(h_ref[...], wz_ref[...].astype(BF16))
    af = jnp.dot(z, wf_ref[...], precision=HIGHEST, preferred_element_type=F32) + bf_ref[...]
    ab = jnp.dot(z, wb_ref[...], precision=HIGHEST, preferred_element_type=F32) + bb_ref[...]
    gf_ref[...] = _log_sigmoid(af) / GLA_TAU
    gb_ref[...] = _log_sigmoid(ab) / GLA_TAU


def gla_gates(h, wz_pad, wf_pad, bg_f, wb_pad, bg_b, j, tm):
    row = lambda i: (i, 0)
    return pl.pallas_call(
        _gla_gate_kernel,
        grid=(T // tm,),
        in_specs=[
            pl.BlockSpec((tm, D), row),
            pl.BlockSpec((D, LANES), lambda i: (0, 0)),
            pl.BlockSpec((LANES, GLA_QK), lambda i: (0, 0)),
            pl.BlockSpec((None, 1, GLA_QK), lambda i: (j, 0, 0)),
            pl.BlockSpec((LANES, GLA_QK), lambda i: (0, 0)),
            pl.BlockSpec((None, 1, GLA_QK), lambda i: (j, 0, 0)),
        ],
        out_specs=[pl.BlockSpec((tm, GLA_QK), row), pl.BlockSpec((tm, GLA_QK), row)],
        out_shape=[jax.ShapeDtypeStruct((T, GLA_QK), F32)] * 2,
        compiler_params=_params("arbitrary"),
        name="gla_gates",
    )(h, wz_pad, wf_pad, bg_f.reshape(-1, 1, GLA_QK), wb_pad, bg_b.reshape(-1, 1, GLA_QK))


def _gla_chunks(streams):
    c, sub, nsub = GLA_C, GLA_SUB, GLA_C // GLA_SUB
    ri = lax.broadcasted_iota(jnp.int32, (c, c), 0)
    ci = lax.broadcasted_iota(jnp.int32, (c, c), 1)
    lane8 = lax.broadcasted_iota(jnp.int32, (8, c), 1)
    srow8 = lax.broadcasted_iota(jnp.int32, (8, 1), 0)

    bcs = []
    for q, k, v, g, st, rev in streams:
        tri = (ci >= ri if rev else ci <= ri).astype(F32)
        bcs.append(jnp.dot(tri, g, precision=HIGHEST, preferred_element_type=F32))

    partial = []
    for (q, k, v, g, st, rev), bc in zip(streams, bcs):
        b_end = bc[0:1] if rev else bc[c - 1:c]
        o = _dot_nt((q * jnp.exp(bc)).astype(BF16), st.astype(BF16))
        st_new = st * jnp.exp(b_end) + _dot_tn(v, (k * jnp.exp(b_end - bc)).astype(BF16))
        off = []
        for bi in range(nsub):
            lo, hi = sub * bi, sub * (bi + 1)
            q_i, b_i = q[lo:hi], bc[lo:hi]
            if not rev and bi > 0:
                ref = bc[lo - 1:lo]
                kt = jnp.concatenate([k[:lo] * jnp.exp(ref - bc[:lo]), jnp.zeros((c - lo, GLA_DK), F32)], axis=0)
                off.append(_dot_nt((q_i * jnp.exp(b_i - ref)).astype(BF16), kt.astype(BF16)))
            elif rev and bi < nsub - 1:
                ref = bc[hi:hi + 1]
                kt = jnp.concatenate([jnp.zeros((hi, GLA_DK), F32), k[hi:] * jnp.exp(ref - bc[hi:])], axis=0)
                off.append(_dot_nt((q_i * jnp.exp(b_i - ref)).astype(BF16), kt.astype(BF16)))
            else:
                off.append(jnp.zeros((sub, c), F32))
        partial.append((o, st_new, off))

    intra = []
    for (q, k, v, g, st, rev), bc, (o, st_new, off) in zip(streams, bcs, partial):
        blocks = []
        for bi in range(nsub):
            lo = sub * bi
            for r0 in range(0, sub, 8):
                q_r, b_r = q[lo + r0:lo + r0 + 8], bc[lo + r0:lo + r0 + 8]
                a_r = off[bi][r0:r0 + 8]
                for jj in range(sub):
                    if (jj > r0 + 7) if not rev else (jj < r0):
                        continue
                    t = q_r * k[lo + jj:lo + jj + 1] * jnp.exp(b_r - bc[lo + jj:lo + jj + 1])
                    s = jnp.sum(t, axis=-1, keepdims=True)
                    seen = (srow8 + r0 <= jj) if rev else (srow8 + r0 >= jj)
                    a_r = jnp.where(lane8 == lo + jj, jnp.where(seen, s, 0.0), a_r)
                blocks.append(a_r)
        intra.append(jnp.concatenate(blocks, axis=0).astype(BF16))

    return [(o + _dot(a, v), st_new) for (q, k, v, g, st, rev), (o, st_new, off), a in zip(streams, partial, intra)]


def _gla_kernel(q_ref, k_ref, v_ref, r_ref, gf_ref, gb_ref, ng_ref, y_ref, stf_ref, stb_ref, of_ref, ob_ref):
    nch = ROWS // GLA_C
    nctx = CTX // GLA_C

    def chunk_rows(ch):
        return pl.ds(pl.multiple_of(ch * GLA_C, GLA_C), GLA_C)

    def stream(rows, g_ref, st_ref, rev):
        return q_ref[rows, :], k_ref[rows, :], v_ref[rows, :], g_ref[rows, :], st_ref[...], rev

    stf_ref[...] = jnp.zeros_like(stf_ref)
    stb_ref[...] = jnp.zeros_like(stb_ref)

    def scan(s, carry):
        rows_f = chunk_rows(s)
        rows_b = chunk_rows(jnp.where(s < nctx, nctx - 1 - s, nch + nctx - 1 - s))
        (o_f, st_f), (o_b, st_b) = _gla_chunks([stream(rows_f, gf_ref, stf_ref, False),
                                                stream(rows_b, gb_ref, stb_ref, True)])
        of_ref[rows_f, :] = o_f
        ob_ref[rows_b, :] = o_b
        stf_ref[...] = st_f
        stb_ref[...] = st_b
        return carry

    lax.fori_loop(0, nch, scan, 0)

    def readout(ch, carry):
        rows = chunk_rows(ch)
        o = of_ref[rows, :] + ob_ref[rows, :]
        on = o * lax.rsqrt(jnp.mean(o * o, axis=-1, keepdims=True) + EPS) * ng_ref[...]
        y_ref[rows, :] = (on * _silu(r_ref[rows, :].astype(F32))).astype(y_ref.dtype)
        return carry

    lax.fori_loop(0, nch, readout, 0)


def gla_scan(qk, vr, gf, gb, norm_g, j):
    return pl.pallas_call(
        _gla_kernel,
        grid=(BATCH, GLA_H),
        in_specs=[
            pl.BlockSpec((ROWS, GLA_DK), lambda b, h: (b, h)),
            pl.BlockSpec((ROWS, GLA_DK), lambda b, h: (b, GLA_H + h)),
            pl.BlockSpec((ROWS, GLA_DV), lambda b, h: (b, h)),
            pl.BlockSpec((ROWS, GLA_DV), lambda b, h: (b, GLA_H + h)),
            pl.BlockSpec((ROWS, GLA_DK), lambda b, h: (b, h)),
            pl.BlockSpec((ROWS, GLA_DK), lambda b, h: (b, h)),
            pl.BlockSpec((None, 1, GLA_DV), lambda b, h: (j, 0, 0)),
        ],
        out_specs=pl.BlockSpec((ROWS, GLA_DV), lambda b, h: (b, h)),
        out_shape=jax.ShapeDtypeStruct((T, D), BF16),
        scratch_shapes=[pltpu.VMEM((GLA_DV, GLA_DK), F32), pltpu.VMEM((GLA_DV, GLA_DK), F32),
                        pltpu.VMEM((ROWS, GLA_DV), F32), pltpu.VMEM((ROWS, GLA_DV), F32)],
        compiler_params=_params("arbitrary", "arbitrary"),
        name="gla_scan",
    )(qk, qk, vr, vr, gf, gb, norm_g.reshape(-1, 1, GLA_DV))


def rope_tables():
    half = GLA_DK // 2
    freqs = ROPE_BASE ** (-np.arange(0, half, 2, dtype=np.float32) / half)
    t = np.arange(SEQ)
    pos = np.stack([t // GRID_W, t % GRID_W], axis=-1).astype(np.float32)
    ang = pos[:, :, None] * freqs
    cos, sin = np.cos(ang), np.sin(ang)
    cos_t = np.concatenate([cos, cos], axis=-1).reshape(SEQ, GLA_DK)
    sin_t = np.concatenate([-sin, sin], axis=-1).reshape(SEQ, GLA_DK)
    cos_t = np.concatenate([np.ones((CTX, GLA_DK), np.float32), cos_t], axis=0)
    sin_t = np.concatenate([np.zeros((CTX, GLA_DK), np.float32), sin_t], axis=0)
    return jnp.asarray(cos_t, F32), jnp.asarray(sin_t, F32)


def _nat_kernel(q_ref, k_ref, v_ref, bias_ref, y_ref, *, need_ctx):
    kc = k_ref[0:CTX, :]
    vc = v_ref[0:CTX, :]
    if need_ctx:
        s = _dot_nt(q_ref[0:CTX, :], kc)
        p = jnp.exp(s - jnp.max(s, axis=-1, keepdims=True))
        o = _dot(p.astype(BF16), vc) / jnp.sum(p, axis=-1, keepdims=True)
        y_ref[0:CTX, :] = o.astype(y_ref.dtype)
    else:
        y_ref[0:CTX, :] = jnp.zeros((CTX, NAT_DH), y_ref.dtype)
    n_keys = WIN_R * GRID_W

    def scores(r):
        rs = jnp.clip(r - WIN_R // 2, 0, GRID_H - WIN_R)
        q0 = pl.multiple_of(CTX + r * GRID_W, GRID_W)
        k0 = pl.multiple_of(CTX + rs * GRID_W, GRID_W)
        q = q_ref[pl.ds(q0, GRID_W), :]
        s_lat = _dot_nt(q, k_ref[pl.ds(k0, n_keys), :]) + bias_ref[r - rs]
        s_ctx = _dot_nt(q, kc)
        return q0, k0, s_lat, s_ctx

    def probs(q0, k0, s_lat, s_ctx):
        m = jnp.maximum(jnp.max(s_lat, axis=-1, keepdims=True), jnp.max(s_ctx, axis=-1, keepdims=True))
        p_lat = jnp.exp(s_lat - m)
        p_ctx = jnp.exp(s_ctx - m)
        denom = jnp.sum(p_lat, axis=-1, keepdims=True) + jnp.sum(p_ctx, axis=-1, keepdims=True)
        return q0, k0, p_lat.astype(BF16), p_ctx.astype(BF16), denom

    def values(q0, k0, p_lat, p_ctx, denom):
        o = _dot(p_lat, v_ref[pl.ds(k0, n_keys), :]) + _dot(p_ctx, vc)
        return q0, (o / denom).astype(y_ref.dtype)

    def body(it, carry):
        rows = [it * NAT_ROWS_PER_STEP + u for u in range(NAT_ROWS_PER_STEP)]
        outs = [values(*pr) for pr in [probs(*sc) for sc in [scores(r) for r in rows]]]
        for q0, o in outs:
            y_ref[pl.ds(q0, GRID_W), :] = o
        return carry

    lax.fori_loop(0, GRID_H // NAT_ROWS_PER_STEP, body, 0)


def nat_attention(qkv, bias_tbl, need_ctx):
    return pl.pallas_call(
        functools.partial(_nat_kernel, need_ctx=need_ctx),
        grid=(NAT_H, BATCH),
        in_specs=[
            pl.BlockSpec((ROWS, NAT_DH), lambda h, b: (b, h)),
            pl.BlockSpec((ROWS, NAT_DH), lambda h, b: (b, NAT_H + h)),
            pl.BlockSpec((ROWS, NAT_DH), lambda h, b: (b, 2 * NAT_H + h)),
            pl.BlockSpec((None, WIN_R, GRID_W, WIN_R * GRID_W), lambda h, b: (h, 0, 0, 0)),
        ],
        out_specs=pl.BlockSpec((ROWS, NAT_DH), lambda h, b: (b, h)),
        out_shape=jax.ShapeDtypeStruct((T, D), BF16),
        compiler_params=_params("arbitrary", "arbitrary"),
        name="nat_attention",
    )(qkv, qkv, qkv, bias_tbl)


def nat_bias_table(rpb):
    qc = np.arange(GRID_W)[:, None]
    kcol = np.arange(GRID_W)[None, :]
    start = np.clip(qc - WIN_C // 2, 0, GRID_W - WIN_C)
    in_win = (kcol >= start) & (kcol < start + WIN_C)
    dc = np.clip(kcol - qc + WIN_C - 1, 0, 2 * WIN_C - 2)
    pick = jnp.asarray(dc[None] == np.arange(2 * WIN_C - 1)[:, None, None], F32)
    by_col = jnp.einsum("hdc,cqk->hqdk", rpb, pick, precision=HIGHEST)
    by_col = jnp.where(in_win[None, :, None, :], by_col, -jnp.inf)
    tbl = jnp.stack([by_col[:, :, WIN_R - 1 - off:2 * WIN_R - 1 - off, :] for off in range(WIN_R)], axis=1)
    return tbl.reshape(NAT_H, WIN_R, GRID_W, WIN_R * GRID_W)


def _pair_slots(e0_ref, e1_ref, r0_ref, r1_ref, start_ref, t):
    return start_ref[e0_ref[t]] + r0_ref[t], start_ref[e1_ref[t]] + r1_ref[t]


def _is_routed_tile(i, latent_only):
    return (i % (ROWS // TG) >= CTX // TG) if latent_only else None


def _dispatch_kernel(e0_ref, e1_ref, r0_ref, r1_ref, start_ref, part_ref, h_ref, o_ref, zero_ref, sem, *,
                     latent_only):
    i = pl.program_id(0)
    base = i * TG

    @pl.when(i == 0)
    def _():
        zero_ref[...] = jnp.zeros_like(zero_ref)

        def zero_copy(p):
            return pltpu.make_async_copy(zero_ref, o_ref.at[pl.ds(pl.multiple_of(p * TM_PART, TM_PART), TM_PART)],
                                         sem.at[2])

        def start(p, carry):
            pl.when(part_ref[p] < TM_PART)(lambda: zero_copy(p).start())
            return carry

        def wait(p, carry):
            pl.when(part_ref[p] < TM_PART)(lambda: zero_copy(p).wait())
            return carry

        n_parts = o_ref.shape[0] // TM_PART
        lax.fori_loop(0, n_parts, start, 0)
        lax.fori_loop(0, n_parts, wait, 0)

    def start_rows(r, carry):
        row = h_ref.at[pl.ds(r, 1)]
        s0, s1 = _pair_slots(e0_ref, e1_ref, r0_ref, r1_ref, start_ref, base + r)
        pltpu.make_async_copy(row, o_ref.at[pl.ds(s0, 1)], sem.at[0]).start()
        pltpu.make_async_copy(row, o_ref.at[pl.ds(s1, 1)], sem.at[1]).start()
        return carry

    def run():
        lax.fori_loop(0, TG, start_rows, 0, unroll=DMA_UNROLL)
        for s in range(2):
            pltpu.make_async_copy(h_ref, o_ref.at[pl.ds(0, TG)], sem.at[s]).wait()

    routed = _is_routed_tile(i, latent_only)
    if routed is None:
        run()
    else:
        pl.when(routed)(run)


def moe_dispatch(routing, part_rows, h_words, n_slots, latent_only):
    words = h_words.shape[1]
    return pl.pallas_call(
        functools.partial(_dispatch_kernel, latent_only=latent_only),
        grid_spec=pltpu.PrefetchScalarGridSpec(
            num_scalar_prefetch=6,
            grid=(T // TG,),
            in_specs=[pl.BlockSpec((TG, words), lambda i, *_: (i, 0))],
            out_specs=pl.BlockSpec(memory_space=pl.ANY),
            scratch_shapes=[pltpu.VMEM((TM_PART, words), jnp.uint32), pltpu.SemaphoreType.DMA((3,))],
        ),
        out_shape=jax.ShapeDtypeStruct((n_slots, words), jnp.uint32),
        compiler_params=_params("arbitrary"),
        name="moe_dispatch",
    )(*routing, part_rows, h_words)


def _expert_changed(te_ref, i):
    return (i == 0) | (te_ref[i] != te_ref[jnp.maximum(i - 1, 0)])


def _on_occupied_prefix(n_rows, tile, o_ref, compute):
    for parts in range(tile // TM_PART + 1):
        p = parts * TM_PART

        @pl.when((n_rows > p - TM_PART) & (n_rows <= p))
        def _():
            if p > 0:
                o_ref[0:p, :] = compute(p)
            if p < tile:
                o_ref[p:tile, :] = jnp.zeros((tile - p, o_ref.shape[1]), o_ref.dtype)


def _moe_up_kernel(te_ref, tr_ref, ts_ref, to_ref, a_ref, wg_ref, wu_ref, o_ref, wgb_ref, wub_ref):
    del ts_ref, to_ref
    i = pl.program_id(1)

    @pl.when(_expert_changed(te_ref, i))
    def _():
        wgb_ref[...] = wg_ref[...].astype(BF16)
        wub_ref[...] = wu_ref[...].astype(BF16)

    half = D // 2

    def compute(p):
        lo, hi = _unpack_words(a_ref[0:p, :])
        gate = _dot(lo, wgb_ref[0:half, :]) + _dot(hi, wgb_ref[half:D, :])
        up = _dot(lo, wub_ref[0:half, :]) + _dot(hi, wub_ref[half:D, :])
        return (_silu(gate) * up).astype(o_ref.dtype)

    _on_occupied_prefix(tr_ref[i], TM_UP, o_ref, compute)


def moe_up(tiles, xw, wg, wu, j):
    n_slots = xw.shape[0]
    w_spec = pl.BlockSpec((None, None, D, TF_MOE), lambda f, i, te, tr, ts, to: (j, te[i], 0, f))
    return pl.pallas_call(
        _moe_up_kernel,
        grid_spec=pltpu.PrefetchScalarGridSpec(
            num_scalar_prefetch=4,
            grid=(D_FF // TF_MOE, n_slots // TM_UP),
            in_specs=[pl.BlockSpec((TM_UP, D // 2), lambda f, i, te, tr, ts, to: (ts[i], 0)), w_spec, w_spec],
            out_specs=pl.BlockSpec((TM_UP, TF_MOE), lambda f, i, te, tr, ts, to: (to[i], f)),
            scratch_shapes=[pltpu.VMEM((D, TF_MOE), BF16), pltpu.VMEM((D, TF_MOE), BF16)],
        ),
        out_shape=jax.ShapeDtypeStruct((n_slots, D_FF), BF16),
        compiler_params=_params("arbitrary", "arbitrary"),
        name="moe_up",
    )(*tiles, xw, wg, wu)


def _moe_down_kernel(te_ref, tr_ref, ts_ref, to_ref, a_ref, w_ref, o_ref, wb_ref):
    del ts_ref, to_ref
    i = pl.program_id(1)

    @pl.when(_expert_changed(te_ref, i))
    def _():
        wb_ref[...] = w_ref[...].astype(BF16)

    _on_occupied_prefix(tr_ref[i], TM_DOWN, o_ref, lambda p: _dot(a_ref[0:p, :], wb_ref[...]))


def moe_down(tiles, act, wd, j):
    n_slots = act.shape[0]
    tn = TN_MOE_DOWN
    return pl.pallas_call(
        _moe_down_kernel,
        grid_spec=pltpu.PrefetchScalarGridSpec(
            num_scalar_prefetch=4,
            grid=(D // tn, n_slots // TM_DOWN),
            in_specs=[
                pl.BlockSpec((TM_DOWN, D_FF), lambda n, i, te, tr, ts, to: (ts[i], 0)),
                pl.BlockSpec((None, None, D_FF, tn), lambda n, i, te, tr, ts, to: (j, te[i], 0, n)),
            ],
            out_specs=pl.BlockSpec((TM_DOWN, tn), lambda n, i, te, tr, ts, to: (to[i], n)),
            scratch_shapes=[pltpu.VMEM((D_FF, tn), BF16)],
        ),
        out_shape=jax.ShapeDtypeStruct((n_slots, D), F32),
        compiler_params=_params("arbitrary", "arbitrary"),
        name="moe_down",
    )(*tiles, act, wd)


def _combine_kernel(e0_ref, e1_ref, r0_ref, r1_ref, start_ref, ys_ref, x_ref, gt_ref, w_ref, o_ref,
                    b0_ref, b1_ref, sem, *, latent_only):
    i = pl.program_id(0)
    base = i * TG

    def start_rows(r, carry):
        s0, s1 = _pair_slots(e0_ref, e1_ref, r0_ref, r1_ref, start_ref, base + r)
        pltpu.make_async_copy(ys_ref.at[pl.ds(s0, 1)], b0_ref.at[pl.ds(r, 1)], sem.at[0]).start()
        pltpu.make_async_copy(ys_ref.at[pl.ds(s1, 1)], b1_ref.at[pl.ds(r, 1)], sem.at[1]).start()
        return carry

    def run():
        lax.fori_loop(0, TG, start_rows, 0, unroll=DMA_UNROLL)
        pltpu.make_async_copy(ys_ref.at[pl.ds(0, TG)], b0_ref, sem.at[0]).wait()
        pltpu.make_async_copy(ys_ref.at[pl.ds(0, TG)], b1_ref, sem.at[1]).wait()
        mix = w_ref[:, 0:1] * b0_ref[...] + w_ref[:, 1:2] * b1_ref[...]
        o_ref[...] = x_ref[...] + _row_gate(gt_ref, i, TG) * mix

    routed = _is_routed_tile(i, latent_only)
    if routed is None:
        run()
    else:
        pl.when(routed)(run)

        @pl.when(jnp.logical_not(routed))
        def _():
            o_ref[...] = x_ref[...]


def moe_combine(routing, ys, x, mods, layer, gate_col, wcol, latent_only):
    return pl.pallas_call(
        functools.partial(_combine_kernel, latent_only=latent_only),
        grid_spec=pltpu.PrefetchScalarGridSpec(
            num_scalar_prefetch=5,
            grid=(T // TG,),
            in_specs=[
                pl.BlockSpec(memory_space=pl.ANY),
                pl.BlockSpec((TG, D), lambda i, *_: (i, 0)),
                pl.BlockSpec((None, N_MOD, D), lambda i, *_: (layer, 0, gate_col)),
                pl.BlockSpec((TG, LANES), lambda i, *_: (i, 0)),
            ],
            out_specs=pl.BlockSpec((TG, D), lambda i, *_: (i, 0)),
            scratch_shapes=[pltpu.VMEM((TG, D), F32), pltpu.VMEM((TG, D), F32), pltpu.SemaphoreType.DMA((2,))],
        ),
        out_shape=jax.ShapeDtypeStruct((T, D), F32),
        compiler_params=_params("arbitrary"),
        name="moe_combine",
    )(*routing, ys, x, mods, wcol)


def moe_slots(n_tokens):
    return (2 * n_tokens // TM_UP + N_EXP) * TM_UP


def moe_plan(counts, n_slots):
    group = (counts + TM_UP - 1) // TM_UP * TM_UP
    start = jnp.cumsum(group) - group
    end = start + counts
    last_row = jnp.max(jnp.where(counts > 0, end, 0)) - 1

    def of_expert(table, expert):
        return jnp.sum(jnp.where(expert[:, None] == jnp.arange(N_EXP), table, 0), axis=1)

    def occupied_rows(tm):
        ids = jnp.arange(n_slots // tm)
        expert = jnp.sum((start + group)[None, :] <= jnp.minimum(ids * tm, last_row)[:, None], axis=1)
        return ids, expert, jnp.clip(of_expert(end, expert) - ids * tm, 0, tm)

    def tiles(tm):
        ids, expert, rows = occupied_rows(tm)
        first = of_expert(start, expert) // tm
        n_occ = (of_expert(counts, expert) + tm - 1) // tm
        k = ids - first
        tile = jnp.where(k < n_occ, first + (k - 1) % jnp.maximum(n_occ, 1), ids)
        rows_v = jnp.clip(of_expert(end, expert) - tile * tm, 0, tm)
        fetched = jnp.where(rows_v > 0, ids, 0)
        last_fetch = jnp.max(jnp.where(ids[None, :] <= ids[:, None], fetched[None, :], 0), axis=1)
        block = jnp.sum(jnp.where(ids[None, :] == last_fetch[:, None], tile[None, :], 0), axis=1)
        return tuple(a.astype(jnp.int32) for a in (expert, rows_v, block, tile))

    return start.astype(jnp.int32), tiles(TM_UP), tiles(TM_DOWN), occupied_rows(TM_PART)[2].astype(jnp.int32)


def _final_norm_kernel(x_ref, g_ref, o_ref):
    x = x_ref[...]
    o_ref[...] = x * lax.rsqrt(jnp.mean(x * x, axis=-1, keepdims=True) + EPS) * g_ref[...]


def final_norm(x, g):
    per_batch = SEQ // TM_MOD
    tiles = ROWS // TM_MOD
    first = CTX // TM_MOD
    out = pl.pallas_call(
        _final_norm_kernel,
        grid=(BATCH * per_batch,),
        in_specs=[
            pl.BlockSpec((TM_MOD, D), lambda i: ((i // per_batch) * tiles + first + i % per_batch, 0)),
            pl.BlockSpec((1, D), lambda i: (0, 0)),
        ],
        out_specs=pl.BlockSpec((TM_MOD, D), lambda i: (i, 0)),
        out_shape=jax.ShapeDtypeStruct((BATCH * SEQ, D), F32),
        compiler_params=_params("arbitrary"),
        name="final_norm",
    )(x, g.reshape(1, D))
    return out.reshape(BATCH, SEQ, D)


def _pad_cols(w, n):
    return jnp.pad(w, ((0, 0), (0, n - w.shape[1])))


def kernel(x, c, ctx, c_ctx, ada_w, ada_b, norm_mix_g, norm_ffn_g, gla_w_in, gla_wg_fwd, gla_bg_fwd, gla_wg_bwd, gla_bg_bwd, gla_norm_g, gla_w_out, nat_w_in, nat_rpb, nat_w_out, ffn_w_gate, ffn_w_up, ffn_w_down, moe_router, moe_w_gate, moe_w_up, moe_w_down, final_norm_g):
    xs = jnp.concatenate([ctx, x], axis=1).reshape(T, D)
    c8 = jnp.concatenate([c, c_ctx[None, :], jnp.zeros((N_MOD - BATCH - 1, D), F32)], axis=0)
    mods = ada_tables(c8, ada_w, ada_b)
    cos_t, sin_t = rope_tables()
    router_pad = jnp.pad(moe_router, ((0, 0), (0, 0), (0, LANES - N_EXP)))

    for i in range(DEPTH):
        j = i // 2
        last = i == DEPTH - 1
        h = modulate(xs, norm_mix_g, mods, i, 0, 1)
        if i % 2 == 0:
            qk = mm_rope(h, gla_w_in, j, cos_t, sin_t, tm=1152)
            vr = mm_plain(h, gla_w_in, j, 2 * GLA_QK, 2 * D, BF16, tm=1152, tn=1024)
            z0 = 2 * GLA_QK + 2 * D
            wz_pad = _pad_cols(gla_w_in[j, :, z0:z0 + 2 * GLA_RANK], LANES)
            wf_pad = jnp.pad(gla_wg_fwd[j], ((0, LANES - GLA_RANK), (0, 0)))
            wb_pad = jnp.pad(gla_wg_bwd[j], ((GLA_RANK, LANES - 2 * GLA_RANK), (0, 0)))
            gf, gb = gla_gates(h, wz_pad, wf_pad, gla_bg_fwd, wb_pad, gla_bg_bwd, j, tm=1152)
            y = gla_scan(qk, vr, gf, gb, gla_norm_g, j)
            xs = mm_resid(y, gla_w_out, j, xs, mods, i, 2, tm=1152, tn=1024)
        else:
            qkv = mm_plain(h, nat_w_in, j, 0, 3 * D, BF16, tm=1152, tn=1024,
                           scale=NAT_DH ** -0.5, n_scaled=D // 1024)
            y = nat_attention(qkv, nat_bias_table(nat_rpb[j]), not last)
            xs = mm_resid(y, nat_w_out, j, xs, mods, i, 2, tm=1152, tn=1024)
        if i % 2 == 0:
            h2 = modulate(xs, norm_ffn_g, mods, i, 3, 4)
            act = mm_swiglu(h2, ffn_w_gate, ffn_w_up, j, tm=1152, tn=512)
            xs = mm_resid(act, ffn_w_down, j, xs, mods, i, 5, tm=576, tn=512)
        else:
            h_words, idx, rank, wcol, counts = modulate_route(xs, norm_ffn_g, mods, i, 3, 4, router_pad, j, last)
            n_slots = moe_slots(BATCH * SEQ if last else T)
            start, up_tiles, down_tiles, part_rows = moe_plan(counts[:, 0], n_slots)
            routing = (idx[0], idx[1], rank[0], rank[1], start)
            xw = moe_dispatch(routing, part_rows, h_words, n_slots, last)
            act = moe_up(up_tiles, xw, moe_w_gate, moe_w_up, j)
            ys = moe_down(down_tiles, act, moe_w_down, j)
            xs = moe_combine(routing, ys, xs, mods, i, 5, wcol, last)
    return final_norm(xs, final_norm_g)
```

```python
import functools

import numpy as np
import jax
import jax.numpy as jnp
from jax import lax
from jax.experimental import pallas as pl
from jax.experimental.pallas import tpu as pltpu

F32 = jnp.float32
BF16 = jnp.bfloat16
HIGHEST = lax.Precision.HIGHEST

D = 2048
BATCH = 4
SEQ = 2048
CTX = 256
ROWS = CTX + SEQ
T = BATCH * ROWS
DEPTH = 4
GRID_W = 64
GRID_H = SEQ // GRID_W
EPS = 1e-6
ROPE_BASE = 10000.0
CTX_GROUP = BATCH
N_MOD = 8

GLA_H = 4
GLA_DK = 256
GLA_DV = 512
GLA_QK = GLA_H * GLA_DK
GLA_RANK = 16
GLA_TAU = 16.0
GLA_C = 64
GLA_SPLIT_DECAY_MAX = 60.0

NAT_H = 16
NAT_DH = 128
WIN_R = 8
WIN_C = 16
NAT_ROWS_PER_STEP = 8

D_FF = 5632
N_EXP = 8

LANES = 128
VMEM_LIMIT = 56 * 1024 * 1024

TM_MOD = 256
TM_MODULATE = 768
TM_UP = 1024
TM_DOWN = 512
TM_PART = 256
TF_MOE = 512
TN_MOE_DOWN = 512
TG = 256
DMA_UNROLL = 8


def _params(*sem):
    return pltpu.CompilerParams(dimension_semantics=sem, vmem_limit_bytes=VMEM_LIMIT)


def _dot(a, b):
    return jnp.dot(a, b, preferred_element_type=F32)


def _dot_nt(a, b):
    return lax.dot_general(a, b, (((1,), (1,)), ((), ())), preferred_element_type=F32)


def _dot_tn(a, b):
    return lax.dot_general(a, b, (((0,), (0,)), ((), ())), preferred_element_type=F32)


def _silu(v):
    return v * jax.nn.sigmoid(v)


def _row_gate(tab_ref, m, tm):
    start = m * tm
    b = start // ROWS
    rows = start % ROWS + lax.broadcasted_iota(jnp.int32, (tm, 1), 0)
    per_batch = tab_ref[pl.ds(b, 1), :]
    per_ctx = tab_ref[CTX_GROUP:CTX_GROUP + 1, :]
    return jnp.where(rows < CTX, per_ctx, per_batch)


def _ada_kernel(c_ref, w_ref, b_ref, o_ref):
    a = _silu(c_ref[...]).astype(BF16)
    o_ref[...] = _dot(a, w_ref[...].astype(BF16)) + b_ref[...]


def ada_tables(c8, ada_w, ada_b):
    tn = 1024
    return pl.pallas_call(
        _ada_kernel,
        grid=(DEPTH, 6 * D // tn),
        in_specs=[
            pl.BlockSpec((N_MOD, D), lambda l, n: (0, 0)),
            pl.BlockSpec((None, D, tn), lambda l, n: (l, 0, n)),
            pl.BlockSpec((None, 1, tn), lambda l, n: (l, 0, n)),
        ],
        out_specs=pl.BlockSpec((None, N_MOD, tn), lambda l, n: (l, 0, n)),
        out_shape=jax.ShapeDtypeStruct((DEPTH, N_MOD, 6 * D), F32),
        compiler_params=_params("arbitrary", "arbitrary"),
        name="ada_tables",
    )(c8, ada_w, ada_b.reshape(DEPTH, 1, 6 * D))


def _modulated(x_ref, g_ref, sh_ref, sc_ref):
    x = x_ref[...]
    tm = x.shape[0]
    y = x * lax.rsqrt(jnp.mean(x * x, axis=-1, keepdims=True) + EPS) * g_ref[...]
    i = pl.program_id(0)
    return y * (1.0 + _row_gate(sc_ref, i, tm)) + _row_gate(sh_ref, i, tm)


def _modulate_kernel(x_ref, g_ref, sh_ref, sc_ref, o_ref):
    o_ref[...] = _modulated(x_ref, g_ref, sh_ref, sc_ref).astype(o_ref.dtype)


def _mod_specs(layer, sh_col, sc_col, tm):
    return [
        pl.BlockSpec((tm, D), lambda i: (i, 0)),
        pl.BlockSpec((None, 1, D), lambda i: (layer, 0, 0)),
        pl.BlockSpec((None, N_MOD, D), lambda i: (layer, 0, sh_col)),
        pl.BlockSpec((None, N_MOD, D), lambda i: (layer, 0, sc_col)),
    ]


def modulate(x, norm_g, mods, layer, sh_col, sc_col):
    tm = TM_MODULATE
    return pl.pallas_call(
        _modulate_kernel,
        grid=(T // tm,),
        in_specs=_mod_specs(layer, sh_col, sc_col, tm),
        out_specs=pl.BlockSpec((tm, D), lambda i: (i, 0)),
        out_shape=jax.ShapeDtypeStruct((T, D), BF16),
        compiler_params=_params("arbitrary"),
        name="modulate",
    )(x, norm_g.reshape(DEPTH, 1, D), mods, mods)


def _pack_words(h):
    half = h.shape[1] // 2
    lo = lax.bitcast_convert_type(h[:, :half].astype(BF16).astype(F32), jnp.uint32)
    hi = lax.bitcast_convert_type(h[:, half:].astype(BF16).astype(F32), jnp.uint32)
    return (lo >> 16) | (hi & jnp.uint32(0xFFFF0000))


def _unpack_words(w):
    lo = lax.bitcast_convert_type(w << 16, F32).astype(BF16)
    hi = lax.bitcast_convert_type(w & jnp.uint32(0xFFFF0000), F32).astype(BF16)
    return lo, hi


def _modulate_route_kernel(x_ref, g_ref, sh_ref, sc_ref, rt_ref, hw_ref, idx_ref, rank_ref, wcol_ref, cnt_ref,
                           base_ref, *, latent_only):
    i = pl.program_id(0)

    @pl.when(i == 0)
    def _():
        base_ref[...] = jnp.zeros_like(base_ref)

    h = _modulated(x_ref, g_ref, sh_ref, sc_ref)
    hw_ref[...] = _pack_words(h)
    logits = jnp.dot(h, rt_ref[...], precision=HIGHEST, preferred_element_type=F32)

    lane = lax.broadcasted_iota(jnp.int32, logits.shape, 1)
    lm = jnp.where(lane < N_EXP, logits, -jnp.inf)
    c1 = jnp.max(lm, axis=1, keepdims=True)
    j1 = jnp.min(jnp.where(lm == c1, lane, LANES), axis=1, keepdims=True)
    c2 = jnp.max(jnp.where(lane == j1, -jnp.inf, lm), axis=1, keepdims=True)
    e = jnp.exp(c2 - c1)
    wcol_ref[...] = jnp.where(lane == 0, 1.0 / (1.0 + e), jnp.where(lane == 1, e / (1.0 + e), 0.0))

    lt = logits.T[:N_EXP]
    eid = lax.broadcasted_iota(jnp.int32, lt.shape, 0)
    m1 = jnp.max(lt, axis=0, keepdims=True)
    i1 = jnp.min(jnp.where(lt == m1, eid, N_EXP), axis=0, keepdims=True)
    lt2 = jnp.where(eid == i1, -jnp.inf, lt)
    m2 = jnp.max(lt2, axis=0, keepdims=True)
    i2 = jnp.min(jnp.where(lt2 == m2, eid, N_EXP), axis=0, keepdims=True)
    idx_ref[...] = jnp.concatenate([i1, i2], axis=0)

    routed = 1.0
    if latent_only:
        routed = jnp.where(i % (ROWS // TM_MOD) < CTX // TM_MOD, 0.0, 1.0)
    oh1 = jnp.where(eid == i1, routed, 0.0)
    oh2 = jnp.where(eid == i2, routed, 0.0)
    both = oh1 + oh2
    tj = lax.broadcasted_iota(jnp.int32, (TM_MOD, TM_MOD), 0)
    tt = lax.broadcasted_iota(jnp.int32, (TM_MOD, TM_MOD), 1)
    before = _dot(both.astype(BF16), jnp.where(tj < tt, 1.0, 0.0).astype(BF16))
    seen = base_ref[:, 0:1] + before
    r1 = jnp.sum(oh1 * seen, axis=0, keepdims=True)
    r2 = jnp.sum(oh2 * seen, axis=0, keepdims=True)
    rank_ref[...] = jnp.concatenate([r1, r2], axis=0).astype(jnp.int32)
    total = base_ref[...] + jnp.sum(both, axis=1, keepdims=True)
    base_ref[...] = total
    cnt_ref[...] = total.astype(jnp.int32)


def modulate_route(x, norm_g, mods, layer, sh_col, sc_col, router_pad, j, latent_only):
    pairs = pl.BlockSpec((2, TM_MOD), lambda i: (0, i))
    return pl.pallas_call(
        functools.partial(_modulate_route_kernel, latent_only=latent_only),
        grid=(T // TM_MOD,),
        in_specs=_mod_specs(layer, sh_col, sc_col, TM_MOD) + [
            pl.BlockSpec((None, D, LANES), lambda i: (j, 0, 0)),
        ],
        out_specs=[
            pl.BlockSpec((TM_MOD, D // 2), lambda i: (i, 0)),
            pairs, pairs,
            pl.BlockSpec((TM_MOD, LANES), lambda i: (i, 0)),
            pl.BlockSpec((N_EXP, LANES), lambda i: (0, 0)),
        ],
        out_shape=[
            jax.ShapeDtypeStruct((T, D // 2), jnp.uint32),
            jax.ShapeDtypeStruct((2, T), jnp.int32),
            jax.ShapeDtypeStruct((2, T), jnp.int32),
            jax.ShapeDtypeStruct((T, LANES), F32),
            jax.ShapeDtypeStruct((N_EXP, LANES), jnp.int32),
        ],
        scratch_shapes=[pltpu.VMEM((N_EXP, LANES), F32)],
        compiler_params=_params("arbitrary"),
        name="modulate_route",
    )(x, norm_g.reshape(DEPTH, 1, D), mods, mods, router_pad)


def _cast_weight(w_ref, wb_ref):
    @pl.when(pl.program_id(1) == 0)
    def _():
        wb_ref[...] = w_ref[...].astype(BF16)


def _mm_plain_kernel(a_ref, w_ref, o_ref, wb_ref, *, scale, n_scaled):
    _cast_weight(w_ref, wb_ref)
    acc = _dot(a_ref[...], wb_ref[...])
    if n_scaled:
        acc = acc * jnp.where(pl.program_id(0) < n_scaled, scale, 1.0)
    o_ref[...] = acc.astype(o_ref.dtype)


def mm_plain(a, w, layer, col0, n_out, out_dtype, tm, tn, scale=1.0, n_scaled=0):
    k = a.shape[1]
    c0 = col0 // tn
    return pl.pallas_call(
        functools.partial(_mm_plain_kernel, scale=scale, n_scaled=n_scaled),
        grid=(n_out // tn, T // tm),
        in_specs=[
            pl.BlockSpec((tm, k), lambda n, m: (m, 0)),
            pl.BlockSpec((None, k, tn), lambda n, m: (layer, 0, n + c0)),
        ],
        out_specs=pl.BlockSpec((tm, tn), lambda n, m: (m, n)),
        out_shape=jax.ShapeDtypeStruct((T, n_out), out_dtype),
        scratch_shapes=[pltpu.VMEM((k, tn), BF16)],
        compiler_params=_params("arbitrary", "arbitrary"),
        name="mm_plain",
    )(a, w)


def _mm_rope_kernel(a_ref, w_ref, cos_ref, sin_ref, o_ref, wb_ref, *, tn):
    _cast_weight(w_ref, wb_ref)
    acc = _dot(a_ref[...], wb_ref[...])
    scale = jnp.where(pl.program_id(0) == 0, GLA_DK ** -0.5, 1.0)
    for s in range(tn // LANES):
        xs = acc[:, s * LANES:(s + 1) * LANES]
        t = (s % 2) * LANES
        rot = xs * cos_ref[:, t:t + LANES] + pltpu.roll(xs, LANES // 2, 1) * sin_ref[:, t:t + LANES]
        o_ref[:, s * LANES:(s + 1) * LANES] = rot * scale


def mm_rope(a, w, layer, cos_t, sin_t, tm):
    tn = GLA_QK
    per_batch = ROWS // tm
    return pl.pallas_call(
        functools.partial(_mm_rope_kernel, tn=tn),
        grid=(2, T // tm),
        in_specs=[
            pl.BlockSpec((tm, D), lambda n, m: (m, 0)),
            pl.BlockSpec((None, D, tn), lambda n, m: (layer, 0, n)),
            pl.BlockSpec((tm, GLA_DK), lambda n, m: (m % per_batch, 0)),
            pl.BlockSpec((tm, GLA_DK), lambda n, m: (m % per_batch, 0)),
        ],
        out_specs=pl.BlockSpec((tm, tn), lambda n, m: (m, n)),
        out_shape=jax.ShapeDtypeStruct((T, 2 * GLA_QK), F32),
        scratch_shapes=[pltpu.VMEM((D, tn), BF16)],
        compiler_params=_params("arbitrary", "arbitrary"),
        name="mm_rope",
    )(a, w, cos_t, sin_t)


def _mm_resid_kernel(a_ref, w_ref, x_ref, gt_ref, o_ref, wb_ref, *, tm):
    _cast_weight(w_ref, wb_ref)
    acc = _dot(a_ref[...], wb_ref[...])
    o_ref[...] = x_ref[...] + _row_gate(gt_ref, pl.program_id(1), tm) * acc


def mm_resid(a, w, layer_w, x, mods, layer, gate_col, tm, tn):
    k = a.shape[1]
    g0 = gate_col * D // tn
    return pl.pallas_call(
        functools.partial(_mm_resid_kernel, tm=tm),
        grid=(D // tn, T // tm),
        in_specs=[
            pl.BlockSpec((tm, k), lambda n, m: (m, 0)),
            pl.BlockSpec((None, k, tn), lambda n, m: (layer_w, 0, n)),
            pl.BlockSpec((tm, tn), lambda n, m: (m, n)),
            pl.BlockSpec((None, N_MOD, tn), lambda n, m: (layer, 0, g0 + n)),
        ],
        out_specs=pl.BlockSpec((tm, tn), lambda n, m: (m, n)),
        out_shape=jax.ShapeDtypeStruct((T, D), F32),
        scratch_shapes=[pltpu.VMEM((k, tn), BF16)],
        compiler_params=_params("arbitrary", "arbitrary"),
        name="mm_resid",
    )(a, w, x, mods)


def _mm_swiglu_kernel(a_ref, wg_ref, wu_ref, o_ref, wgb_ref, wub_ref):
    _cast_weight(wg_ref, wgb_ref)
    _cast_weight(wu_ref, wub_ref)
    a = a_ref[...]
    o_ref[...] = (_silu(_dot(a, wgb_ref[...])) * _dot(a, wub_ref[...])).astype(o_ref.dtype)


def mm_swiglu(a, wg, wu, layer, tm, tn):
    return pl.pallas_call(
        _mm_swiglu_kernel,
        grid=(D_FF // tn, T // tm),
        in_specs=[
            pl.BlockSpec((tm, D), lambda n, m: (m, 0)),
            pl.BlockSpec((None, D, tn), lambda n, m: (layer, 0, n)),
            pl.BlockSpec((None, D, tn), lambda n, m: (layer, 0, n)),
        ],
        out_specs=pl.BlockSpec((tm, tn), lambda n, m: (m, n)),
        out_shape=jax.ShapeDtypeStruct((T, D_FF), BF16),
        scratch_shapes=[pltpu.VMEM((D, tn), BF16), pltpu.VMEM((D, tn), BF16)],
        compiler_params=_params("arbitrary", "arbitrary"),
        name="mm_swiglu",
    )(a, wg, wu)


def _log_sigmoid(v):
    return jnp.minimum(v, 0.0) - jnp.log(1.0 + jnp.exp(-jnp.abs(v)))


def _gla_gate_kernel(h_ref, wz_ref, wf_ref, bf_ref, wb_ref, bb_ref, gf_ref, gb_ref):
    z = _dot(h_ref[...], wz_ref[...].astype(BF16))
    af = jnp.dot(z, wf_ref[...], precision=HIGHEST, preferred_element_type=F32) + bf_ref[...]
    ab = jnp.dot(z, wb_ref[...], precision=HIGHEST, preferred_element_type=F32) + bb_ref[...]
    gf = _log_sigmoid(af) / GLA_TAU
    gb = _log_sigmoid(ab) / GLA_TAU
    c = GLA_C
    ri = lax.broadcasted_iota(jnp.int32, (c, c), 0)
    ci = lax.broadcasted_iota(jnp.int32, (c, c), 1)
    lower = (ci <= ri).astype(F32)
    upper = (ci >= ri).astype(F32)
    for ch in range(h_ref.shape[0] // c):
        rows = slice(ch * c, (ch + 1) * c)
        gf_ref[rows, :] = jnp.dot(lower, gf[rows], precision=HIGHEST, preferred_element_type=F32)
        gb_ref[rows, :] = jnp.dot(upper, gb[rows], precision=HIGHEST, preferred_element_type=F32)


def gla_gates(h, wz_pad, wf_pad, bg_f, wb_pad, bg_b, j, tm):
    row = lambda i: (i, 0)
    return pl.pallas_call(
        _gla_gate_kernel,
        grid=(T // tm,),
        in_specs=[
            pl.BlockSpec((tm, D), row),
            pl.BlockSpec((D, LANES), lambda i: (0, 0)),
            pl.BlockSpec((LANES, GLA_QK), lambda i: (0, 0)),
            pl.BlockSpec((None, 1, GLA_QK), lambda i: (j, 0, 0)),
            pl.BlockSpec((LANES, GLA_QK), lambda i: (0, 0)),
            pl.BlockSpec((None, 1, GLA_QK), lambda i: (j, 0, 0)),
        ],
        out_specs=[pl.BlockSpec((tm, GLA_QK), row), pl.BlockSpec((tm, GLA_QK), row)],
        out_shape=[jax.ShapeDtypeStruct((T, GLA_QK), F32)] * 2,
        compiler_params=_params("arbitrary"),
        name="gla_gates",
    )(h, wz_pad, wf_pad, bg_f.reshape(-1, 1, GLA_QK), wb_pad, bg_b.reshape(-1, 1, GLA_QK))


def _gla_intra_direct(q, k, bc, rev):
    c = GLA_C
    lane8 = lax.broadcasted_iota(jnp.int32, (8, c), 1)
    srow8 = lax.broadcasted_iota(jnp.int32, (8, 1), 0)
    blocks = []
    for r0 in range(0, c, 8):
        q_r, b_r = q[r0:r0 + 8], bc[r0:r0 + 8]
        a_r = jnp.zeros((8, c), F32)
        for jj in (range(r0, c) if rev else range(r0 + 8)):
            t = q_r * k[jj:jj + 1] * jnp.exp(b_r - bc[jj:jj + 1])
            s = jnp.sum(t, axis=-1, keepdims=True)
            seen = (srow8 + r0 <= jj) if rev else (srow8 + r0 >= jj)
            a_r = jnp.where(lane8 == jj, jnp.where(seen, s, 0.0), a_r)
        blocks.append(a_r)
    return jnp.concatenate(blocks, axis=0).astype(BF16)


def _gla_chunks(streams, a_ref):
    c = GLA_C
    ri = lax.broadcasted_iota(jnp.int32, (c, c), 0)
    ci = lax.broadcasted_iota(jnp.int32, (c, c), 1)
    bcs = [b for (q, k, v, b, st, rev) in streams]

    partial = []
    for (q, k, v, g, st, rev), bc in zip(streams, bcs):
        b_end = bc[0:1] if rev else bc[c - 1:c]
        qe = (q * jnp.exp(bc)).astype(BF16)
        st_new = st * jnp.exp(b_end) + _dot_tn(v, (k * jnp.exp(b_end - bc)).astype(BF16))
        partial.append((st.astype(BF16), st_new, qe, b_end))

    for n, ((q, k, v, g, st, rev), bc, (_, _, qe, _)) in enumerate(zip(streams, bcs, partial)):
        a = _dot_nt(qe, (k * jnp.exp(jnp.minimum(-bc, GLA_SPLIT_DECAY_MAX))).astype(BF16))
        a_ref[n] = jnp.where(ci >= ri if rev else ci <= ri, a, 0.0).astype(BF16)

    total_decay = functools.reduce(jnp.maximum, [jnp.max(-b_end) for (_, _, _, b_end) in partial])

    @pl.when(total_decay > GLA_SPLIT_DECAY_MAX)
    def _():
        for n, ((q, k, v, g, st, rev), bc) in enumerate(zip(streams, bcs)):
            a_ref[n] = _gla_intra_direct(q, k, bc, rev)

    return [(_dot_nt(qe, st_b) + _dot(a_ref[n], v), st_new)
            for n, ((q, k, v, g, st, rev), (st_b, st_new, qe, _)) in enumerate(zip(streams, partial))]


def _gla_kernel(q_ref, k_ref, v_ref, r_ref, gf_ref, gb_ref, ng_ref, y_ref, stf_ref, stb_ref, of_ref, ob_ref,
                a_ref):
    nch = ROWS // GLA_C
    nctx = CTX // GLA_C

    def chunk_rows(ch):
        return pl.ds(pl.multiple_of(ch * GLA_C, GLA_C), GLA_C)

    def stream(rows, g_ref, st_ref, rev):
        return q_ref[rows, :], k_ref[rows, :], v_ref[rows, :], g_ref[rows, :], st_ref[...], rev

    stf_ref[...] = jnp.zeros_like(stf_ref)
    stb_ref[...] = jnp.zeros_like(stb_ref)

    def scan(s, carry):
        rows_f = chunk_rows(s)
        rows_b = chunk_rows(jnp.where(s < nctx, nctx - 1 - s, nch + nctx - 1 - s))
        (o_f, st_f), (o_b, st_b) = _gla_chunks([stream(rows_f, gf_ref, stf_ref, False),
                                                stream(rows_b, gb_ref, stb_ref, True)], a_ref)
        of_ref[rows_f, :] = o_f
        ob_ref[rows_b, :] = o_b
        stf_ref[...] = st_f
        stb_ref[...] = st_b
        return carry

    lax.fori_loop(0, nch, scan, 0)

    def readout(ch, carry):
        rows = chunk_rows(ch)
        o = of_ref[rows, :] + ob_ref[rows, :]
        on = o * lax.rsqrt(jnp.mean(o * o, axis=-1, keepdims=True) + EPS) * ng_ref[...]
        y_ref[rows, :] = (on * _silu(r_ref[rows, :].astype(F32))).astype(y_ref.dtype)
        return carry

    lax.fori_loop(0, nch, readout, 0)


def gla_scan(qk, vr, gf, gb, norm_g, j):
    return pl.pallas_call(
        _gla_kernel,
        grid=(BATCH, GLA_H),
        in_specs=[
            pl.BlockSpec((ROWS, GLA_DK), lambda b, h: (b, h)),
            pl.BlockSpec((ROWS, GLA_DK), lambda b, h: (b, GLA_H + h)),
            pl.BlockSpec((ROWS, GLA_DV), lambda b, h: (b, h)),
            pl.BlockSpec((ROWS, GLA_DV), lambda b, h: (b, GLA_H + h)),
            pl.BlockSpec((ROWS, GLA_DK), lambda b, h: (b, h)),
            pl.BlockSpec((ROWS, GLA_DK), lambda b, h: (b, h)),
            pl.BlockSpec((None, 1, GLA_DV), lambda b, h: (j, 0, 0)),
        ],
        out_specs=pl.BlockSpec((ROWS, GLA_DV), lambda b, h: (b, h)),
        out_shape=jax.ShapeDtypeStruct((T, D), BF16),
        scratch_shapes=[pltpu.VMEM((GLA_DV, GLA_DK), F32), pltpu.VMEM((GLA_DV, GLA_DK), F32),
                        pltpu.VMEM((ROWS, GLA_DV), F32), pltpu.VMEM((ROWS, GLA_DV), F32),
                        pltpu.VMEM((2, GLA_C, GLA_C), BF16)],
        compiler_params=_params("arbitrary", "arbitrary"),
        name="gla_scan",
    )(qk, qk, vr, vr, gf, gb, norm_g.reshape(-1, 1, GLA_DV))


def rope_tables():
    half = GLA_DK // 2
    freqs = ROPE_BASE ** (-np.arange(0, half, 2, dtype=np.float32) / half)
    t = np.arange(SEQ)
    pos = np.stack([t // GRID_W, t % GRID_W], axis=-1).astype(np.float32)
    ang = pos[:, :, None] * freqs
    cos, sin = np.cos(ang), np.sin(ang)
    cos_t = np.concatenate([cos, cos], axis=-1).reshape(SEQ, GLA_DK)
    sin_t = np.concatenate([-sin, sin], axis=-1).reshape(SEQ, GLA_DK)
    cos_t = np.concatenate([np.ones((CTX, GLA_DK), np.float32), cos_t], axis=0)
    sin_t = np.concatenate([np.zeros((CTX, GLA_DK), np.float32), sin_t], axis=0)
    return jnp.asarray(cos_t, F32), jnp.asarray(sin_t, F32)


def _nat_kernel(q_ref, k_ref, v_ref, bias_ref, y_ref, *, need_ctx):
    kc = k_ref[0:CTX, :]
    vc = v_ref[0:CTX, :]
    if need_ctx:
        s = _dot_nt(q_ref[0:CTX, :], kc)
        p = jnp.exp(s - jnp.max(s, axis=-1, keepdims=True))
        o = _dot(p.astype(BF16), vc) / jnp.sum(p, axis=-1, keepdims=True)
        y_ref[0:CTX, :] = o.astype(y_ref.dtype)
    else:
        y_ref[0:CTX, :] = jnp.zeros((CTX, NAT_DH), y_ref.dtype)
    n_keys = WIN_R * GRID_W

    def scores(r):
        rs = jnp.clip(r - WIN_R // 2, 0, GRID_H - WIN_R)
        q0 = pl.multiple_of(CTX + r * GRID_W, GRID_W)
        k0 = pl.multiple_of(CTX + rs * GRID_W, GRID_W)
        q = q_ref[pl.ds(q0, GRID_W), :]
        s_lat = _dot_nt(q, k_ref[pl.ds(k0, n_keys), :]) + bias_ref[r - rs]
        s_ctx = _dot_nt(q, kc)
        return q0, k0, s_lat, s_ctx

    def probs(q0, k0, s_lat, s_ctx):
        m = jnp.maximum(jnp.max(s_lat, axis=-1, keepdims=True), jnp.max(s_ctx, axis=-1, keepdims=True))
        p_lat = jnp.exp(s_lat - m)
        p_ctx = jnp.exp(s_ctx - m)
        denom = jnp.sum(p_lat, axis=-1, keepdims=True) + jnp.sum(p_ctx, axis=-1, keepdims=True)
        return q0, k0, p_lat.astype(BF16), p_ctx.astype(BF16), denom

    def values(q0, k0, p_lat, p_ctx, denom):
        o = _dot(p_lat, v_ref[pl.ds(k0, n_keys), :]) + _dot(p_ctx, vc)
        return q0, (o / denom).astype(y_ref.dtype)

    def body(it, carry):
        rows = [it * NAT_ROWS_PER_STEP + u for u in range(NAT_ROWS_PER_STEP)]
        outs = [values(*pr) for pr in [probs(*sc) for sc in [scores(r) for r in rows]]]
        for q0, o in outs:
            y_ref[pl.ds(q0, GRID_W), :] = o
        return carry

    lax.fori_loop(0, GRID_H // NAT_ROWS_PER_STEP, body, 0)


def nat_attention(qkv, bias_tbl, need_ctx):
    return pl.pallas_call(
        functools.partial(_nat_kernel, need_ctx=need_ctx),
        grid=(NAT_H, BATCH),
        in_specs=[
            pl.BlockSpec((ROWS, NAT_DH), lambda h, b: (b, h)),
            pl.BlockSpec((ROWS, NAT_DH), lambda h, b: (b, NAT_H + h)),
            pl.BlockSpec((ROWS, NAT_DH), lambda h, b: (b, 2 * NAT_H + h)),
            pl.BlockSpec((None, WIN_R, GRID_W, WIN_R * GRID_W), lambda h, b: (h, 0, 0, 0)),
        ],
        out_specs=pl.BlockSpec((ROWS, NAT_DH), lambda h, b: (b, h)),
        out_shape=jax.ShapeDtypeStruct((T, D), BF16),
        compiler_params=_params("arbitrary", "arbitrary"),
        name="nat_attention",
    )(qkv, qkv, qkv, bias_tbl)


def nat_bias_table(rpb):
    qc = np.arange(GRID_W)[:, None]
    kcol = np.arange(GRID_W)[None, :]
    start = np.clip(qc - WIN_C // 2, 0, GRID_W - WIN_C)
    in_win = (kcol >= start) & (kcol < start + WIN_C)
    dc = np.clip(kcol - qc + WIN_C - 1, 0, 2 * WIN_C - 2)
    pick = jnp.asarray(dc[None] == np.arange(2 * WIN_C - 1)[:, None, None], F32)
    by_col = jnp.einsum("hdc,cqk->hqdk", rpb, pick, precision=HIGHEST)
    by_col = jnp.where(in_win[None, :, None, :], by_col, -jnp.inf)
    tbl = jnp.stack([by_col[:, :, WIN_R - 1 - off:2 * WIN_R - 1 - off, :] for off in range(WIN_R)], axis=1)
    return tbl.reshape(NAT_H, WIN_R, GRID_W, WIN_R * GRID_W)


def _pair_slots(e0_ref, e1_ref, r0_ref, r1_ref, start_ref, t):
    return start_ref[e0_ref[t]] + r0_ref[t], start_ref[e1_ref[t]] + r1_ref[t]


def _is_routed_tile(i, latent_only):
    return (i % (ROWS // TG) >= CTX // TG) if latent_only else None


def _dispatch_kernel(e0_ref, e1_ref, r0_ref, r1_ref, start_ref, part_ref, h_ref, o_ref, zero_ref, sem, *,
                     latent_only):
    i = pl.program_id(0)
    base = i * TG

    @pl.when(i == 0)
    def _():
        zero_ref[...] = jnp.zeros_like(zero_ref)

        def zero_copy(p):
            return pltpu.make_async_copy(zero_ref, o_ref.at[pl.ds(pl.multiple_of(p * TM_PART, TM_PART), TM_PART)],
                                         sem.at[2])

        def start(p, carry):
            pl.when(part_ref[p] < TM_PART)(lambda: zero_copy(p).start())
            return carry

        def wait(p, carry):
            pl.when(part_ref[p] < TM_PART)(lambda: zero_copy(p).wait())
            return carry

        n_parts = o_ref.shape[0] // TM_PART
        lax.fori_loop(0, n_parts, start, 0)
        lax.fori_loop(0, n_parts, wait, 0)

    def start_rows(r, carry):
        row = h_ref.at[pl.ds(r, 1)]
        s0, s1 = _pair_slots(e0_ref, e1_ref, r0_ref, r1_ref, start_ref, base + r)
        pltpu.make_async_copy(row, o_ref.at[pl.ds(s0, 1)], sem.at[0]).start()
        pltpu.make_async_copy(row, o_ref.at[pl.ds(s1, 1)], sem.at[1]).start()
        return carry

    def run():
        lax.fori_loop(0, TG, start_rows, 0, unroll=DMA_UNROLL)
        for s in range(2):
            pltpu.make_async_copy(h_ref, o_ref.at[pl.ds(0, TG)], sem.at[s]).wait()

    routed = _is_routed_tile(i, latent_only)
    if routed is None:
        run()
    else:
        pl.when(routed)(run)


def moe_dispatch(routing, part_rows, h_words, n_slots, latent_only):
    words = h_words.shape[1]
    return pl.pallas_call(
        functools.partial(_dispatch_kernel, latent_only=latent_only),
        grid_spec=pltpu.PrefetchScalarGridSpec(
            num_scalar_prefetch=6,
            grid=(T // TG,),
            in_specs=[pl.BlockSpec((TG, words), lambda i, *_: (i, 0))],
            out_specs=pl.BlockSpec(memory_space=pl.ANY),
            scratch_shapes=[pltpu.VMEM((TM_PART, words), jnp.uint32), pltpu.SemaphoreType.DMA((3,))],
        ),
        out_shape=jax.ShapeDtypeStruct((n_slots, words), jnp.uint32),
        compiler_params=_params("arbitrary"),
        name="moe_dispatch",
    )(*routing, part_rows, h_words)


def _expert_changed(te_ref, i):
    return (i == 0) | (te_ref[i] != te_ref[jnp.maximum(i - 1, 0)])


def _on_occupied_prefix(n_rows, tile, o_ref, compute):
    for parts in range(tile // TM_PART + 1):
        p = parts * TM_PART

        @pl.when((n_rows > p - TM_PART) & (n_rows <= p))
        def _():
            if p > 0:
                o_ref[0:p, :] = compute(p)
            if p < tile:
                o_ref[p:tile, :] = jnp.zeros((tile - p, o_ref.shape[1]), o_ref.dtype)


def _moe_up_kernel(te_ref, tr_ref, ts_ref, to_ref, a_ref, wg_ref, wu_ref, o_ref, wgb_ref, wub_ref):
    del ts_ref, to_ref
    i = pl.program_id(1)

    @pl.when(_expert_changed(te_ref, i))
    def _():
        wgb_ref[...] = wg_ref[...].astype(BF16)
        wub_ref[...] = wu_ref[...].astype(BF16)

    half = D // 2

    def compute(p):
        lo, hi = _unpack_words(a_ref[0:p, :])
        gate = _dot(lo, wgb_ref[0:half, :]) + _dot(hi, wgb_ref[half:D, :])
        up = _dot(lo, wub_ref[0:half, :]) + _dot(hi, wub_ref[half:D, :])
        return (_silu(gate) * up).astype(o_ref.dtype)

    _on_occupied_prefix(tr_ref[i], TM_UP, o_ref, compute)


def moe_up(tiles, xw, wg, wu, j):
    n_slots = xw.shape[0]
    w_spec = pl.BlockSpec((None, None, D, TF_MOE), lambda f, i, te, tr, ts, to: (j, te[i], 0, f))
    return pl.pallas_call(
        _moe_up_kernel,
        grid_spec=pltpu.PrefetchScalarGridSpec(
            num_scalar_prefetch=4,
            grid=(D_FF // TF_MOE, n_slots // TM_UP),
            in_specs=[pl.BlockSpec((TM_UP, D // 2), lambda f, i, te, tr, ts, to: (ts[i], 0)), w_spec, w_spec],
            out_specs=pl.BlockSpec((TM_UP, TF_MOE), lambda f, i, te, tr, ts, to: (to[i], f)),
            scratch_shapes=[pltpu.VMEM((D, TF_MOE), BF16), pltpu.VMEM((D, TF_MOE), BF16)],
        ),
        out_shape=jax.ShapeDtypeStruct((n_slots, D_FF), BF16),
        compiler_params=_params("arbitrary", "arbitrary"),
        name="moe_up",
    )(*tiles, xw, wg, wu)


def _moe_down_kernel(te_ref, tr_ref, ts_ref, to_ref, a_ref, w_ref, o_ref, wb_ref):
    del ts_ref, to_ref
    i = pl.program_id(1)

    @pl.when(_expert_changed(te_ref, i))
    def _():
        wb_ref[...] = w_ref[...].astype(BF16)

    _on_occupied_prefix(tr_ref[i], TM_DOWN, o_ref, lambda p: _dot(a_ref[0:p, :], wb_ref[...]))


def moe_down(tiles, act, wd, j):
    n_slots = act.shape[0]
    tn = TN_MOE_DOWN
    return pl.pallas_call(
        _moe_down_kernel,
        grid_spec=pltpu.PrefetchScalarGridSpec(
            num_scalar_prefetch=4,
            grid=(D // tn, n_slots // TM_DOWN),
            in_specs=[
                pl.BlockSpec((TM_DOWN, D_FF), lambda n, i, te, tr, ts, to: (ts[i], 0)),
                pl.BlockSpec((None, None, D_FF, tn), lambda n, i, te, tr, ts, to: (j, te[i], 0, n)),
            ],
            out_specs=pl.BlockSpec((TM_DOWN, tn), lambda n, i, te, tr, ts, to: (to[i], n)),
            scratch_shapes=[pltpu.VMEM((D_FF, tn), BF16)],
        ),
        out_shape=jax.ShapeDtypeStruct((n_slots, D), F32),
        compiler_params=_params("arbitrary", "arbitrary"),
        name="moe_down",
    )(*tiles, act, wd)


def _combine_kernel(e0_ref, e1_ref, r0_ref, r1_ref, start_ref, ys_ref, x_ref, gt_ref, w_ref, o_ref,
                    b0_ref, b1_ref, sem, *, latent_only):
    i = pl.program_id(0)
    base = i * TG

    def start_rows(r, carry):
        s0, s1 = _pair_slots(e0_ref, e1_ref, r0_ref, r1_ref, start_ref, base + r)
        pltpu.make_async_copy(ys_ref.at[pl.ds(s0, 1)], b0_ref.at[pl.ds(r, 1)], sem.at[0]).start()
        pltpu.make_async_copy(ys_ref.at[pl.ds(s1, 1)], b1_ref.at[pl.ds(r, 1)], sem.at[1]).start()
        return carry

    def run():
        lax.fori_loop(0, TG, start_rows, 0, unroll=DMA_UNROLL)
        pltpu.make_async_copy(ys_ref.at[pl.ds(0, TG)], b0_ref, sem.at[0]).wait()
        pltpu.make_async_copy(ys_ref.at[pl.ds(0, TG)], b1_ref, sem.at[1]).wait()
        mix = w_ref[:, 0:1] * b0_ref[...] + w_ref[:, 1:2] * b1_ref[...]
        o_ref[...] = x_ref[...] + _row_gate(gt_ref, i, TG) * mix

    routed = _is_routed_tile(i, latent_only)
    if routed is None:
        run()
    else:
        pl.when(routed)(run)

        @pl.when(jnp.logical_not(routed))
        def _():
            o_ref[...] = x_ref[...]


def moe_combine(routing, ys, x, mods, layer, gate_col, wcol, latent_only):
    return pl.pallas_call(
        functools.partial(_combine_kernel, latent_only=latent_only),
        grid_spec=pltpu.PrefetchScalarGridSpec(
            num_scalar_prefetch=5,
            grid=(T // TG,),
            in_specs=[
                pl.BlockSpec(memory_space=pl.ANY),
                pl.BlockSpec((TG, D), lambda i, *_: (i, 0)),
                pl.BlockSpec((None, N_MOD, D), lambda i, *_: (layer, 0, gate_col)),
                pl.BlockSpec((TG, LANES), lambda i, *_: (i, 0)),
            ],
            out_specs=pl.BlockSpec((TG, D), lambda i, *_: (i, 0)),
            scratch_shapes=[pltpu.VMEM((TG, D), F32), pltpu.VMEM((TG, D), F32), pltpu.SemaphoreType.DMA((2,))],
        ),
        out_shape=jax.ShapeDtypeStruct((T, D), F32),
        compiler_params=_params("arbitrary"),
        name="moe_combine",
    )(*routing, ys, x, mods, wcol)


def moe_slots(n_tokens):
    return (2 * n_tokens // TM_UP + N_EXP) * TM_UP


def moe_plan(counts, n_slots):
    group = (counts + TM_UP - 1) // TM_UP * TM_UP
    start = jnp.cumsum(group) - group
    end = start + counts
    last_row = jnp.max(jnp.where(counts > 0, end, 0)) - 1

    def of_expert(table, expert):
        return jnp.sum(jnp.where(expert[:, None] == jnp.arange(N_EXP), table, 0), axis=1)

    def occupied_rows(tm):
        ids = jnp.arange(n_slots // tm)
        expert = jnp.sum((start + group)[None, :] <= jnp.minimum(ids * tm, last_row)[:, None], axis=1)
        return ids, expert, jnp.clip(of_expert(end, expert) - ids * tm, 0, tm)

    def tiles(tm):
        ids, expert, rows = occupied_rows(tm)
        first = of_expert(start, expert) // tm
        n_occ = (of_expert(counts, expert) + tm - 1) // tm
        k = ids - first
        tile = jnp.where(k < n_occ, first + (k - 1) % jnp.maximum(n_occ, 1), ids)
        rows_v = jnp.clip(of_expert(end, expert) - tile * tm, 0, tm)
        fetched = jnp.where(rows_v > 0, ids, 0)
        last_fetch = jnp.max(jnp.where(ids[None, :] <= ids[:, None], fetched[None, :], 0), axis=1)
        block = jnp.sum(jnp.where(ids[None, :] == last_fetch[:, None], tile[None, :], 0), axis=1)
        return tuple(a.astype(jnp.int32) for a in (expert, rows_v, block, tile))

    return start.astype(jnp.int32), tiles(TM_UP), tiles(TM_DOWN), occupied_rows(TM_PART)[2].astype(jnp.int32)


def _final_norm_kernel(x_ref, g_ref, o_ref):
    x = x_ref[...]
    o_ref[...] = x * lax.rsqrt(jnp.mean(x * x, axis=-1, keepdims=True) + EPS) * g_ref[...]


def final_norm(x, g):
    per_batch = SEQ // TM_MOD
    tiles = ROWS // TM_MOD
    first = CTX // TM_MOD
    out = pl.pallas_call(
        _final_norm_kernel,
        grid=(BATCH * per_batch,),
        in_specs=[
            pl.BlockSpec((TM_MOD, D), lambda i: ((i // per_batch) * tiles + first + i % per_batch, 0)),
            pl.BlockSpec((1, D), lambda i: (0, 0)),
        ],
        out_specs=pl.BlockSpec((TM_MOD, D), lambda i: (i, 0)),
        out_shape=jax.ShapeDtypeStruct((BATCH * SEQ, D), F32),
        compiler_params=_params("arbitrary"),
        name="final_norm",
    )(x, g.reshape(1, D))
    return out.reshape(BATCH, SEQ, D)


def _pad_cols(w, n):
    return jnp.pad(w, ((0, 0), (0, n - w.shape[1])))


def kernel(x, c, ctx, c_ctx, ada_w, ada_b, norm_mix_g, norm_ffn_g, gla_w_in, gla_wg_fwd, gla_bg_fwd, gla_wg_bwd, gla_bg_bwd, gla_norm_g, gla_w_out, nat_w_in, nat_rpb, nat_w_out, ffn_w_gate, ffn_w_up, ffn_w_down, moe_router, moe_w_gate, moe_w_up, moe_w_down, final_norm_g):
    xs = jnp.concatenate([ctx, x], axis=1).reshape(T, D)
    c8 = jnp.concatenate([c, c_ctx[None, :], jnp.zeros((N_MOD - BATCH - 1, D), F32)], axis=0)
    mods = ada_tables(c8, ada_w, ada_b)
    cos_t, sin_t = rope_tables()
    router_pad = jnp.pad(moe_router, ((0, 0), (0, 0), (0, LANES - N_EXP)))

    for i in range(DEPTH):
        j = i // 2
        last = i == DEPTH - 1
        h = modulate(xs, norm_mix_g, mods, i, 0, 1)
        if i % 2 == 0:
            qk = mm_rope(h, gla_w_in, j, cos_t, sin_t, tm=1152)
            vr = mm_plain(h, gla_w_in, j, 2 * GLA_QK, 2 * D, BF16, tm=1152, tn=1024)
            z0 = 2 * GLA_QK + 2 * D
            wz_pad = _pad_cols(gla_w_in[j, :, z0:z0 + 2 * GLA_RANK], LANES)
            wf_pad = jnp.pad(gla_wg_fwd[j], ((0, LANES - GLA_RANK), (0, 0)))
            wb_pad = jnp.pad(gla_wg_bwd[j], ((GLA_RANK, LANES - 2 * GLA_RANK), (0, 0)))
            gf, gb = gla_gates(h, wz_pad, wf_pad, gla_bg_fwd, wb_pad, gla_bg_bwd, j, tm=1152)
            y = gla_scan(qk, vr, gf, gb, gla_norm_g, j)
            xs = mm_resid(y, gla_w_out, j, xs, mods, i, 2, tm=1152, tn=1024)
        else:
            qkv = mm_plain(h, nat_w_in, j, 0, 3 * D, BF16, tm=1152, tn=1024,
                           scale=NAT_DH ** -0.5, n_scaled=D // 1024)
            y = nat_attention(qkv, nat_bias_table(nat_rpb[j]), not last)
            xs = mm_resid(y, nat_w_out, j, xs, mods, i, 2, tm=1152, tn=1024)
        if i % 2 == 0:
            h2 = modulate(xs, norm_ffn_g, mods, i, 3, 4)
            act = mm_swiglu(h2, ffn_w_gate, ffn_w_up, j, tm=1152, tn=512)
            xs = mm_resid(act, ffn_w_down, j, xs, mods, i, 5, tm=576, tn=512)
        else:
            h_words, idx, rank, wcol, counts = modulate_route(xs, norm_ffn_g, mods, i, 3, 4, router_pad, j, last)
            n_slots = moe_slots(BATCH * SEQ if last else T)
            start, up_tiles, down_tiles, part_rows = moe_plan(counts[:, 0], n_slots)
            routing = (idx[0], idx[1], rank[0], rank[1], start)
            xw = moe_dispatch(routing, part_rows, h_words, n_slots, last)
            act = moe_up(up_tiles, xw, moe_w_gate, moe_w_up, j)
            ys = moe_down(down_tiles, act, moe_w_down, j)
            xs = moe_combine(routing, ys, xs, mods, i, 5, wcol, last)
    return final_norm(xs, final_norm_g)
```

```python
import functools

import numpy as np
import jax
import jax.numpy as jnp
from jax import lax
from jax.experimental import pallas as pl
from jax.experimental.pallas import tpu as pltpu

F32 = jnp.float32
BF16 = jnp.bfloat16
HIGHEST = lax.Precision.HIGHEST

D = 2048
BATCH = 4
SEQ = 2048
CTX = 256
ROWS = CTX + SEQ
T = BATCH * ROWS
DEPTH = 4
GRID_W = 64
GRID_H = SEQ // GRID_W
EPS = 1e-6
ROPE_BASE = 10000.0
CTX_GROUP = BATCH
N_MOD = 8

GLA_H = 4
GLA_DK = 256
GLA_DV = 512
GLA_QK = GLA_H * GLA_DK
GLA_RANK = 16
GLA_TAU = 16.0
GLA_C = 64
GLA_SPLIT_DECAY_MAX = 60.0

NAT_H = 16
NAT_DH = 128
WIN_R = 8
WIN_C = 16
NAT_ROWS_PER_STEP = 8

D_FF = 5632
N_EXP = 8

LANES = 128
VMEM_LIMIT = 56 * 1024 * 1024

TM_MOD = 256
TM_MODULATE = 768
TM_UP = 1024
TM_DOWN = 512
TM_PART = 256
TF_MOE = 512
TN_MOE_DOWN = 512
TG = 256
DMA_UNROLL = 8


def _params(*sem):
    return pltpu.CompilerParams(dimension_semantics=sem, vmem_limit_bytes=VMEM_LIMIT)


def _dot(a, b):
    return jnp.dot(a, b, preferred_element_type=F32)


def _dot_nt(a, b):
    return lax.dot_general(a, b, (((1,), (1,)), ((), ())), preferred_element_type=F32)


def _dot_tn(a, b):
    return lax.dot_general(a, b, (((0,), (0,)), ((), ())), preferred_element_type=F32)


def _silu(v):
    return v * jax.nn.sigmoid(v)


def _row_gate(tab_ref, m, tm):
    start = m * tm
    b = start // ROWS
    rows = start % ROWS + lax.broadcasted_iota(jnp.int32, (tm, 1), 0)
    per_batch = tab_ref[pl.ds(b, 1), :]
    per_ctx = tab_ref[CTX_GROUP:CTX_GROUP + 1, :]
    return jnp.where(rows < CTX, per_ctx, per_batch)


def _ada_kernel(c_ref, w_ref, b_ref, o_ref):
    a = _silu(c_ref[...]).astype(BF16)
    o_ref[...] = _dot(a, w_ref[...].astype(BF16)) + b_ref[...]


def ada_tables(c8, ada_w, ada_b):
    tn = 1024
    return pl.pallas_call(
        _ada_kernel,
        grid=(DEPTH, 6 * D // tn),
        in_specs=[
            pl.BlockSpec((N_MOD, D), lambda l, n: (0, 0)),
            pl.BlockSpec((None, D, tn), lambda l, n: (l, 0, n)),
            pl.BlockSpec((None, 1, tn), lambda l, n: (l, 0, n)),
        ],
        out_specs=pl.BlockSpec((None, N_MOD, tn), lambda l, n: (l, 0, n)),
        out_shape=jax.ShapeDtypeStruct((DEPTH, N_MOD, 6 * D), F32),
        compiler_params=_params("arbitrary", "arbitrary"),
        name="ada_tables",
    )(c8, ada_w, ada_b.reshape(DEPTH, 1, 6 * D))


def _modulate_rows(x, g_ref, sh_ref, sc_ref):
    tm = x.shape[0]
    y = x * lax.rsqrt(jnp.mean(x * x, axis=-1, keepdims=True) + EPS) * g_ref[...]
    i = pl.program_id(0)
    return y * (1.0 + _row_gate(sc_ref, i, tm)) + _row_gate(sh_ref, i, tm)


def _modulated(x_ref, g_ref, sh_ref, sc_ref):
    return _modulate_rows(x_ref[...], g_ref, sh_ref, sc_ref)


def _embed_modulate_kernel(x_ref, c_ref, g_ref, sh_ref, sc_ref, xs_ref, h_ref):
    is_ctx = pl.program_id(0) % (ROWS // TM_MOD) < CTX // TM_MOD
    rows = jnp.where(is_ctx, c_ref[...], x_ref[...])
    xs_ref[...] = rows
    h_ref[...] = _modulate_rows(rows, g_ref, sh_ref, sc_ref).astype(h_ref.dtype)


def embed_modulate(x, ctx, norm_g, mods, layer, sh_col, sc_col):
    per_batch = ROWS // TM_MOD
    n_ctx = CTX // TM_MOD
    n_lat = SEQ // TM_MOD
    specs = _mod_specs(layer, sh_col, sc_col, TM_MOD)
    return pl.pallas_call(
        _embed_modulate_kernel,
        grid=(T // TM_MOD,),
        in_specs=[
            pl.BlockSpec((TM_MOD, D), lambda i: ((i // per_batch) * n_lat + jnp.maximum(i % per_batch - n_ctx, 0), 0)),
            pl.BlockSpec((TM_MOD, D), lambda i: ((i // per_batch) * n_ctx + jnp.minimum(i % per_batch, n_ctx - 1), 0)),
        ] + specs[1:],
        out_specs=[pl.BlockSpec((TM_MOD, D), lambda i: (i, 0)), pl.BlockSpec((TM_MOD, D), lambda i: (i, 0))],
        out_shape=[jax.ShapeDtypeStruct((T, D), F32), jax.ShapeDtypeStruct((T, D), BF16)],
        compiler_params=_params("arbitrary"),
        name="embed_modulate",
    )(x.reshape(BATCH * SEQ, D), ctx.reshape(BATCH * CTX, D), norm_g.reshape(DEPTH, 1, D), mods, mods)


def _modulate_kernel(x_ref, g_ref, sh_ref, sc_ref, o_ref):
    o_ref[...] = _modulated(x_ref, g_ref, sh_ref, sc_ref).astype(o_ref.dtype)


def _mod_specs(layer, sh_col, sc_col, tm):
    return [
        pl.BlockSpec((tm, D), lambda i: (i, 0)),
        pl.BlockSpec((None, 1, D), lambda i: (layer, 0, 0)),
        pl.BlockSpec((None, N_MOD, D), lambda i: (layer, 0, sh_col)),
        pl.BlockSpec((None, N_MOD, D), lambda i: (layer, 0, sc_col)),
    ]


def modulate(x, norm_g, mods, layer, sh_col, sc_col):
    tm = TM_MODULATE
    return pl.pallas_call(
        _modulate_kernel,
        grid=(T // tm,),
        in_specs=_mod_specs(layer, sh_col, sc_col, tm),
        out_specs=pl.BlockSpec((tm, D), lambda i: (i, 0)),
        out_shape=jax.ShapeDtypeStruct((T, D), BF16),
        compiler_params=_params("arbitrary"),
        name="modulate",
    )(x, norm_g.reshape(DEPTH, 1, D), mods, mods)


def _pack_words(h):
    half = h.shape[1] // 2
    lo = lax.bitcast_convert_type(h[:, :half].astype(BF16).astype(F32), jnp.uint32)
    hi = lax.bitcast_convert_type(h[:, half:].astype(BF16).astype(F32), jnp.uint32)
    return (lo >> 16) | (hi & jnp.uint32(0xFFFF0000))


def _unpack_words(w):
    lo = lax.bitcast_convert_type(w << 16, F32).astype(BF16)
    hi = lax.bitcast_convert_type(w & jnp.uint32(0xFFFF0000), F32).astype(BF16)
    return lo, hi


def _modulate_route_kernel(x_ref, g_ref, sh_ref, sc_ref, rt_ref, hw_ref, idx_ref, rank_ref, wcol_ref, cnt_ref,
                           base_ref, *, latent_only):
    i = pl.program_id(0)

    @pl.when(i == 0)
    def _():
        base_ref[...] = jnp.zeros_like(base_ref)

    h = _modulated(x_ref, g_ref, sh_ref, sc_ref)
    hw_ref[...] = _pack_words(h)
    logits = jnp.dot(h, rt_ref[...], precision=HIGHEST, preferred_element_type=F32)

    lane = lax.broadcasted_iota(jnp.int32, logits.shape, 1)
    lm = jnp.where(lane < N_EXP, logits, -jnp.inf)
    c1 = jnp.max(lm, axis=1, keepdims=True)
    j1 = jnp.min(jnp.where(lm == c1, lane, LANES), axis=1, keepdims=True)
    c2 = jnp.max(jnp.where(lane == j1, -jnp.inf, lm), axis=1, keepdims=True)
    e = jnp.exp(c2 - c1)
    wcol_ref[...] = jnp.where(lane == 0, 1.0 / (1.0 + e), jnp.where(lane == 1, e / (1.0 + e), 0.0))

    lt = logits.T[:N_EXP]
    eid = lax.broadcasted_iota(jnp.int32, lt.shape, 0)
    m1 = jnp.max(lt, axis=0, keepdims=True)
    i1 = jnp.min(jnp.where(lt == m1, eid, N_EXP), axis=0, keepdims=True)
    lt2 = jnp.where(eid == i1, -jnp.inf, lt)
    m2 = jnp.max(lt2, axis=0, keepdims=True)
    i2 = jnp.min(jnp.where(lt2 == m2, eid, N_EXP), axis=0, keepdims=True)
    idx_ref[...] = jnp.concatenate([i1, i2], axis=0)

    routed = 1.0
    if latent_only:
        routed = jnp.where(i % (ROWS // TM_MOD) < CTX // TM_MOD, 0.0, 1.0)
    oh1 = jnp.where(eid == i1, routed, 0.0)
    oh2 = jnp.where(eid == i2, routed, 0.0)
    both = oh1 + oh2
    tj = lax.broadcasted_iota(jnp.int32, (TM_MOD, TM_MOD), 0)
    tt = lax.broadcasted_iota(jnp.int32, (TM_MOD, TM_MOD), 1)
    before = _dot(both.astype(BF16), jnp.where(tj < tt, 1.0, 0.0).astype(BF16))
    seen = base_ref[:, 0:1] + before
    r1 = jnp.sum(oh1 * seen, axis=0, keepdims=True)
    r2 = jnp.sum(oh2 * seen, axis=0, keepdims=True)
    rank_ref[...] = jnp.concatenate([r1, r2], axis=0).astype(jnp.int32)
    total = base_ref[...] + jnp.sum(both, axis=1, keepdims=True)
    base_ref[...] = total
    cnt_ref[...] = total.astype(jnp.int32)


def modulate_route(x, norm_g, mods, layer, sh_col, sc_col, router_pad, j, latent_only):
    pairs = pl.BlockSpec((2, TM_MOD), lambda i: (0, i))
    return pl.pallas_call(
        functools.partial(_modulate_route_kernel, latent_only=latent_only),
        grid=(T // TM_MOD,),
        in_specs=_mod_specs(layer, sh_col, sc_col, TM_MOD) + [
            pl.BlockSpec((None, D, LANES), lambda i: (j, 0, 0)),
        ],
        out_specs=[
            pl.BlockSpec((TM_MOD, D // 2), lambda i: (i, 0)),
            pairs, pairs,
            pl.BlockSpec((TM_MOD, LANES), lambda i: (i, 0)),
            pl.BlockSpec((N_EXP, LANES), lambda i: (0, 0)),
        ],
        out_shape=[
            jax.ShapeDtypeStruct((T, D // 2), jnp.uint32),
            jax.ShapeDtypeStruct((2, T), jnp.int32),
            jax.ShapeDtypeStruct((2, T), jnp.int32),
            jax.ShapeDtypeStruct((T, LANES), F32),
            jax.ShapeDtypeStruct((N_EXP, LANES), jnp.int32),
        ],
        scratch_shapes=[pltpu.VMEM((N_EXP, LANES), F32)],
        compiler_params=_params("arbitrary"),
        name="modulate_route",
    )(x, norm_g.reshape(DEPTH, 1, D), mods, mods, router_pad)


def _cast_weight(w_ref, wb_ref):
    @pl.when(pl.program_id(1) == 0)
    def _():
        wb_ref[...] = w_ref[...].astype(BF16)


def _mm_plain_kernel(a_ref, w_ref, o_ref, wb_ref, *, scale, n_scaled):
    _cast_weight(w_ref, wb_ref)
    acc = _dot(a_ref[...], wb_ref[...])
    if n_scaled:
        acc = acc * jnp.where(pl.program_id(0) < n_scaled, scale, 1.0)
    o_ref[...] = acc.astype(o_ref.dtype)


def mm_plain(a, w, layer, col0, n_out, out_dtype, tm, tn, scale=1.0, n_scaled=0):
    k = a.shape[1]
    c0 = col0 // tn
    return pl.pallas_call(
        functools.partial(_mm_plain_kernel, scale=scale, n_scaled=n_scaled),
        grid=(n_out // tn, T // tm),
        in_specs=[
            pl.BlockSpec((tm, k), lambda n, m: (m, 0)),
            pl.BlockSpec((None, k, tn), lambda n, m: (layer, 0, n + c0)),
        ],
        out_specs=pl.BlockSpec((tm, tn), lambda n, m: (m, n)),
        out_shape=jax.ShapeDtypeStruct((T, n_out), out_dtype),
        scratch_shapes=[pltpu.VMEM((k, tn), BF16)],
        compiler_params=_params("arbitrary", "arbitrary"),
        name="mm_plain",
    )(a, w)


def _mm_rope_kernel(a_ref, w_ref, cos_ref, sin_ref, o_ref, wb_ref, *, tn):
    _cast_weight(w_ref, wb_ref)
    acc = _dot(a_ref[...], wb_ref[...])
    scale = jnp.where(pl.program_id(0) == 0, GLA_DK ** -0.5, 1.0)
    for s in range(tn // LANES):
        xs = acc[:, s * LANES:(s + 1) * LANES]
        t = (s % 2) * LANES
        rot = xs * cos_ref[:, t:t + LANES] + pltpu.roll(xs, LANES // 2, 1) * sin_ref[:, t:t + LANES]
        o_ref[:, s * LANES:(s + 1) * LANES] = rot * scale


def mm_rope(a, w, layer, cos_t, sin_t, tm):
    tn = GLA_QK
    per_batch = ROWS // tm
    return pl.pallas_call(
        functools.partial(_mm_rope_kernel, tn=tn),
        grid=(2, T // tm),
        in_specs=[
            pl.BlockSpec((tm, D), lambda n, m: (m, 0)),
            pl.BlockSpec((None, D, tn), lambda n, m: (layer, 0, n)),
            pl.BlockSpec((tm, GLA_DK), lambda n, m: (m % per_batch, 0)),
            pl.BlockSpec((tm, GLA_DK), lambda n, m: (m % per_batch, 0)),
        ],
        out_specs=pl.BlockSpec((tm, tn), lambda n, m: (m, n)),
        out_shape=jax.ShapeDtypeStruct((T, 2 * GLA_QK), F32),
        scratch_shapes=[pltpu.VMEM((D, tn), BF16)],
        compiler_params=_params("arbitrary", "arbitrary"),
        name="mm_rope",
    )(a, w, cos_t, sin_t)


def _mm_resid_kernel(a_ref, w_ref, x_ref, gt_ref, o_ref, wb_ref, *, tm):
    _cast_weight(w_ref, wb_ref)
    acc = _dot(a_ref[...], wb_ref[...])
    o_ref[...] = x_ref[...] + _row_gate(gt_ref, pl.program_id(1), tm) * acc


def mm_resid(a, w, layer_w, x, mods, layer, gate_col, tm, tn):
    k = a.shape[1]
    g0 = gate_col * D // tn
    return pl.pallas_call(
        functools.partial(_mm_resid_kernel, tm=tm),
        grid=(D // tn, T // tm),
        in_specs=[
            pl.BlockSpec((tm, k), lambda n, m: (m, 0)),
            pl.BlockSpec((None, k, tn), lambda n, m: (layer_w, 0, n)),
            pl.BlockSpec((tm, tn), lambda n, m: (m, n)),
            pl.BlockSpec((None, N_MOD, tn), lambda n, m: (layer, 0, g0 + n)),
        ],
        out_specs=pl.BlockSpec((tm, tn), lambda n, m: (m, n)),
        out_shape=jax.ShapeDtypeStruct((T, D), F32),
        scratch_shapes=[pltpu.VMEM((k, tn), BF16)],
        compiler_params=_params("arbitrary", "arbitrary"),
        name="mm_resid",
    )(a, w, x, mods)


def _mm_swiglu_kernel(a_ref, wg_ref, wu_ref, o_ref, wgb_ref, wub_ref):
    _cast_weight(wg_ref, wgb_ref)
    _cast_weight(wu_ref, wub_ref)
    a = a_ref[...]
    o_ref[...] = (_silu(_dot(a, wgb_ref[...])) * _dot(a, wub_ref[...])).astype(o_ref.dtype)


def mm_swiglu(a, wg, wu, layer, tm, tn):
    return pl.pallas_call(
        _mm_swiglu_kernel,
        grid=(D_FF // tn, T // tm),
        in_specs=[
            pl.BlockSpec((tm, D), lambda n, m: (m, 0)),
            pl.BlockSpec((None, D, tn), lambda n, m: (layer, 0, n)),
            pl.BlockSpec((None, D, tn), lambda n, m: (layer, 0, n)),
        ],
        out_specs=pl.BlockSpec((tm, tn), lambda n, m: (m, n)),
        out_shape=jax.ShapeDtypeStruct((T, D_FF), BF16),
        scratch_shapes=[pltpu.VMEM((D, tn), BF16), pltpu.VMEM((D, tn), BF16)],
        compiler_params=_params("arbitrary", "arbitrary"),
        name="mm_swiglu",
    )(a, wg, wu)


def _log_sigmoid(v):
    return jnp.minimum(v, 0.0) - jnp.log(1.0 + jnp.exp(-jnp.abs(v)))


def _gla_gate_kernel(h_ref, wz_ref, wf_ref, bf_ref, wb_ref, bb_ref, gf_ref, gb_ref):
    z = _dot(h_ref[...], wz_ref[...].astype(BF16))
    af = jnp.dot(z, wf_ref[...], precision=HIGHEST, preferred_element_type=F32) + bf_ref[...]
    ab = jnp.dot(z, wb_ref[...], precision=HIGHEST, preferred_element_type=F32) + bb_ref[...]
    gf = _log_sigmoid(af) / GLA_TAU
    gb = _log_sigmoid(ab) / GLA_TAU
    c = GLA_C
    ri = lax.broadcasted_iota(jnp.int32, (c, c), 0)
    ci = lax.broadcasted_iota(jnp.int32, (c, c), 1)
    lower = (ci <= ri).astype(BF16)
    upper = (ci >= ri).astype(BF16)

    def terms(v):
        hi = v.astype(BF16)
        rest = v - hi.astype(F32)
        mid = rest.astype(BF16)
        return hi, mid, (rest - mid.astype(F32)).astype(BF16)

    gf3, gb3 = terms(gf), terms(gb)
    for ch in range(h_ref.shape[0] // c):
        rows = slice(ch * c, (ch + 1) * c)
        gf_ref[rows, :] = _dot(lower, gf3[0][rows]) + _dot(lower, gf3[1][rows]) + _dot(lower, gf3[2][rows])
        gb_ref[rows, :] = _dot(upper, gb3[0][rows]) + _dot(upper, gb3[1][rows]) + _dot(upper, gb3[2][rows])


def gla_gates(h, wz_pad, wf_pad, bg_f, wb_pad, bg_b, j, tm):
    row = lambda i: (i, 0)
    return pl.pallas_call(
        _gla_gate_kernel,
        grid=(T // tm,),
        in_specs=[
            pl.BlockSpec((tm, D), row),
            pl.BlockSpec((D, LANES), lambda i: (0, 0)),
            pl.BlockSpec((LANES, GLA_QK), lambda i: (0, 0)),
            pl.BlockSpec((None, 1, GLA_QK), lambda i: (j, 0, 0)),
            pl.BlockSpec((LANES, GLA_QK), lambda i: (0, 0)),
            pl.BlockSpec((None, 1, GLA_QK), lambda i: (j, 0, 0)),
        ],
        out_specs=[pl.BlockSpec((tm, GLA_QK), row), pl.BlockSpec((tm, GLA_QK), row)],
        out_shape=[jax.ShapeDtypeStruct((T, GLA_QK), F32)] * 2,
        compiler_params=_params("arbitrary"),
        name="gla_gates",
    )(h, wz_pad, wf_pad, bg_f.reshape(-1, 1, GLA_QK), wb_pad, bg_b.reshape(-1, 1, GLA_QK))


def _gla_intra_direct(q, k, bc, rev):
    c = GLA_C
    lane8 = lax.broadcasted_iota(jnp.int32, (8, c), 1)
    srow8 = lax.broadcasted_iota(jnp.int32, (8, 1), 0)
    blocks = []
    for r0 in range(0, c, 8):
        q_r, b_r = q[r0:r0 + 8], bc[r0:r0 + 8]
        a_r = jnp.zeros((8, c), F32)
        for jj in (range(r0, c) if rev else range(r0 + 8)):
            t = q_r * k[jj:jj + 1] * jnp.exp(b_r - bc[jj:jj + 1])
            s = jnp.sum(t, axis=-1, keepdims=True)
            seen = (srow8 + r0 <= jj) if rev else (srow8 + r0 >= jj)
            a_r = jnp.where(lane8 == jj, jnp.where(seen, s, 0.0), a_r)
        blocks.append(a_r)
    return jnp.concatenate(blocks, axis=0).astype(BF16)


def _gla_chunks(streams, a_ref):
    c = GLA_C
    ri = lax.broadcasted_iota(jnp.int32, (c, c), 0)
    ci = lax.broadcasted_iota(jnp.int32, (c, c), 1)
    bcs = [b for (q, k, v, b, st, rev) in streams]

    partial = []
    for (q, k, v, g, st, rev), bc in zip(streams, bcs):
        b_end = bc[0:1] if rev else bc[c - 1:c]
        qe = (q * jnp.exp(bc)).astype(BF16)
        st_new = st * jnp.exp(b_end) + _dot_tn(v, (k * jnp.exp(b_end - bc)).astype(BF16))
        partial.append((st.astype(BF16), st_new, qe, b_end))

    for n, ((q, k, v, g, st, rev), bc, (_, _, qe, _)) in enumerate(zip(streams, bcs, partial)):
        a = _dot_nt(qe, (k * jnp.exp(jnp.minimum(-bc, GLA_SPLIT_DECAY_MAX))).astype(BF16))
        a_ref[n] = jnp.where(ci >= ri if rev else ci <= ri, a, 0.0).astype(BF16)

    total_decay = functools.reduce(jnp.maximum, [jnp.max(-b_end) for (_, _, _, b_end) in partial])

    @pl.when(total_decay > GLA_SPLIT_DECAY_MAX)
    def _():
        for n, ((q, k, v, g, st, rev), bc) in enumerate(zip(streams, bcs)):
            a_ref[n] = _gla_intra_direct(q, k, bc, rev)

    return [(_dot_nt(qe, st_b) + _dot(a_ref[n], v), st_new)
            for n, ((q, k, v, g, st, rev), (st_b, st_new, qe, _)) in enumerate(zip(streams, partial))]


def _gla_kernel(q_ref, k_ref, v_ref, r_ref, gf_ref, gb_ref, ng_ref, y_ref, stf_ref, stb_ref, of_ref, ob_ref,
                a_ref):
    nch = ROWS // GLA_C
    nctx = CTX // GLA_C

    def chunk_rows(ch):
        return pl.ds(pl.multiple_of(ch * GLA_C, GLA_C), GLA_C)

    def stream(rows, g_ref, st_ref, rev):
        return q_ref[rows, :], k_ref[rows, :], v_ref[rows, :], g_ref[rows, :], st_ref[...], rev

    stf_ref[...] = jnp.zeros_like(stf_ref)
    stb_ref[...] = jnp.zeros_like(stb_ref)

    def scan(s, carry):
        rows_f = chunk_rows(s)
        rows_b = chunk_rows(jnp.where(s < nctx, nctx - 1 - s, nch + nctx - 1 - s))
        (o_f, st_f), (o_b, st_b) = _gla_chunks([stream(rows_f, gf_ref, stf_ref, False),
                                                stream(rows_b, gb_ref, stb_ref, True)], a_ref)
        of_ref[rows_f, :] = o_f
        ob_ref[rows_b, :] = o_b
        stf_ref[...] = st_f
        stb_ref[...] = st_b
        return carry

    lax.fori_loop(0, nch, scan, 0)

    def readout(ch, carry):
        rows = chunk_rows(ch)
        o = of_ref[rows, :] + ob_ref[rows, :]
        on = o * lax.rsqrt(jnp.mean(o * o, axis=-1, keepdims=True) + EPS) * ng_ref[...]
        y_ref[rows, :] = (on * _silu(r_ref[rows, :].astype(F32))).astype(y_ref.dtype)
        return carry

    lax.fori_loop(0, nch, readout, 0)


def gla_scan(qk, vr, gf, gb, norm_g, j):
    return pl.pallas_call(
        _gla_kernel,
        grid=(BATCH, GLA_H),
        in_specs=[
            pl.BlockSpec((ROWS, GLA_DK), lambda b, h: (b, h)),
            pl.BlockSpec((ROWS, GLA_DK), lambda b, h: (b, GLA_H + h)),
            pl.BlockSpec((ROWS, GLA_DV), lambda b, h: (b, h)),
            pl.BlockSpec((ROWS, GLA_DV), lambda b, h: (b, GLA_H + h)),
            pl.BlockSpec((ROWS, GLA_DK), lambda b, h: (b, h)),
            pl.BlockSpec((ROWS, GLA_DK), lambda b, h: (b, h)),
            pl.BlockSpec((None, 1, GLA_DV), lambda b, h: (j, 0, 0)),
        ],
        out_specs=pl.BlockSpec((ROWS, GLA_DV), lambda b, h: (b, h)),
        out_shape=jax.ShapeDtypeStruct((T, D), BF16),
        scratch_shapes=[pltpu.VMEM((GLA_DV, GLA_DK), F32), pltpu.VMEM((GLA_DV, GLA_DK), F32),
                        pltpu.VMEM((ROWS, GLA_DV), F32), pltpu.VMEM((ROWS, GLA_DV), F32),
                        pltpu.VMEM((2, GLA_C, GLA_C), BF16)],
        compiler_params=_params("arbitrary", "arbitrary"),
        name="gla_scan",
    )(qk, qk, vr, vr, gf, gb, norm_g.reshape(-1, 1, GLA_DV))


def rope_tables():
    half = GLA_DK // 2
    freqs = ROPE_BASE ** (-np.arange(0, half, 2, dtype=np.float32) / half)
    t = np.arange(SEQ)
    pos = np.stack([t // GRID_W, t % GRID_W], axis=-1).astype(np.float32)
    ang = pos[:, :, None] * freqs
    cos, sin = np.cos(ang), np.sin(ang)
    cos_t = np.concatenate([cos, cos], axis=-1).reshape(SEQ, GLA_DK)
    sin_t = np.concatenate([-sin, sin], axis=-1).reshape(SEQ, GLA_DK)
    cos_t = np.concatenate([np.ones((CTX, GLA_DK), np.float32), cos_t], axis=0)
    sin_t = np.concatenate([np.zeros((CTX, GLA_DK), np.float32), sin_t], axis=0)
    return jnp.asarray(cos_t, F32), jnp.asarray(sin_t, F32)


def _nat_kernel(q_ref, k_ref, v_ref, bias_ref, y_ref, *, need_ctx):
    kc = k_ref[0:CTX, :]
    vc = v_ref[0:CTX, :]
    if need_ctx:
        s = _dot_nt(q_ref[0:CTX, :], kc)
        p = jnp.exp(s - jnp.max(s, axis=-1, keepdims=True))
        o = _dot(p.astype(BF16), vc) / jnp.sum(p, axis=-1, keepdims=True)
        y_ref[0:CTX, :] = o.astype(y_ref.dtype)
    else:
        y_ref[0:CTX, :] = jnp.zeros((CTX, NAT_DH), y_ref.dtype)
    n_keys = WIN_R * GRID_W

    def scores(r):
        rs = jnp.clip(r - WIN_R // 2, 0, GRID_H - WIN_R)
        q0 = pl.multiple_of(CTX + r * GRID_W, GRID_W)
        k0 = pl.multiple_of(CTX + rs * GRID_W, GRID_W)
        q = q_ref[pl.ds(q0, GRID_W), :]
        s_lat = _dot_nt(q, k_ref[pl.ds(k0, n_keys), :]) + bias_ref[r - rs]
        s_ctx = _dot_nt(q, kc)
        return q0, k0, s_lat, s_ctx

    def probs(q0, k0, s_lat, s_ctx):
        m = jnp.maximum(jnp.max(s_lat, axis=-1, keepdims=True), jnp.max(s_ctx, axis=-1, keepdims=True))
        p_lat = jnp.exp(s_lat - m)
        p_ctx = jnp.exp(s_ctx - m)
        denom = jnp.sum(p_lat, axis=-1, keepdims=True) + jnp.sum(p_ctx, axis=-1, keepdims=True)
        return q0, k0, p_lat.astype(BF16), p_ctx.astype(BF16), denom

    def values(q0, k0, p_lat, p_ctx, denom):
        o = _dot(p_lat, v_ref[pl.ds(k0, n_keys), :]) + _dot(p_ctx, vc)
        return q0, (o / denom).astype(y_ref.dtype)

    def body(it, carry):
        rows = [it * NAT_ROWS_PER_STEP + u for u in range(NAT_ROWS_PER_STEP)]
        outs = [values(*pr) for pr in [probs(*sc) for sc in [scores(r) for r in rows]]]
        for q0, o in outs:
            y_ref[pl.ds(q0, GRID_W), :] = o
        return carry

    lax.fori_loop(0, GRID_H // NAT_ROWS_PER_STEP, body, 0)


def nat_attention(qkv, bias_tbl, need_ctx):
    return pl.pallas_call(
        functools.partial(_nat_kernel, need_ctx=need_ctx),
        grid=(NAT_H, BATCH),
        in_specs=[
            pl.BlockSpec((ROWS, NAT_DH), lambda h, b: (b, h)),
            pl.BlockSpec((ROWS, NAT_DH), lambda h, b: (b, NAT_H + h)),
            pl.BlockSpec((ROWS, NAT_DH), lambda h, b: (b, 2 * NAT_H + h)),
            pl.BlockSpec((None, WIN_R, GRID_W, WIN_R * GRID_W), lambda h, b: (h, 0, 0, 0)),
        ],
        out_specs=pl.BlockSpec((ROWS, NAT_DH), lambda h, b: (b, h)),
        out_shape=jax.ShapeDtypeStruct((T, D), BF16),
        compiler_params=_params("arbitrary", "arbitrary"),
        name="nat_attention",
    )(qkv, qkv, qkv, bias_tbl)


def nat_bias_table(rpb):
    qc = np.arange(GRID_W)[:, None]
    kcol = np.arange(GRID_W)[None, :]
    start = np.clip(qc - WIN_C // 2, 0, GRID_W - WIN_C)
    in_win = (kcol >= start) & (kcol < start + WIN_C)
    dc = np.clip(kcol - qc + WIN_C - 1, 0, 2 * WIN_C - 2)
    pick = jnp.asarray(dc[None] == np.arange(2 * WIN_C - 1)[:, None, None], F32)
    by_col = jnp.einsum("hdc,cqk->hqdk", rpb, pick, precision=HIGHEST)
    by_col = jnp.where(in_win[None, :, None, :], by_col, -jnp.inf)
    tbl = jnp.stack([by_col[:, :, WIN_R - 1 - off:2 * WIN_R - 1 - off, :] for off in range(WIN_R)], axis=1)
    return tbl.reshape(NAT_H, WIN_R, GRID_W, WIN_R * GRID_W)


def _pair_slots(e0_ref, e1_ref, r0_ref, r1_ref, start_ref, t):
    return start_ref[e0_ref[t]] + r0_ref[t], start_ref[e1_ref[t]] + r1_ref[t]


def _is_routed_tile(i, latent_only):
    return (i % (ROWS // TG) >= CTX // TG) if latent_only else None


def _dispatch_kernel(e0_ref, e1_ref, r0_ref, r1_ref, start_ref, part_ref, h_ref, o_ref, zero_ref, sem, *,
                     latent_only):
    i = pl.program_id(0)
    base = i * TG

    @pl.when(i == 0)
    def _():
        zero_ref[...] = jnp.zeros_like(zero_ref)

        def zero_copy(p):
            return pltpu.make_async_copy(zero_ref, o_ref.at[pl.ds(pl.multiple_of(p * TM_PART, TM_PART), TM_PART)],
                                         sem.at[2])

        def start(p, carry):
            pl.when(part_ref[p] < TM_PART)(lambda: zero_copy(p).start())
            return carry

        def wait(p, carry):
            pl.when(part_ref[p] < TM_PART)(lambda: zero_copy(p).wait())
            return carry

        n_parts = o_ref.shape[0] // TM_PART
        lax.fori_loop(0, n_parts, start, 0)
        lax.fori_loop(0, n_parts, wait, 0)

    def start_rows(r, carry):
        row = h_ref.at[pl.ds(r, 1)]
        s0, s1 = _pair_slots(e0_ref, e1_ref, r0_ref, r1_ref, start_ref, base + r)
        pltpu.make_async_copy(row, o_ref.at[pl.ds(s0, 1)], sem.at[0]).start()
        pltpu.make_async_copy(row, o_ref.at[pl.ds(s1, 1)], sem.at[1]).start()
        return carry

    def run():
        lax.fori_loop(0, TG, start_rows, 0, unroll=DMA_UNROLL)
        for s in range(2):
            pltpu.make_async_copy(h_ref, o_ref.at[pl.ds(0, TG)], sem.at[s]).wait()

    routed = _is_routed_tile(i, latent_only)
    if routed is None:
        run()
    else:
        pl.when(routed)(run)


def moe_dispatch(routing, part_rows, h_words, n_slots, latent_only):
    words = h_words.shape[1]
    return pl.pallas_call(
        functools.partial(_dispatch_kernel, latent_only=latent_only),
        grid_spec=pltpu.PrefetchScalarGridSpec(
            num_scalar_prefetch=6,
            grid=(T // TG,),
            in_specs=[pl.BlockSpec((TG, words), lambda i, *_: (i, 0))],
            out_specs=pl.BlockSpec(memory_space=pl.ANY),
            scratch_shapes=[pltpu.VMEM((TM_PART, words), jnp.uint32), pltpu.SemaphoreType.DMA((3,))],
        ),
        out_shape=jax.ShapeDtypeStruct((n_slots, words), jnp.uint32),
        compiler_params=_params("arbitrary"),
        name="moe_dispatch",
    )(*routing, part_rows, h_words)


def _expert_changed(te_ref, i):
    return (i == 0) | (te_ref[i] != te_ref[jnp.maximum(i - 1, 0)])


def _on_occupied_prefix(n_rows, tile, o_ref, compute):
    for parts in range(tile // TM_PART + 1):
        p = parts * TM_PART

        @pl.when((n_rows > p - TM_PART) & (n_rows <= p))
        def _():
            if p > 0:
                o_ref[0:p, :] = compute(p)
            if p < tile:
                o_ref[p:tile, :] = jnp.zeros((tile - p, o_ref.shape[1]), o_ref.dtype)


def _moe_up_kernel(te_ref, tr_ref, ts_ref, to_ref, a_ref, wg_ref, wu_ref, o_ref, wgb_ref, wub_ref):
    del ts_ref, to_ref
    i = pl.program_id(1)

    @pl.when(_expert_changed(te_ref, i))
    def _():
        wgb_ref[...] = wg_ref[...].astype(BF16)
        wub_ref[...] = wu_ref[...].astype(BF16)

    half = D // 2

    def compute(p):
        lo, hi = _unpack_words(a_ref[0:p, :])
        gate = _dot(lo, wgb_ref[0:half, :]) + _dot(hi, wgb_ref[half:D, :])
        up = _dot(lo, wub_ref[0:half, :]) + _dot(hi, wub_ref[half:D, :])
        return (_silu(gate) * up).astype(o_ref.dtype)

    _on_occupied_prefix(tr_ref[i], TM_UP, o_ref, compute)


def moe_up(tiles, xw, wg, wu, j):
    n_slots = xw.shape[0]
    w_spec = pl.BlockSpec((None, None, D, TF_MOE), lambda f, i, te, tr, ts, to: (j, te[i], 0, f))
    return pl.pallas_call(
        _moe_up_kernel,
        grid_spec=pltpu.PrefetchScalarGridSpec(
            num_scalar_prefetch=4,
            grid=(D_FF // TF_MOE, n_slots // TM_UP),
            in_specs=[pl.BlockSpec((TM_UP, D // 2), lambda f, i, te, tr, ts, to: (ts[i], 0)), w_spec, w_spec],
            out_specs=pl.BlockSpec((TM_UP, TF_MOE), lambda f, i, te, tr, ts, to: (to[i], f)),
            scratch_shapes=[pltpu.VMEM((D, TF_MOE), BF16), pltpu.VMEM((D, TF_MOE), BF16)],
        ),
        out_shape=jax.ShapeDtypeStruct((n_slots, D_FF), BF16),
        compiler_params=_params("arbitrary", "arbitrary"),
        name="moe_up",
    )(*tiles, xw, wg, wu)


def _moe_down_kernel(te_ref, tr_ref, ts_ref, to_ref, a_ref, w_ref, o_ref, wb_ref):
    del ts_ref, to_ref
    i = pl.program_id(1)

    @pl.when(_expert_changed(te_ref, i))
    def _():
        wb_ref[...] = w_ref[...].astype(BF16)

    _on_occupied_prefix(tr_ref[i], TM_DOWN, o_ref, lambda p: _dot(a_ref[0:p, :], wb_ref[...]))


def moe_down(tiles, act, wd, j):
    n_slots = act.shape[0]
    tn = TN_MOE_DOWN
    return pl.pallas_call(
        _moe_down_kernel,
        grid_spec=pltpu.PrefetchScalarGridSpec(
            num_scalar_prefetch=4,
            grid=(D // tn, n_slots // TM_DOWN),
            in_specs=[
                pl.BlockSpec((TM_DOWN, D_FF), lambda n, i, te, tr, ts, to: (ts[i], 0)),
                pl.BlockSpec((None, None, D_FF, tn), lambda n, i, te, tr, ts, to: (j, te[i], 0, n)),
            ],
            out_specs=pl.BlockSpec((TM_DOWN, tn), lambda n, i, te, tr, ts, to: (to[i], n)),
            scratch_shapes=[pltpu.VMEM((D_FF, tn), BF16)],
        ),
        out_shape=jax.ShapeDtypeStruct((n_slots, D), F32),
        compiler_params=_params("arbitrary", "arbitrary"),
        name="moe_down",
    )(*tiles, act, wd)


def _combine_kernel(e0_ref, e1_ref, r0_ref, r1_ref, start_ref, ys_ref, x_ref, gt_ref, w_ref, fg_ref, o_ref,
                    b0_ref, b1_ref, sem, *, latent_only, final):
    i = pl.program_id(0)
    base = i * TG

    def start_rows(r, carry):
        s0, s1 = _pair_slots(e0_ref, e1_ref, r0_ref, r1_ref, start_ref, base + r)
        pltpu.make_async_copy(ys_ref.at[pl.ds(s0, 1)], b0_ref.at[pl.ds(r, 1)], sem.at[0]).start()
        pltpu.make_async_copy(ys_ref.at[pl.ds(s1, 1)], b1_ref.at[pl.ds(r, 1)], sem.at[1]).start()
        return carry

    def run():
        lax.fori_loop(0, TG, start_rows, 0, unroll=DMA_UNROLL)
        pltpu.make_async_copy(ys_ref.at[pl.ds(0, TG)], b0_ref, sem.at[0]).wait()
        pltpu.make_async_copy(ys_ref.at[pl.ds(0, TG)], b1_ref, sem.at[1]).wait()
        mix = w_ref[:, 0:1] * b0_ref[...] + w_ref[:, 1:2] * b1_ref[...]
        out = x_ref[...] + _row_gate(gt_ref, i, TG) * mix
        if final:
            out = out * lax.rsqrt(jnp.mean(out * out, axis=-1, keepdims=True) + EPS) * fg_ref[...]
        o_ref[...] = out

    routed = _is_routed_tile(i, latent_only)
    if routed is None:
        run()
    else:
        pl.when(routed)(run)
        if not final:
            @pl.when(jnp.logical_not(routed))
            def _():
                o_ref[...] = x_ref[...]


def moe_combine(routing, ys, x, mods, layer, gate_col, wcol, latent_only, final_g=None):
    final = final_g is not None
    assert latent_only or not final
    per_batch, n_ctx, n_lat = ROWS // TG, CTX // TG, SEQ // TG
    if final:
        out_rows = BATCH * SEQ
        out_map = lambda i, *_: ((i // per_batch) * n_lat + jnp.maximum(i % per_batch - n_ctx, 0), 0)
    else:
        out_rows = T
        out_map = lambda i, *_: (i, 0)
        final_g = jnp.ones((D,), F32)
    return pl.pallas_call(
        functools.partial(_combine_kernel, latent_only=latent_only, final=final),
        grid_spec=pltpu.PrefetchScalarGridSpec(
            num_scalar_prefetch=5,
            grid=(T // TG,),
            in_specs=[
                pl.BlockSpec(memory_space=pl.ANY),
                pl.BlockSpec((TG, D), lambda i, *_: (i, 0)),
                pl.BlockSpec((None, N_MOD, D), lambda i, *_: (layer, 0, gate_col)),
                pl.BlockSpec((TG, LANES), lambda i, *_: (i, 0)),
                pl.BlockSpec((1, D), lambda i, *_: (0, 0)),
            ],
            out_specs=pl.BlockSpec((TG, D), out_map),
            scratch_shapes=[pltpu.VMEM((TG, D), F32), pltpu.VMEM((TG, D), F32), pltpu.SemaphoreType.DMA((2,))],
        ),
        out_shape=jax.ShapeDtypeStruct((out_rows, D), F32),
        compiler_params=_params("arbitrary"),
        name="moe_combine",
    )(*routing, ys, x, mods, wcol, final_g.reshape(1, D))


def moe_slots(n_tokens):
    return (2 * n_tokens // TM_UP + N_EXP) * TM_UP


def moe_plan(counts, n_slots):
    group = (counts + TM_UP - 1) // TM_UP * TM_UP
    start = jnp.cumsum(group) - group
    end = start + counts
    last_row = jnp.max(jnp.where(counts > 0, end, 0)) - 1

    def of_expert(table, expert):
        return jnp.sum(jnp.where(expert[:, None] == jnp.arange(N_EXP), table, 0), axis=1)

    def occupied_rows(tm):
        ids = jnp.arange(n_slots // tm)
        expert = jnp.sum((start + group)[None, :] <= jnp.minimum(ids * tm, last_row)[:, None], axis=1)
        return ids, expert, jnp.clip(of_expert(end, expert) - ids * tm, 0, tm)

    def tiles(tm):
        ids, expert, rows = occupied_rows(tm)
        first = of_expert(start, expert) // tm
        n_occ = (of_expert(counts, expert) + tm - 1) // tm
        k = ids - first
        tile = jnp.where(k < n_occ, first + (k - 1) % jnp.maximum(n_occ, 1), ids)
        rows_v = jnp.clip(of_expert(end, expert) - tile * tm, 0, tm)
        fetched = jnp.where(rows_v > 0, ids, 0)
        last_fetch = jnp.max(jnp.where(ids[None, :] <= ids[:, None], fetched[None, :], 0), axis=1)
        block = jnp.sum(jnp.where(ids[None, :] == last_fetch[:, None], tile[None, :], 0), axis=1)
        return tuple(a.astype(jnp.int32) for a in (expert, rows_v, block, tile))

    return start.astype(jnp.int32), tiles(TM_UP), tiles(TM_DOWN), occupied_rows(TM_PART)[2].astype(jnp.int32)


def _pad_cols(w, n):
    return jnp.pad(w, ((0, 0), (0, n - w.shape[1])))


def kernel(x, c, ctx, c_ctx, ada_w, ada_b, norm_mix_g, norm_ffn_g, gla_w_in, gla_wg_fwd, gla_bg_fwd, gla_wg_bwd, gla_bg_bwd, gla_norm_g, gla_w_out, nat_w_in, nat_rpb, nat_w_out, ffn_w_gate, ffn_w_up, ffn_w_down, moe_router, moe_w_gate, moe_w_up, moe_w_down, final_norm_g):
    c8 = jnp.concatenate([c, c_ctx[None, :], jnp.zeros((N_MOD - BATCH - 1, D), F32)], axis=0)
    mods = ada_tables(c8, ada_w, ada_b)
    cos_t, sin_t = rope_tables()
    router_pad = jnp.pad(moe_router, ((0, 0), (0, 0), (0, LANES - N_EXP)))

    for i in range(DEPTH):
        j = i // 2
        last = i == DEPTH - 1
        if i == 0:
            xs, h = embed_modulate(x, ctx, norm_mix_g, mods, i, 0, 1)
        else:
            h = modulate(xs, norm_mix_g, mods, i, 0, 1)
        if i % 2 == 0:
            qk = mm_rope(h, gla_w_in, j, cos_t, sin_t, tm=1152)
            vr = mm_plain(h, gla_w_in, j, 2 * GLA_QK, 2 * D, BF16, tm=1152, tn=1024)
            z0 = 2 * GLA_QK + 2 * D
            wz_pad = _pad_cols(gla_w_in[j, :, z0:z0 + 2 * GLA_RANK], LANES)
            wf_pad = jnp.pad(gla_wg_fwd[j], ((0, LANES - GLA_RANK), (0, 0)))
            wb_pad = jnp.pad(gla_wg_bwd[j], ((GLA_RANK, LANES - 2 * GLA_RANK), (0, 0)))
            gf, gb = gla_gates(h, wz_pad, wf_pad, gla_bg_fwd, wb_pad, gla_bg_bwd, j, tm=1152)
            y = gla_scan(qk, vr, gf, gb, gla_norm_g, j)
            xs = mm_resid(y, gla_w_out, j, xs, mods, i, 2, tm=1152, tn=1024)
        else:
            qkv = mm_plain(h, nat_w_in, j, 0, 3 * D, BF16, tm=1152, tn=1024,
                           scale=NAT_DH ** -0.5, n_scaled=D // 1024)
            y = nat_attention(qkv, nat_bias_table(nat_rpb[j]), not last)
            xs = mm_resid(y, nat_w_out, j, xs, mods, i, 2, tm=1152, tn=1024)
        if i % 2 == 0:
            h2 = modulate(xs, norm_ffn_g, mods, i, 3, 4)
            act = mm_swiglu(h2, ffn_w_gate, ffn_w_up, j, tm=1152, tn=512)
            xs = mm_resid(act, ffn_w_down, j, xs, mods, i, 5, tm=576, tn=512)
        else:
            h_words, idx, rank, wcol, counts = modulate_route(xs, norm_ffn_g, mods, i, 3, 4, router_pad, j, last)
            n_slots = moe_slots(BATCH * SEQ if last else T)
            start, up_tiles, down_tiles, part_rows = moe_plan(counts[:, 0], n_slots)
            routing = (idx[0], idx[1], rank[0], rank[1], start)
            xw = moe_dispatch(routing, part_rows, h_words, n_slots, last)
            act = moe_up(up_tiles, xw, moe_w_gate, moe_w_up, j)
            ys = moe_down(down_tiles, act, moe_w_down, j)
            xs = moe_combine(routing, ys, xs, mods, i, 5, wcol, last, final_norm_g if last else None)
    return xs.reshape(BATCH, SEQ, D)
```

```python
import functools

import numpy as np
import jax
import jax.numpy as jnp
from jax import lax
from jax.experimental import pallas as pl
from jax.experimental.pallas import tpu as pltpu

F32 = jnp.float32
BF16 = jnp.bfloat16
HIGHEST = lax.Precision.HIGHEST

D = 2048
BATCH = 4
SEQ = 2048
CTX = 256
ROWS = CTX + SEQ
T = BATCH * ROWS
DEPTH = 4
GRID_W = 64
GRID_H = SEQ // GRID_W
EPS = 1e-6
ROPE_BASE = 10000.0
CTX_GROUP = BATCH
N_MOD = 8

GLA_H = 4
GLA_DK = 256
GLA_DV = 512
GLA_QK = GLA_H * GLA_DK
GLA_RANK = 16
GLA_TAU = 16.0
GLA_C = 64
GLA_SPLIT_DECAY_MAX = 60.0

NAT_H = 16
NAT_DH = 128
WIN_R = 8
WIN_C = 16
NAT_ROWS_PER_STEP = 16

D_FF = 5632
N_EXP = 8

LANES = 128
VMEM_LIMIT = 56 * 1024 * 1024

TM_MOD = 256
TM_MODULATE = 1152
TM_UP = 1024
TM_DOWN = 512
TM_PART = 256
TF_MOE = 512
TN_MOE_DOWN = 1024
TG = 256
DMA_UNROLL = 16


def _params(*sem):
    return pltpu.CompilerParams(dimension_semantics=sem, vmem_limit_bytes=VMEM_LIMIT)


def _dot(a, b):
    return jnp.dot(a, b, preferred_element_type=F32)


def _dot_nt(a, b):
    return lax.dot_general(a, b, (((1,), (1,)), ((), ())), preferred_element_type=F32)


def _dot_tn(a, b):
    return lax.dot_general(a, b, (((0,), (0,)), ((), ())), preferred_element_type=F32)


def _silu(v):
    return v * jax.nn.sigmoid(v)


def _row_gate(tab_ref, m, tm):
    start = m * tm
    b = start // ROWS
    rows = start % ROWS + lax.broadcasted_iota(jnp.int32, (tm, 1), 0)
    per_batch = tab_ref[pl.ds(b, 1), :]
    per_ctx = tab_ref[CTX_GROUP:CTX_GROUP + 1, :]
    return jnp.where(rows < CTX, per_ctx, per_batch)


def _ada_kernel(c_ref, w_ref, b_ref, o_ref):
    a = _silu(c_ref[...]).astype(BF16)
    o_ref[...] = _dot(a, w_ref[...].astype(BF16)) + b_ref[...]


def ada_tables(c8, ada_w, ada_b):
    tn = 1024
    return pl.pallas_call(
        _ada_kernel,
        grid=(DEPTH, 6 * D // tn),
        in_specs=[
            pl.BlockSpec((N_MOD, D), lambda l, n: (0, 0)),
            pl.BlockSpec((None, D, tn), lambda l, n: (l, 0, n)),
            pl.BlockSpec((None, 1, tn), lambda l, n: (l, 0, n)),
        ],
        out_specs=pl.BlockSpec((None, N_MOD, tn), lambda l, n: (l, 0, n)),
        out_shape=jax.ShapeDtypeStruct((DEPTH, N_MOD, 6 * D), F32),
        compiler_params=_params("arbitrary", "arbitrary"),
        name="ada_tables",
    )(c8, ada_w, ada_b.reshape(DEPTH, 1, 6 * D))


def _modulate_rows(x, g_ref, sh_ref, sc_ref):
    tm = x.shape[0]
    y = x * lax.rsqrt(jnp.mean(x * x, axis=-1, keepdims=True) + EPS) * g_ref[...]
    i = pl.program_id(0)
    return y * (1.0 + _row_gate(sc_ref, i, tm)) + _row_gate(sh_ref, i, tm)


def _modulated(x_ref, g_ref, sh_ref, sc_ref):
    return _modulate_rows(x_ref[...], g_ref, sh_ref, sc_ref)


def _embed_modulate_kernel(x_ref, c_ref, g_ref, sh_ref, sc_ref, xs_ref, h_ref):
    is_ctx = pl.program_id(0) % (ROWS // TM_MOD) < CTX // TM_MOD
    rows = jnp.where(is_ctx, c_ref[...], x_ref[...])
    xs_ref[...] = rows
    h_ref[...] = _modulate_rows(rows, g_ref, sh_ref, sc_ref).astype(h_ref.dtype)


def embed_modulate(x, ctx, norm_g, mods, layer, sh_col, sc_col):
    per_batch = ROWS // TM_MOD
    n_ctx = CTX // TM_MOD
    n_lat = SEQ // TM_MOD
    specs = _mod_specs(layer, sh_col, sc_col, TM_MOD)
    return pl.pallas_call(
        _embed_modulate_kernel,
        grid=(T // TM_MOD,),
        in_specs=[
            pl.BlockSpec((TM_MOD, D), lambda i: ((i // per_batch) * n_lat + jnp.maximum(i % per_batch - n_ctx, 0), 0)),
            pl.BlockSpec((TM_MOD, D), lambda i: ((i // per_batch) * n_ctx + jnp.minimum(i % per_batch, n_ctx - 1), 0)),
        ] + specs[1:],
        out_specs=[pl.BlockSpec((TM_MOD, D), lambda i: (i, 0)), pl.BlockSpec((TM_MOD, D), lambda i: (i, 0))],
        out_shape=[jax.ShapeDtypeStruct((T, D), F32), jax.ShapeDtypeStruct((T, D), BF16)],
        compiler_params=_params("arbitrary"),
        name="embed_modulate",
    )(x.reshape(BATCH * SEQ, D), ctx.reshape(BATCH * CTX, D), norm_g.reshape(DEPTH, 1, D), mods, mods)


def _modulate_kernel(x_ref, g_ref, sh_ref, sc_ref, o_ref):
    o_ref[...] = _modulated(x_ref, g_ref, sh_ref, sc_ref).astype(o_ref.dtype)


def _mod_specs(layer, sh_col, sc_col, tm):
    return [
        pl.BlockSpec((tm, D), lambda i: (i, 0)),
        pl.BlockSpec((None, 1, D), lambda i: (layer, 0, 0)),
        pl.BlockSpec((None, N_MOD, D), lambda i: (layer, 0, sh_col)),
        pl.BlockSpec((None, N_MOD, D), lambda i: (layer, 0, sc_col)),
    ]


def modulate(x, norm_g, mods, layer, sh_col, sc_col):
    tm = TM_MODULATE
    return pl.pallas_call(
        _modulate_kernel,
        grid=(T // tm,),
        in_specs=_mod_specs(layer, sh_col, sc_col, tm),
        out_specs=pl.BlockSpec((tm, D), lambda i: (i, 0)),
        out_shape=jax.ShapeDtypeStruct((T, D), BF16),
        compiler_params=_params("arbitrary"),
        name="modulate",
    )(x, norm_g.reshape(DEPTH, 1, D), mods, mods)


def _pack_words(h):
    half = h.shape[1] // 2
    lo = lax.bitcast_convert_type(h[:, :half].astype(BF16).astype(F32), jnp.uint32)
    hi = lax.bitcast_convert_type(h[:, half:].astype(BF16).astype(F32), jnp.uint32)
    return (lo >> 16) | (hi & jnp.uint32(0xFFFF0000))


def _unpack_words(w):
    lo = lax.bitcast_convert_type(w << 16, F32).astype(BF16)
    hi = lax.bitcast_convert_type(w & jnp.uint32(0xFFFF0000), F32).astype(BF16)
    return lo, hi


def _modulate_route_kernel(x_ref, g_ref, sh_ref, sc_ref, rt_ref, hw_ref, idx_ref, rank_ref, wcol_ref, cnt_ref,
                           base_ref, *, latent_only):
    i = pl.program_id(0)

    @pl.when(i == 0)
    def _():
        base_ref[...] = jnp.zeros_like(base_ref)

    h = _modulated(x_ref, g_ref, sh_ref, sc_ref)
    hw_ref[...] = _pack_words(h)
    logits = jnp.dot(h, rt_ref[...], precision=HIGHEST, preferred_element_type=F32)

    lane = lax.broadcasted_iota(jnp.int32, logits.shape, 1)
    lm = jnp.where(lane < N_EXP, logits, -jnp.inf)
    c1 = jnp.max(lm, axis=1, keepdims=True)
    j1 = jnp.min(jnp.where(lm == c1, lane, LANES), axis=1, keepdims=True)
    c2 = jnp.max(jnp.where(lane == j1, -jnp.inf, lm), axis=1, keepdims=True)
    e = jnp.exp(c2 - c1)
    wcol_ref[...] = jnp.where(lane == 0, 1.0 / (1.0 + e), jnp.where(lane == 1, e / (1.0 + e), 0.0))

    lt = logits.T[:N_EXP]
    eid = lax.broadcasted_iota(jnp.int32, lt.shape, 0)
    m1 = jnp.max(lt, axis=0, keepdims=True)
    i1 = jnp.min(jnp.where(lt == m1, eid, N_EXP), axis=0, keepdims=True)
    lt2 = jnp.where(eid == i1, -jnp.inf, lt)
    m2 = jnp.max(lt2, axis=0, keepdims=True)
    i2 = jnp.min(jnp.where(lt2 == m2, eid, N_EXP), axis=0, keepdims=True)
    idx_ref[...] = jnp.concatenate([i1, i2], axis=0)

    routed = 1.0
    if latent_only:
        routed = jnp.where(i % (ROWS // TM_MOD) < CTX // TM_MOD, 0.0, 1.0)
    oh1 = jnp.where(eid == i1, routed, 0.0)
    oh2 = jnp.where(eid == i2, routed, 0.0)
    both = oh1 + oh2
    tj = lax.broadcasted_iota(jnp.int32, (TM_MOD, TM_MOD), 0)
    tt = lax.broadcasted_iota(jnp.int32, (TM_MOD, TM_MOD), 1)
    before = _dot(both.astype(BF16), jnp.where(tj < tt, 1.0, 0.0).astype(BF16))
    seen = base_ref[:, 0:1] + before
    r1 = jnp.sum(oh1 * seen, axis=0, keepdims=True)
    r2 = jnp.sum(oh2 * seen, axis=0, keepdims=True)
    rank_ref[...] = jnp.concatenate([r1, r2], axis=0).astype(jnp.int32)
    total = base_ref[...] + jnp.sum(both, axis=1, keepdims=True)
    base_ref[...] = total
    cnt_ref[...] = total.astype(jnp.int32)


def modulate_route(x, norm_g, mods, layer, sh_col, sc_col, router_pad, j, latent_only):
    pairs = pl.BlockSpec((2, TM_MOD), lambda i: (0, i))
    return pl.pallas_call(
        functools.partial(_modulate_route_kernel, latent_only=latent_only),
        grid=(T // TM_MOD,),
        in_specs=_mod_specs(layer, sh_col, sc_col, TM_MOD) + [
            pl.BlockSpec((None, D, LANES), lambda i: (j, 0, 0)),
        ],
        out_specs=[
            pl.BlockSpec((TM_MOD, D // 2), lambda i: (i, 0)),
            pairs, pairs,
            pl.BlockSpec((TM_MOD, LANES), lambda i: (i, 0)),
            pl.BlockSpec((N_EXP, LANES), lambda i: (0, 0)),
        ],
        out_shape=[
            jax.ShapeDtypeStruct((T, D // 2), jnp.uint32),
            jax.ShapeDtypeStruct((2, T), jnp.int32),
            jax.ShapeDtypeStruct((2, T), jnp.int32),
            jax.ShapeDtypeStruct((T, LANES), F32),
            jax.ShapeDtypeStruct((N_EXP, LANES), jnp.int32),
        ],
        scratch_shapes=[pltpu.VMEM((N_EXP, LANES), F32)],
        compiler_params=_params("arbitrary"),
        name="modulate_route",
    )(x, norm_g.reshape(DEPTH, 1, D), mods, mods, router_pad)


def _cast_weight(w_ref, wb_ref):
    @pl.when(pl.program_id(1) == 0)
    def _():
        wb_ref[...] = w_ref[...].astype(BF16)


def _mm_plain_kernel(a_ref, w_ref, o_ref, wb_ref, *, scale, n_scaled):
    _cast_weight(w_ref, wb_ref)
    acc = _dot(a_ref[...], wb_ref[...])
    if n_scaled:
        acc = acc * jnp.where(pl.program_id(0) < n_scaled, scale, 1.0)
    o_ref[...] = acc.astype(o_ref.dtype)


def mm_plain(a, w, layer, col0, n_out, out_dtype, tm, tn, scale=1.0, n_scaled=0):
    k = a.shape[1]
    c0 = col0 // tn
    return pl.pallas_call(
        functools.partial(_mm_plain_kernel, scale=scale, n_scaled=n_scaled),
        grid=(n_out // tn, T // tm),
        in_specs=[
            pl.BlockSpec((tm, k), lambda n, m: (m, 0)),
            pl.BlockSpec((None, k, tn), lambda n, m: (layer, 0, n + c0)),
        ],
        out_specs=pl.BlockSpec((tm, tn), lambda n, m: (m, n)),
        out_shape=jax.ShapeDtypeStruct((T, n_out), out_dtype),
        scratch_shapes=[pltpu.VMEM((k, tn), BF16)],
        compiler_params=_params("arbitrary", "arbitrary"),
        name="mm_plain",
    )(a, w)


def _mm_rope_kernel(a_ref, w_ref, cos_ref, sin_ref, o_ref, wb_ref, *, tn):
    _cast_weight(w_ref, wb_ref)
    acc = _dot(a_ref[...], wb_ref[...])
    scale = jnp.where(pl.program_id(0) == 0, GLA_DK ** -0.5, 1.0)
    for s in range(tn // LANES):
        xs = acc[:, s * LANES:(s + 1) * LANES]
        t = (s % 2) * LANES
        rot = xs * cos_ref[:, t:t + LANES] + pltpu.roll(xs, LANES // 2, 1) * sin_ref[:, t:t + LANES]
        o_ref[:, s * LANES:(s + 1) * LANES] = rot * scale


def mm_rope(a, w, layer, cos_t, sin_t, tm):
    tn = GLA_QK
    per_batch = ROWS // tm
    return pl.pallas_call(
        functools.partial(_mm_rope_kernel, tn=tn),
        grid=(2, T // tm),
        in_specs=[
            pl.BlockSpec((tm, D), lambda n, m: (m, 0)),
            pl.BlockSpec((None, D, tn), lambda n, m: (layer, 0, n)),
            pl.BlockSpec((tm, GLA_DK), lambda n, m: (m % per_batch, 0)),
            pl.BlockSpec((tm, GLA_DK), lambda n, m: (m % per_batch, 0)),
        ],
        out_specs=pl.BlockSpec((tm, tn), lambda n, m: (m, n)),
        out_shape=jax.ShapeDtypeStruct((T, 2 * GLA_QK), F32),
        scratch_shapes=[pltpu.VMEM((D, tn), BF16)],
        compiler_params=_params("arbitrary", "arbitrary"),
        name="mm_rope",
    )(a, w, cos_t, sin_t)


def _mm_resid_kernel(a_ref, w_ref, x_ref, gt_ref, o_ref, wb_ref, *, tm):
    _cast_weight(w_ref, wb_ref)
    acc = _dot(a_ref[...], wb_ref[...])
    o_ref[...] = x_ref[...] + _row_gate(gt_ref, pl.program_id(1), tm) * acc


def mm_resid(a, w, layer_w, x, mods, layer, gate_col, tm, tn):
    k = a.shape[1]
    g0 = gate_col * D // tn
    return pl.pallas_call(
        functools.partial(_mm_resid_kernel, tm=tm),
        grid=(D // tn, T // tm),
        in_specs=[
            pl.BlockSpec((tm, k), lambda n, m: (m, 0)),
            pl.BlockSpec((None, k, tn), lambda n, m: (layer_w, 0, n)),
            pl.BlockSpec((tm, tn), lambda n, m: (m, n)),
            pl.BlockSpec((None, N_MOD, tn), lambda n, m: (layer, 0, g0 + n)),
        ],
        out_specs=pl.BlockSpec((tm, tn), lambda n, m: (m, n)),
        out_shape=jax.ShapeDtypeStruct((T, D), F32),
        scratch_shapes=[pltpu.VMEM((k, tn), BF16)],
        compiler_params=_params("arbitrary", "arbitrary"),
        name="mm_resid",
    )(a, w, x, mods)


def _mm_swiglu_kernel(a_ref, wg_ref, wu_ref, o_ref, wgb_ref, wub_ref):
    _cast_weight(wg_ref, wgb_ref)
    _cast_weight(wu_ref, wub_ref)
    a = a_ref[...]
    o_ref[...] = (_silu(_dot(a, wgb_ref[...])) * _dot(a, wub_ref[...])).astype(o_ref.dtype)


def mm_swiglu(a, wg, wu, layer, tm, tn):
    return pl.pallas_call(
        _mm_swiglu_kernel,
        grid=(D_FF // tn, T // tm),
        in_specs=[
            pl.BlockSpec((tm, D), lambda n, m: (m, 0)),
            pl.BlockSpec((None, D, tn), lambda n, m: (layer, 0, n)),
            pl.BlockSpec((None, D, tn), lambda n, m: (layer, 0, n)),
        ],
        out_specs=pl.BlockSpec((tm, tn), lambda n, m: (m, n)),
        out_shape=jax.ShapeDtypeStruct((T, D_FF), BF16),
        scratch_shapes=[pltpu.VMEM((D, tn), BF16), pltpu.VMEM((D, tn), BF16)],
        compiler_params=_params("arbitrary", "arbitrary"),
        name="mm_swiglu",
    )(a, wg, wu)


def _log_sigmoid(v):
    return jnp.minimum(v, 0.0) - jnp.log(1.0 + jnp.exp(-jnp.abs(v)))


def _gla_gate_kernel(h_ref, wz_ref, wf_ref, bf_ref, wb_ref, bb_ref, gf_ref, gb_ref):
    z = _dot(h_ref[...], wz_ref[...].astype(BF16))
    af = jnp.dot(z, wf_ref[...], precision=HIGHEST, preferred_element_type=F32) + bf_ref[...]
    ab = jnp.dot(z, wb_ref[...], precision=HIGHEST, preferred_element_type=F32) + bb_ref[...]
    gf = _log_sigmoid(af) / GLA_TAU
    gb = _log_sigmoid(ab) / GLA_TAU
    c = GLA_C
    ri = lax.broadcasted_iota(jnp.int32, (c, c), 0)
    ci = lax.broadcasted_iota(jnp.int32, (c, c), 1)
    lower = (ci <= ri).astype(BF16)
    upper = (ci >= ri).astype(BF16)

    def terms(v):
        hi = v.astype(BF16)
        rest = v - hi.astype(F32)
        mid = rest.astype(BF16)
        return hi, mid, (rest - mid.astype(F32)).astype(BF16)

    gf3, gb3 = terms(gf), terms(gb)
    for ch in range(h_ref.shape[0] // c):
        rows = slice(ch * c, (ch + 1) * c)
        gf_ref[rows, :] = _dot(lower, gf3[0][rows]) + _dot(lower, gf3[1][rows]) + _dot(lower, gf3[2][rows])
        gb_ref[rows, :] = _dot(upper, gb3[0][rows]) + _dot(upper, gb3[1][rows]) + _dot(upper, gb3[2][rows])


def gla_gates(h, wz_pad, wf_pad, bg_f, wb_pad, bg_b, j, tm):
    row = lambda i: (i, 0)
    return pl.pallas_call(
        _gla_gate_kernel,
        grid=(T // tm,),
        in_specs=[
            pl.BlockSpec((tm, D), row),
            pl.BlockSpec((D, LANES), lambda i: (0, 0)),
            pl.BlockSpec((LANES, GLA_QK), lambda i: (0, 0)),
            pl.BlockSpec((None, 1, GLA_QK), lambda i: (j, 0, 0)),
            pl.BlockSpec((LANES, GLA_QK), lambda i: (0, 0)),
            pl.BlockSpec((None, 1, GLA_QK), lambda i: (j, 0, 0)),
        ],
        out_specs=[pl.BlockSpec((tm, GLA_QK), row), pl.BlockSpec((tm, GLA_QK), row)],
        out_shape=[jax.ShapeDtypeStruct((T, GLA_QK), F32)] * 2,
        compiler_params=_params("arbitrary"),
        name="gla_gates",
    )(h, wz_pad, wf_pad, bg_f.reshape(-1, 1, GLA_QK), wb_pad, bg_b.reshape(-1, 1, GLA_QK))


def _gla_intra_direct(q, k, bc, rev):
    c = GLA_C
    lane8 = lax.broadcasted_iota(jnp.int32, (8, c), 1)
    srow8 = lax.broadcasted_iota(jnp.int32, (8, 1), 0)
    blocks = []
    for r0 in range(0, c, 8):
        q_r, b_r = q[r0:r0 + 8], bc[r0:r0 + 8]
        a_r = jnp.zeros((8, c), F32)
        for jj in (range(r0, c) if rev else range(r0 + 8)):
            t = q_r * k[jj:jj + 1] * jnp.exp(b_r - bc[jj:jj + 1])
            s = jnp.sum(t, axis=-1, keepdims=True)
            seen = (srow8 + r0 <= jj) if rev else (srow8 + r0 >= jj)
            a_r = jnp.where(lane8 == jj, jnp.where(seen, s, 0.0), a_r)
        blocks.append(a_r)
    return jnp.concatenate(blocks, axis=0).astype(BF16)


def _gla_chunks(streams, a_ref):
    c = GLA_C
    ri = lax.broadcasted_iota(jnp.int32, (c, c), 0)
    ci = lax.broadcasted_iota(jnp.int32, (c, c), 1)
    bcs = [b for (q, k, v, b, st, rev) in streams]

    partial = []
    for (q, k, v, g, st, rev), bc in zip(streams, bcs):
        b_end = bc[0:1] if rev else bc[c - 1:c]
        qe = (q * jnp.exp(bc)).astype(BF16)
        st_new = st * jnp.exp(b_end) + _dot_tn(v, (k * jnp.exp(b_end - bc)).astype(BF16))
        partial.append((st.astype(BF16), st_new, qe, b_end))

    for n, ((q, k, v, g, st, rev), bc, (_, _, qe, _)) in enumerate(zip(streams, bcs, partial)):
        a = _dot_nt(qe, (k * jnp.exp(jnp.minimum(-bc, GLA_SPLIT_DECAY_MAX))).astype(BF16))
        a_ref[n] = jnp.where(ci >= ri if rev else ci <= ri, a, 0.0).astype(BF16)

    total_decay = functools.reduce(jnp.maximum, [jnp.max(-b_end) for (_, _, _, b_end) in partial])

    @pl.when(total_decay > GLA_SPLIT_DECAY_MAX)
    def _():
        for n, ((q, k, v, g, st, rev), bc) in enumerate(zip(streams, bcs)):
            a_ref[n] = _gla_intra_direct(q, k, bc, rev)

    return [(_dot_nt(qe, st_b) + _dot(a_ref[n], v), st_new)
            for n, ((q, k, v, g, st, rev), (st_b, st_new, qe, _)) in enumerate(zip(streams, partial))]


def _gla_kernel(q_ref, k_ref, v_ref, r_ref, gf_ref, gb_ref, ng_ref, y_ref, stf_ref, stb_ref, of_ref, ob_ref,
                a_ref):
    nch = ROWS // GLA_C
    nctx = CTX // GLA_C

    def chunk_rows(ch):
        return pl.ds(pl.multiple_of(ch * GLA_C, GLA_C), GLA_C)

    def stream(rows, g_ref, st_ref, rev):
        return q_ref[rows, :], k_ref[rows, :], v_ref[rows, :], g_ref[rows, :], st_ref[...], rev

    stf_ref[...] = jnp.zeros_like(stf_ref)
    stb_ref[...] = jnp.zeros_like(stb_ref)

    def scan(s, carry):
        rows_f = chunk_rows(s)
        rows_b = chunk_rows(jnp.where(s < nctx, nctx - 1 - s, nch + nctx - 1 - s))
        (o_f, st_f), (o_b, st_b) = _gla_chunks([stream(rows_f, gf_ref, stf_ref, False),
                                                stream(rows_b, gb_ref, stb_ref, True)], a_ref)
        of_ref[rows_f, :] = o_f
        ob_ref[rows_b, :] = o_b
        stf_ref[...] = st_f
        stb_ref[...] = st_b
        return carry

    lax.fori_loop(0, nch, scan, 0)

    def readout(ch, carry):
        rows = chunk_rows(ch)
        o = of_ref[rows, :] + ob_ref[rows, :]
        on = o * lax.rsqrt(jnp.mean(o * o, axis=-1, keepdims=True) + EPS) * ng_ref[...]
        y_ref[rows, :] = (on * _silu(r_ref[rows, :].astype(F32))).astype(y_ref.dtype)
        return carry

    lax.fori_loop(0, nch, readout, 0)


def gla_scan(qk, vr, gf, gb, norm_g, j):
    return pl.pallas_call(
        _gla_kernel,
        grid=(BATCH, GLA_H),
        in_specs=[
            pl.BlockSpec((ROWS, GLA_DK), lambda b, h: (b, h)),
            pl.BlockSpec((ROWS, GLA_DK), lambda b, h: (b, GLA_H + h)),
            pl.BlockSpec((ROWS, GLA_DV), lambda b, h: (b, h)),
            pl.BlockSpec((ROWS, GLA_DV), lambda b, h: (b, GLA_H + h)),
            pl.BlockSpec((ROWS, GLA_DK), lambda b, h: (b, h)),
            pl.BlockSpec((ROWS, GLA_DK), lambda b, h: (b, h)),
            pl.BlockSpec((None, 1, GLA_DV), lambda b, h: (j, 0, 0)),
        ],
        out_specs=pl.BlockSpec((ROWS, GLA_DV), lambda b, h: (b, h)),
        out_shape=jax.ShapeDtypeStruct((T, D), BF16),
        scratch_shapes=[pltpu.VMEM((GLA_DV, GLA_DK), F32), pltpu.VMEM((GLA_DV, GLA_DK), F32),
                        pltpu.VMEM((ROWS, GLA_DV), F32), pltpu.VMEM((ROWS, GLA_DV), F32),
                        pltpu.VMEM((2, GLA_C, GLA_C), BF16)],
        compiler_params=_params("arbitrary", "arbitrary"),
        name="gla_scan",
    )(qk, qk, vr, vr, gf, gb, norm_g.reshape(-1, 1, GLA_DV))


def rope_tables():
    half = GLA_DK // 2
    freqs = ROPE_BASE ** (-np.arange(0, half, 2, dtype=np.float32) / half)
    t = np.arange(SEQ)
    pos = np.stack([t // GRID_W, t % GRID_W], axis=-1).astype(np.float32)
    ang = pos[:, :, None] * freqs
    cos, sin = np.cos(ang), np.sin(ang)
    cos_t = np.concatenate([cos, cos], axis=-1).reshape(SEQ, GLA_DK)
    sin_t = np.concatenate([-sin, sin], axis=-1).reshape(SEQ, GLA_DK)
    cos_t = np.concatenate([np.ones((CTX, GLA_DK), np.float32), cos_t], axis=0)
    sin_t = np.concatenate([np.zeros((CTX, GLA_DK), np.float32), sin_t], axis=0)
    return jnp.asarray(cos_t, F32), jnp.asarray(sin_t, F32)


def _nat_kernel(q_ref, k_ref, v_ref, bias_ref, y_ref, *, need_ctx):
    kc = k_ref[0:CTX, :]
    vc = v_ref[0:CTX, :]
    if need_ctx:
        s = _dot_nt(q_ref[0:CTX, :], kc)
        p = jnp.exp(s - jnp.max(s, axis=-1, keepdims=True))
        o = _dot(p.astype(BF16), vc) / jnp.sum(p, axis=-1, keepdims=True)
        y_ref[0:CTX, :] = o.astype(y_ref.dtype)
    else:
        y_ref[0:CTX, :] = jnp.zeros((CTX, NAT_DH), y_ref.dtype)
    n_keys = WIN_R * GRID_W

    def scores(r):
        rs = jnp.clip(r - WIN_R // 2, 0, GRID_H - WIN_R)
        q0 = pl.multiple_of(CTX + r * GRID_W, GRID_W)
        k0 = pl.multiple_of(CTX + rs * GRID_W, GRID_W)
        q = q_ref[pl.ds(q0, GRID_W), :]
        s_lat = _dot_nt(q, k_ref[pl.ds(k0, n_keys), :]) + bias_ref[r - rs]
        s_ctx = _dot_nt(q, kc)
        return q0, k0, s_lat, s_ctx

    def probs(q0, k0, s_lat, s_ctx):
        m = jnp.maximum(jnp.max(s_lat, axis=-1, keepdims=True), jnp.max(s_ctx, axis=-1, keepdims=True))
        p_lat = jnp.exp(s_lat - m)
        p_ctx = jnp.exp(s_ctx - m)
        denom = jnp.sum(p_lat, axis=-1, keepdims=True) + jnp.sum(p_ctx, axis=-1, keepdims=True)
        return q0, k0, p_lat.astype(BF16), p_ctx.astype(BF16), denom

    def values(q0, k0, p_lat, p_ctx, denom):
        o = _dot(p_lat, v_ref[pl.ds(k0, n_keys), :]) + _dot(p_ctx, vc)
        return q0, (o / denom).astype(y_ref.dtype)

    def body(it, carry):
        rows = [it * NAT_ROWS_PER_STEP + u for u in range(NAT_ROWS_PER_STEP)]
        outs = [values(*pr) for pr in [probs(*sc) for sc in [scores(r) for r in rows]]]
        for q0, o in outs:
            y_ref[pl.ds(q0, GRID_W), :] = o
        return carry

    lax.fori_loop(0, GRID_H // NAT_ROWS_PER_STEP, body, 0)


def nat_attention(qkv, bias_tbl, need_ctx):
    return pl.pallas_call(
        functools.partial(_nat_kernel, need_ctx=need_ctx),
        grid=(NAT_H, BATCH),
        in_specs=[
            pl.BlockSpec((ROWS, NAT_DH), lambda h, b: (b, h)),
            pl.BlockSpec((ROWS, NAT_DH), lambda h, b: (b, NAT_H + h)),
            pl.BlockSpec((ROWS, NAT_DH), lambda h, b: (b, 2 * NAT_H + h)),
            pl.BlockSpec((None, WIN_R, GRID_W, WIN_R * GRID_W), lambda h, b: (h, 0, 0, 0)),
        ],
        out_specs=pl.BlockSpec((ROWS, NAT_DH), lambda h, b: (b, h)),
        out_shape=jax.ShapeDtypeStruct((T, D), BF16),
        compiler_params=_params("arbitrary", "arbitrary"),
        name="nat_attention",
    )(qkv, qkv, qkv, bias_tbl)


def nat_bias_table(rpb):
    qc = np.arange(GRID_W)[:, None]
    kcol = np.arange(GRID_W)[None, :]
    start = np.clip(qc - WIN_C // 2, 0, GRID_W - WIN_C)
    in_win = (kcol >= start) & (kcol < start + WIN_C)
    dc = np.clip(kcol - qc + WIN_C - 1, 0, 2 * WIN_C - 2)
    pick = jnp.asarray(dc[None] == np.arange(2 * WIN_C - 1)[:, None, None], F32)
    by_col = jnp.einsum("hdc,cqk->hqdk", rpb, pick, precision=HIGHEST)
    by_col = jnp.where(in_win[None, :, None, :], by_col, -jnp.inf)
    tbl = jnp.stack([by_col[:, :, WIN_R - 1 - off:2 * WIN_R - 1 - off, :] for off in range(WIN_R)], axis=1)
    return tbl.reshape(NAT_H, WIN_R, GRID_W, WIN_R * GRID_W)


def _pair_slots(e0_ref, e1_ref, r0_ref, r1_ref, start_ref, t):
    return start_ref[e0_ref[t]] + r0_ref[t], start_ref[e1_ref[t]] + r1_ref[t]


def _is_routed_tile(i, latent_only):
    return (i % (ROWS // TG) >= CTX // TG) if latent_only else None


def _dispatch_kernel(e0_ref, e1_ref, r0_ref, r1_ref, start_ref, part_ref, h_ref, o_ref, zero_ref, sem, *,
                     latent_only):
    i = pl.program_id(0)
    base = i * TG

    @pl.when(i == 0)
    def _():
        zero_ref[...] = jnp.zeros_like(zero_ref)

        def zero_copy(p):
            return pltpu.make_async_copy(zero_ref, o_ref.at[pl.ds(pl.multiple_of(p * TM_PART, TM_PART), TM_PART)],
                                         sem.at[2])

        def start(p, carry):
            pl.when(part_ref[p] < TM_PART)(lambda: zero_copy(p).start())
            return carry

        def wait(p, carry):
            pl.when(part_ref[p] < TM_PART)(lambda: zero_copy(p).wait())
            return carry

        n_parts = o_ref.shape[0] // TM_PART
        lax.fori_loop(0, n_parts, start, 0)
        lax.fori_loop(0, n_parts, wait, 0)

    def start_rows(r, carry):
        row = h_ref.at[pl.ds(r, 1)]
        s0, s1 = _pair_slots(e0_ref, e1_ref, r0_ref, r1_ref, start_ref, base + r)
        pltpu.make_async_copy(row, o_ref.at[pl.ds(s0, 1)], sem.at[0]).start()
        pltpu.make_async_copy(row, o_ref.at[pl.ds(s1, 1)], sem.at[1]).start()
        return carry

    def run():
        lax.fori_loop(0, TG, start_rows, 0, unroll=DMA_UNROLL)
        for s in range(2):
            pltpu.make_async_copy(h_ref, o_ref.at[pl.ds(0, TG)], sem.at[s]).wait()

    routed = _is_routed_tile(i, latent_only)
    if routed is None:
        run()
    else:
        pl.when(routed)(run)


def moe_dispatch(routing, part_rows, h_words, n_slots, latent_only):
    words = h_words.shape[1]
    return pl.pallas_call(
        functools.partial(_dispatch_kernel, latent_only=latent_only),
        grid_spec=pltpu.PrefetchScalarGridSpec(
            num_scalar_prefetch=6,
            grid=(T // TG,),
            in_specs=[pl.BlockSpec((TG, words), lambda i, *_: (i, 0))],
            out_specs=pl.BlockSpec(memory_space=pl.ANY),
            scratch_shapes=[pltpu.VMEM((TM_PART, words), jnp.uint32), pltpu.SemaphoreType.DMA((3,))],
        ),
        out_shape=jax.ShapeDtypeStruct((n_slots, words), jnp.uint32),
        compiler_params=_params("arbitrary"),
        name="moe_dispatch",
    )(*routing, part_rows, h_words)


def _expert_changed(te_ref, i):
    return (i == 0) | (te_ref[i] != te_ref[jnp.maximum(i - 1, 0)])


def _on_occupied_prefix(n_rows, tile, o_ref, compute):
    for parts in range(tile // TM_PART + 1):
        p = parts * TM_PART

        @pl.when((n_rows > p - TM_PART) & (n_rows <= p))
        def _():
            if p > 0:
                o_ref[0:p, :] = compute(p)
            if p < tile:
                o_ref[p:tile, :] = jnp.zeros((tile - p, o_ref.shape[1]), o_ref.dtype)


def _moe_up_kernel(te_ref, tr_ref, ts_ref, to_ref, a_ref, wg_ref, wu_ref, o_ref, wgb_ref, wub_ref):
    del ts_ref, to_ref
    i = pl.program_id(1)

    @pl.when(_expert_changed(te_ref, i))
    def _():
        wgb_ref[...] = wg_ref[...].astype(BF16)
        wub_ref[...] = wu_ref[...].astype(BF16)

    half = D // 2

    def compute(p):
        lo, hi = _unpack_words(a_ref[0:p, :])
        gate = _dot(lo, wgb_ref[0:half, :]) + _dot(hi, wgb_ref[half:D, :])
        up = _dot(lo, wub_ref[0:half, :]) + _dot(hi, wub_ref[half:D, :])
        return (_silu(gate) * up).astype(o_ref.dtype)

    _on_occupied_prefix(tr_ref[i], TM_UP, o_ref, compute)


def moe_up(tiles, xw, wg, wu, j):
    n_slots = xw.shape[0]
    w_spec = pl.BlockSpec((None, None, D, TF_MOE), lambda f, i, te, tr, ts, to: (j, te[i], 0, f))
    return pl.pallas_call(
        _moe_up_kernel,
        grid_spec=pltpu.PrefetchScalarGridSpec(
            num_scalar_prefetch=4,
            grid=(D_FF // TF_MOE, n_slots // TM_UP),
            in_specs=[pl.BlockSpec((TM_UP, D // 2), lambda f, i, te, tr, ts, to: (ts[i], 0)), w_spec, w_spec],
            out_specs=pl.BlockSpec((TM_UP, TF_MOE), lambda f, i, te, tr, ts, to: (to[i], f)),
            scratch_shapes=[pltpu.VMEM((D, TF_MOE), BF16), pltpu.VMEM((D, TF_MOE), BF16)],
        ),
        out_shape=jax.ShapeDtypeStruct((n_slots, D_FF), BF16),
        compiler_params=_params("arbitrary", "arbitrary"),
        name="moe_up",
    )(*tiles, xw, wg, wu)


def _moe_down_kernel(te_ref, tr_ref, ts_ref, to_ref, a_ref, w_ref, o_ref, wb_ref):
    del ts_ref, to_ref
    i = pl.program_id(1)

    @pl.when(_expert_changed(te_ref, i))
    def _():
        wb_ref[...] = w_ref[...].astype(BF16)

    _on_occupied_prefix(tr_ref[i], TM_DOWN, o_ref, lambda p: _dot(a_ref[0:p, :], wb_ref[...]))


def moe_down(tiles, act, wd, j):
    n_slots = act.shape[0]
    tn = TN_MOE_DOWN
    return pl.pallas_call(
        _moe_down_kernel,
        grid_spec=pltpu.PrefetchScalarGridSpec(
            num_scalar_prefetch=4,
            grid=(D // tn, n_slots // TM_DOWN),
            in_specs=[
                pl.BlockSpec((TM_DOWN, D_FF), lambda n, i, te, tr, ts, to: (ts[i], 0)),
                pl.BlockSpec((None, None, D_FF, tn), lambda n, i, te, tr, ts, to: (j, te[i], 0, n),
                             pipeline_mode=pl.Buffered(1)),
            ],
            out_specs=pl.BlockSpec((TM_DOWN, tn), lambda n, i, te, tr, ts, to: (to[i], n)),
            scratch_shapes=[pltpu.VMEM((D_FF, tn), BF16)],
        ),
        out_shape=jax.ShapeDtypeStruct((n_slots, D), F32),
        compiler_params=_params("arbitrary", "arbitrary"),
        name="moe_down",
    )(*tiles, act, wd)


def _combine_kernel(e0_ref, e1_ref, r0_ref, r1_ref, start_ref, ys_ref, x_ref, gt_ref, w_ref, fg_ref, o_ref,
                    b0_ref, b1_ref, sem, *, latent_only, final):
    i = pl.program_id(0)
    base = i * TG

    def start_rows(r, carry):
        s0, s1 = _pair_slots(e0_ref, e1_ref, r0_ref, r1_ref, start_ref, base + r)
        pltpu.make_async_copy(ys_ref.at[pl.ds(s0, 1)], b0_ref.at[pl.ds(r, 1)], sem.at[0]).start()
        pltpu.make_async_copy(ys_ref.at[pl.ds(s1, 1)], b1_ref.at[pl.ds(r, 1)], sem.at[1]).start()
        return carry

    def run():
        lax.fori_loop(0, TG, start_rows, 0, unroll=DMA_UNROLL)
        pltpu.make_async_copy(ys_ref.at[pl.ds(0, TG)], b0_ref, sem.at[0]).wait()
        pltpu.make_async_copy(ys_ref.at[pl.ds(0, TG)], b1_ref, sem.at[1]).wait()
        mix = w_ref[:, 0:1] * b0_ref[...] + w_ref[:, 1:2] * b1_ref[...]
        out = x_ref[...] + _row_gate(gt_ref, i, TG) * mix
        if final:
            out = out * lax.rsqrt(jnp.mean(out * out, axis=-1, keepdims=True) + EPS) * fg_ref[...]
        o_ref[...] = out

    routed = _is_routed_tile(i, latent_only)
    if routed is None:
        run()
    else:
        pl.when(routed)(run)
        if not final:
            @pl.when(jnp.logical_not(routed))
            def _():
                o_ref[...] = x_ref[...]


def moe_combine(routing, ys, x, mods, layer, gate_col, wcol, latent_only, final_g=None):
    final = final_g is not None
    assert latent_only or not final
    per_batch, n_ctx, n_lat = ROWS // TG, CTX // TG, SEQ // TG
    if final:
        out_rows = BATCH * SEQ
        out_map = lambda i, *_: ((i // per_batch) * n_lat + jnp.maximum(i % per_batch - n_ctx, 0), 0)
    else:
        out_rows = T
        out_map = lambda i, *_: (i, 0)
        final_g = jnp.ones((D,), F32)
    return pl.pallas_call(
        functools.partial(_combine_kernel, latent_only=latent_only, final=final),
        grid_spec=pltpu.PrefetchScalarGridSpec(
            num_scalar_prefetch=5,
            grid=(T // TG,),
            in_specs=[
                pl.BlockSpec(memory_space=pl.ANY),
                pl.BlockSpec((TG, D), lambda i, *_: (i, 0)),
                pl.BlockSpec((None, N_MOD, D), lambda i, *_: (layer, 0, gate_col)),
                pl.BlockSpec((TG, LANES), lambda i, *_: (i, 0)),
                pl.BlockSpec((1, D), lambda i, *_: (0, 0)),
            ],
            out_specs=pl.BlockSpec((TG, D), out_map),
            scratch_shapes=[pltpu.VMEM((TG, D), F32), pltpu.VMEM((TG, D), F32), pltpu.SemaphoreType.DMA((2,))],
        ),
        out_shape=jax.ShapeDtypeStruct((out_rows, D), F32),
        compiler_params=_params("arbitrary"),
        name="moe_combine",
    )(*routing, ys, x, mods, wcol, final_g.reshape(1, D))


def moe_slots(n_tokens):
    return (2 * n_tokens // TM_UP + N_EXP) * TM_UP


def moe_plan(counts, n_slots):
    group = (counts + TM_UP - 1) // TM_UP * TM_UP
    start = jnp.cumsum(group) - group
    end = start + counts
    last_row = jnp.max(jnp.where(counts > 0, end, 0)) - 1

    def of_expert(table, expert):
        return jnp.sum(jnp.where(expert[:, None] == jnp.arange(N_EXP), table, 0), axis=1)

    def occupied_rows(tm):
        ids = jnp.arange(n_slots // tm)
        expert = jnp.sum((start + group)[None, :] <= jnp.minimum(ids * tm, last_row)[:, None], axis=1)
        return ids, expert, jnp.clip(of_expert(end, expert) - ids * tm, 0, tm)

    def tiles(tm):
        ids, expert, rows = occupied_rows(tm)
        first = of_expert(start, expert) // tm
        n_occ = (of_expert(counts, expert) + tm - 1) // tm
        k = ids - first
        tile = jnp.where(k < n_occ, first + (k - 1) % jnp.maximum(n_occ, 1), ids)
        rows_v = jnp.clip(of_expert(end, expert) - tile * tm, 0, tm)
        fetched = jnp.where(rows_v > 0, ids, 0)
        last_fetch = jnp.max(jnp.where(ids[None, :] <= ids[:, None], fetched[None, :], 0), axis=1)
        block = jnp.sum(jnp.where(ids[None, :] == last_fetch[:, None], tile[None, :], 0), axis=1)
        return tuple(a.astype(jnp.int32) for a in (expert, rows_v, block, tile))

    return start.astype(jnp.int32), tiles(TM_UP), tiles(TM_DOWN), occupied_rows(TM_PART)[2].astype(jnp.int32)


def _pad_cols(w, n):
    return jnp.pad(w, ((0, 0), (0, n - w.shape[1])))


def kernel(x, c, ctx, c_ctx, ada_w, ada_b, norm_mix_g, norm_ffn_g, gla_w_in, gla_wg_fwd, gla_bg_fwd, gla_wg_bwd, gla_bg_bwd, gla_norm_g, gla_w_out, nat_w_in, nat_rpb, nat_w_out, ffn_w_gate, ffn_w_up, ffn_w_down, moe_router, moe_w_gate, moe_w_up, moe_w_down, final_norm_g):
    c8 = jnp.concatenate([c, c_ctx[None, :], jnp.zeros((N_MOD - BATCH - 1, D), F32)], axis=0)
    mods = ada_tables(c8, ada_w, ada_b)
    cos_t, sin_t = rope_tables()
    router_pad = jnp.pad(moe_router, ((0, 0), (0, 0), (0, LANES - N_EXP)))

    for i in range(DEPTH):
        j = i // 2
        last = i == DEPTH - 1
        if i == 0:
            xs, h = embed_modulate(x, ctx, norm_mix_g, mods, i, 0, 1)
        else:
            h = modulate(xs, norm_mix_g, mods, i, 0, 1)
        if i % 2 == 0:
            qk = mm_rope(h, gla_w_in, j, cos_t, sin_t, tm=1152)
            vr = mm_plain(h, gla_w_in, j, 2 * GLA_QK, 2 * D, BF16, tm=1152, tn=1024)
            z0 = 2 * GLA_QK + 2 * D
            wz_pad = _pad_cols(gla_w_in[j, :, z0:z0 + 2 * GLA_RANK], LANES)
            wf_pad = jnp.pad(gla_wg_fwd[j], ((0, LANES - GLA_RANK), (0, 0)))
            wb_pad = jnp.pad(gla_wg_bwd[j], ((GLA_RANK, LANES - 2 * GLA_RANK), (0, 0)))
            gf, gb = gla_gates(h, wz_pad, wf_pad, gla_bg_fwd, wb_pad, gla_bg_bwd, j, tm=1152)
            y = gla_scan(qk, vr, gf, gb, gla_norm_g, j)
            xs = mm_resid(y, gla_w_out, j, xs, mods, i, 2, tm=1152, tn=1024)
        else:
            qkv = mm_plain(h, nat_w_in, j, 0, 3 * D, BF16, tm=1152, tn=1024,
                           scale=NAT_DH ** -0.5, n_scaled=D // 1024)
            y = nat_attention(qkv, nat_bias_table(nat_rpb[j]), not last)
            xs = mm_resid(y, nat_w_out, j, xs, mods, i, 2, tm=1152, tn=1024)
        if i % 2 == 0:
            h2 = modulate(xs, norm_ffn_g, mods, i, 3, 4)
            act = mm_swiglu(h2, ffn_w_gate, ffn_w_up, j, tm=1152, tn=512)
            xs = mm_resid(act, ffn_w_down, j, xs, mods, i, 5, tm=576, tn=512)
        else:
            h_words, idx, rank, wcol, counts = modulate_route(xs, norm_ffn_g, mods, i, 3, 4, router_pad, j, last)
            n_slots = moe_slots(BATCH * SEQ if last else T)
            start, up_tiles, down_tiles, part_rows = moe_plan(counts[:, 0], n_slots)
            routing = (idx[0], idx[1], rank[0], rank[1], start)
            xw = moe_dispatch(routing, part_rows, h_words, n_slots, last)
            act = moe_up(up_tiles, xw, moe_w_gate, moe_w_up, j)
            ys = moe_down(down_tiles, act, moe_w_down, j)
            xs = moe_combine(routing, ys, xs, mods, i, 5, wcol, last, final_norm_g if last else None)
    return xs.reshape(BATCH, SEQ, D)
```

```python
import functools

import numpy as np
import jax
import jax.numpy as jnp
from jax import lax
from jax.experimental import pallas as pl
from jax.experimental.pallas import tpu as pltpu

F32 = jnp.float32
BF16 = jnp.bfloat16
HIGHEST = lax.Precision.HIGHEST

D = 2048
BATCH = 4
SEQ = 2048
CTX = 256
ROWS = CTX + SEQ
T = BATCH * ROWS
DEPTH = 4
GRID_W = 64
GRID_H = SEQ // GRID_W
EPS = 1e-6
ROPE_BASE = 10000.0
CTX_GROUP = BATCH
N_MOD = 8

GLA_H = 4
GLA_DK = 256
GLA_DV = 512
GLA_QK = GLA_H * GLA_DK
GLA_RANK = 16
GLA_TAU = 16.0
GLA_C = 64
GLA_SPLIT_DECAY_MAX = 60.0

NAT_H = 16
NAT_DH = 128
WIN_R = 8
WIN_C = 16
NAT_ROWS_PER_STEP = 16

D_FF = 5632
N_EXP = 8

LANES = 128
VMEM_LIMIT = 56 * 1024 * 1024

TM_MOD = 256
TM_MODULATE = 768
TM_UP = 1024
TM_DOWN = 512
TM_PART = 128
TF_MOE = 512
TN_MOE_DOWN = 1024
TG = 256
DMA_UNROLL = 16


def _params(*sem):
    return pltpu.CompilerParams(dimension_semantics=sem, vmem_limit_bytes=VMEM_LIMIT)


def _dot(a, b):
    return jnp.dot(a, b, preferred_element_type=F32)


def _dot_nt(a, b):
    return lax.dot_general(a, b, (((1,), (1,)), ((), ())), preferred_element_type=F32)


def _dot_tn(a, b):
    return lax.dot_general(a, b, (((0,), (0,)), ((), ())), preferred_element_type=F32)


def _silu(v):
    return v * jax.nn.sigmoid(v)


def _row_gate(tab_ref, m, tm):
    start = m * tm
    b = start // ROWS
    rows = start % ROWS + lax.broadcasted_iota(jnp.int32, (tm, 1), 0)
    per_batch = tab_ref[pl.ds(b, 1), :]
    per_ctx = tab_ref[CTX_GROUP:CTX_GROUP + 1, :]
    return jnp.where(rows < CTX, per_ctx, per_batch)


def _ada_kernel(c_ref, w_ref, b_ref, o_ref):
    a = _silu(c_ref[...]).astype(BF16)
    o_ref[...] = _dot(a, w_ref[...].astype(BF16)) + b_ref[...]


def ada_tables(c8, ada_w, ada_b):
    tn = 1024
    return pl.pallas_call(
        _ada_kernel,
        grid=(DEPTH, 6 * D // tn),
        in_specs=[
            pl.BlockSpec((N_MOD, D), lambda l, n: (0, 0)),
            pl.BlockSpec((None, D, tn), lambda l, n: (l, 0, n)),
            pl.BlockSpec((None, 1, tn), lambda l, n: (l, 0, n)),
        ],
        out_specs=pl.BlockSpec((None, N_MOD, tn), lambda l, n: (l, 0, n)),
        out_shape=jax.ShapeDtypeStruct((DEPTH, N_MOD, 6 * D), F32),
        compiler_params=_params("arbitrary", "arbitrary"),
        name="ada_tables",
    )(c8, ada_w, ada_b.reshape(DEPTH, 1, 6 * D))


def _modulate_rows(x, g_ref, sh_ref, sc_ref):
    tm = x.shape[0]
    y = x * lax.rsqrt(jnp.mean(x * x, axis=-1, keepdims=True) + EPS) * g_ref[...]
    i = pl.program_id(0)
    return y * (1.0 + _row_gate(sc_ref, i, tm)) + _row_gate(sh_ref, i, tm)


def _modulated(x_ref, g_ref, sh_ref, sc_ref):
    return _modulate_rows(x_ref[...], g_ref, sh_ref, sc_ref)


def _embed_modulate_kernel(x_ref, c_ref, g_ref, sh_ref, sc_ref, xs_ref, h_ref):
    is_ctx = pl.program_id(0) % (ROWS // TM_MOD) < CTX // TM_MOD
    rows = jnp.where(is_ctx, c_ref[...], x_ref[...])
    xs_ref[...] = rows
    h_ref[...] = _modulate_rows(rows, g_ref, sh_ref, sc_ref).astype(h_ref.dtype)


def embed_modulate(x, ctx, norm_g, mods, layer, sh_col, sc_col):
    per_batch = ROWS // TM_MOD
    n_ctx = CTX // TM_MOD
    n_lat = SEQ // TM_MOD
    specs = _mod_specs(layer, sh_col, sc_col, TM_MOD)
    return pl.pallas_call(
        _embed_modulate_kernel,
        grid=(T // TM_MOD,),
        in_specs=[
            pl.BlockSpec((TM_MOD, D), lambda i: ((i // per_batch) * n_lat + jnp.maximum(i % per_batch - n_ctx, 0), 0)),
            pl.BlockSpec((TM_MOD, D), lambda i: ((i // per_batch) * n_ctx + jnp.minimum(i % per_batch, n_ctx - 1), 0)),
        ] + specs[1:],
        out_specs=[pl.BlockSpec((TM_MOD, D), lambda i: (i, 0)), pl.BlockSpec((TM_MOD, D), lambda i: (i, 0))],
        out_shape=[jax.ShapeDtypeStruct((T, D), F32), jax.ShapeDtypeStruct((T, D), BF16)],
        compiler_params=_params("arbitrary"),
        name="embed_modulate",
    )(x.reshape(BATCH * SEQ, D), ctx.reshape(BATCH * CTX, D), norm_g.reshape(DEPTH, 1, D), mods, mods)


def _modulate_kernel(x_ref, g_ref, sh_ref, sc_ref, o_ref):
    o_ref[...] = _modulated(x_ref, g_ref, sh_ref, sc_ref).astype(o_ref.dtype)


def _mod_specs(layer, sh_col, sc_col, tm):
    return [
        pl.BlockSpec((tm, D), lambda i: (i, 0)),
        pl.BlockSpec((None, 1, D), lambda i: (layer, 0, 0)),
        pl.BlockSpec((None, N_MOD, D), lambda i: (layer, 0, sh_col)),
        pl.BlockSpec((None, N_MOD, D), lambda i: (layer, 0, sc_col)),
    ]


def modulate(x, norm_g, mods, layer, sh_col, sc_col):
    tm = TM_MODULATE
    return pl.pallas_call(
        _modulate_kernel,
        grid=(T // tm,),
        in_specs=_mod_specs(layer, sh_col, sc_col, tm),
        out_specs=pl.BlockSpec((tm, D), lambda i: (i, 0)),
        out_shape=jax.ShapeDtypeStruct((T, D), BF16),
        compiler_params=_params("arbitrary"),
        name="modulate",
    )(x, norm_g.reshape(DEPTH, 1, D), mods, mods)


def _pack_words(h):
    half = h.shape[1] // 2
    lo = lax.bitcast_convert_type(h[:, :half].astype(BF16).astype(F32), jnp.uint32)
    hi = lax.bitcast_convert_type(h[:, half:].astype(BF16).astype(F32), jnp.uint32)
    return (lo >> 16) | (hi & jnp.uint32(0xFFFF0000))


def _unpack_words(w):
    lo = lax.bitcast_convert_type(w << 16, F32).astype(BF16)
    hi = lax.bitcast_convert_type(w & jnp.uint32(0xFFFF0000), F32).astype(BF16)
    return lo, hi


def _modulate_route_kernel(x_ref, g_ref, sh_ref, sc_ref, rt_ref, hw_ref, idx_ref, rank_ref, wcol_ref, cnt_ref,
                           base_ref, *, latent_only):
    i = pl.program_id(0)

    @pl.when(i == 0)
    def _():
        base_ref[...] = jnp.zeros_like(base_ref)

    h = _modulated(x_ref, g_ref, sh_ref, sc_ref)
    hw_ref[...] = _pack_words(h)
    logits = jnp.dot(h, rt_ref[...], precision=HIGHEST, preferred_element_type=F32)

    lane = lax.broadcasted_iota(jnp.int32, logits.shape, 1)
    lm = jnp.where(lane < N_EXP, logits, -jnp.inf)
    c1 = jnp.max(lm, axis=1, keepdims=True)
    j1 = jnp.min(jnp.where(lm == c1, lane, LANES), axis=1, keepdims=True)
    c2 = jnp.max(jnp.where(lane == j1, -jnp.inf, lm), axis=1, keepdims=True)
    e = jnp.exp(c2 - c1)
    wcol_ref[...] = jnp.where(lane == 0, 1.0 / (1.0 + e), jnp.where(lane == 1, e / (1.0 + e), 0.0))

    lt = logits.T[:N_EXP]
    eid = lax.broadcasted_iota(jnp.int32, lt.shape, 0)
    m1 = jnp.max(lt, axis=0, keepdims=True)
    i1 = jnp.min(jnp.where(lt == m1, eid, N_EXP), axis=0, keepdims=True)
    lt2 = jnp.where(eid == i1, -jnp.inf, lt)
    m2 = jnp.max(lt2, axis=0, keepdims=True)
    i2 = jnp.min(jnp.where(lt2 == m2, eid, N_EXP), axis=0, keepdims=True)
    idx_ref[...] = jnp.concatenate([i1, i2], axis=0)

    routed = 1.0
    if latent_only:
        routed = jnp.where(i % (ROWS // TM_MOD) < CTX // TM_MOD, 0.0, 1.0)
    oh1 = jnp.where(eid == i1, routed, 0.0)
    oh2 = jnp.where(eid == i2, routed, 0.0)
    both = oh1 + oh2
    tj = lax.broadcasted_iota(jnp.int32, (TM_MOD, TM_MOD), 0)
    tt = lax.broadcasted_iota(jnp.int32, (TM_MOD, TM_MOD), 1)
    before = _dot(both.astype(BF16), jnp.where(tj < tt, 1.0, 0.0).astype(BF16))
    seen = base_ref[:, 0:1] + before
    r1 = jnp.sum(oh1 * seen, axis=0, keepdims=True)
    r2 = jnp.sum(oh2 * seen, axis=0, keepdims=True)
    rank_ref[...] = jnp.concatenate([r1, r2], axis=0).astype(jnp.int32)
    total = base_ref[...] + jnp.sum(both, axis=1, keepdims=True)
    base_ref[...] = total
    cnt_ref[...] = total.astype(jnp.int32)


def modulate_route(x, norm_g, mods, layer, sh_col, sc_col, router_pad, j, latent_only):
    pairs = pl.BlockSpec((2, TM_MOD), lambda i: (0, i))
    return pl.pallas_call(
        functools.partial(_modulate_route_kernel, latent_only=latent_only),
        grid=(T // TM_MOD,),
        in_specs=_mod_specs(layer, sh_col, sc_col, TM_MOD) + [
            pl.BlockSpec((None, D, LANES), lambda i: (j, 0, 0)),
        ],
        out_specs=[
            pl.BlockSpec((TM_MOD, D // 2), lambda i: (i, 0)),
            pairs, pairs,
            pl.BlockSpec((TM_MOD, LANES), lambda i: (i, 0)),
            pl.BlockSpec((N_EXP, LANES), lambda i: (0, 0)),
        ],
        out_shape=[
            jax.ShapeDtypeStruct((T, D // 2), jnp.uint32),
            jax.ShapeDtypeStruct((2, T), jnp.int32),
            jax.ShapeDtypeStruct((2, T), jnp.int32),
            jax.ShapeDtypeStruct((T, LANES), F32),
            jax.ShapeDtypeStruct((N_EXP, LANES), jnp.int32),
        ],
        scratch_shapes=[pltpu.VMEM((N_EXP, LANES), F32)],
        compiler_params=_params("arbitrary"),
        name="modulate_route",
    )(x, norm_g.reshape(DEPTH, 1, D), mods, mods, router_pad)


def _cast_weight(w_ref, wb_ref):
    @pl.when(pl.program_id(1) == 0)
    def _():
        wb_ref[...] = w_ref[...].astype(BF16)


def _mm_plain_kernel(a_ref, w_ref, o_ref, wb_ref, *, scale, n_scaled):
    _cast_weight(w_ref, wb_ref)
    acc = _dot(a_ref[...], wb_ref[...])
    if n_scaled:
        acc = acc * jnp.where(pl.program_id(0) < n_scaled, scale, 1.0)
    o_ref[...] = acc.astype(o_ref.dtype)


def mm_plain(a, w, layer, col0, n_out, out_dtype, tm, tn, scale=1.0, n_scaled=0):
    k = a.shape[1]
    c0 = col0 // tn
    return pl.pallas_call(
        functools.partial(_mm_plain_kernel, scale=scale, n_scaled=n_scaled),
        grid=(n_out // tn, T // tm),
        in_specs=[
            pl.BlockSpec((tm, k), lambda n, m: (m, 0)),
            pl.BlockSpec((None, k, tn), lambda n, m: (layer, 0, n + c0)),
        ],
        out_specs=pl.BlockSpec((tm, tn), lambda n, m: (m, n)),
        out_shape=jax.ShapeDtypeStruct((T, n_out), out_dtype),
        scratch_shapes=[pltpu.VMEM((k, tn), BF16)],
        compiler_params=_params("arbitrary", "arbitrary"),
        name="mm_plain",
    )(a, w)


def _mm_rope_kernel(a_ref, w_ref, cos_ref, sin_ref, o_ref, wb_ref, *, tn):
    _cast_weight(w_ref, wb_ref)
    acc = _dot(a_ref[...], wb_ref[...])
    scale = jnp.where(pl.program_id(0) == 0, GLA_DK ** -0.5, 1.0)
    for s in range(tn // LANES):
        xs = acc[:, s * LANES:(s + 1) * LANES]
        t = (s % 2) * LANES
        rot = xs * cos_ref[:, t:t + LANES] + pltpu.roll(xs, LANES // 2, 1) * sin_ref[:, t:t + LANES]
        o_ref[:, s * LANES:(s + 1) * LANES] = rot * scale


def mm_rope(a, w, layer, cos_t, sin_t, tm):
    tn = GLA_QK
    per_batch = ROWS // tm
    return pl.pallas_call(
        functools.partial(_mm_rope_kernel, tn=tn),
        grid=(2, T // tm),
        in_specs=[
            pl.BlockSpec((tm, D), lambda n, m: (m, 0)),
            pl.BlockSpec((None, D, tn), lambda n, m: (layer, 0, n)),
            pl.BlockSpec((tm, GLA_DK), lambda n, m: (m % per_batch, 0)),
            pl.BlockSpec((tm, GLA_DK), lambda n, m: (m % per_batch, 0)),
        ],
        out_specs=pl.BlockSpec((tm, tn), lambda n, m: (m, n)),
        out_shape=jax.ShapeDtypeStruct((T, 2 * GLA_QK), F32),
        scratch_shapes=[pltpu.VMEM((D, tn), BF16)],
        compiler_params=_params("arbitrary", "arbitrary"),
        name="mm_rope",
    )(a, w, cos_t, sin_t)


def _mm_resid_kernel(a_ref, w_ref, x_ref, gt_ref, o_ref, wb_ref, *, tm):
    _cast_weight(w_ref, wb_ref)
    acc = _dot(a_ref[...], wb_ref[...])
    o_ref[...] = x_ref[...] + _row_gate(gt_ref, pl.program_id(1), tm) * acc


def mm_resid(a, w, layer_w, x, mods, layer, gate_col, tm, tn):
    k = a.shape[1]
    g0 = gate_col * D // tn
    return pl.pallas_call(
        functools.partial(_mm_resid_kernel, tm=tm),
        grid=(D // tn, T // tm),
        in_specs=[
            pl.BlockSpec((tm, k), lambda n, m: (m, 0)),
            pl.BlockSpec((None, k, tn), lambda n, m: (layer_w, 0, n)),
            pl.BlockSpec((tm, tn), lambda n, m: (m, n)),
            pl.BlockSpec((None, N_MOD, tn), lambda n, m: (layer, 0, g0 + n)),
        ],
        out_specs=pl.BlockSpec((tm, tn), lambda n, m: (m, n)),
        out_shape=jax.ShapeDtypeStruct((T, D), F32),
        scratch_shapes=[pltpu.VMEM((k, tn), BF16)],
        compiler_params=_params("arbitrary", "arbitrary"),
        name="mm_resid",
    )(a, w, x, mods)


def _mm_swiglu_kernel(a_ref, wg_ref, wu_ref, o_ref, wgb_ref, wub_ref):
    _cast_weight(wg_ref, wgb_ref)
    _cast_weight(wu_ref, wub_ref)
    a = a_ref[...]
    o_ref[...] = (_silu(_dot(a, wgb_ref[...])) * _dot(a, wub_ref[...])).astype(o_ref.dtype)


def mm_swiglu(a, wg, wu, layer, tm, tn):
    return pl.pallas_call(
        _mm_swiglu_kernel,
        grid=(D_FF // tn, T // tm),
        in_specs=[
            pl.BlockSpec((tm, D), lambda n, m: (m, 0)),
            pl.BlockSpec((None, D, tn), lambda n, m: (layer, 0, n)),
            pl.BlockSpec((None, D, tn), lambda n, m: (layer, 0, n)),
        ],
        out_specs=pl.BlockSpec((tm, tn), lambda n, m: (m, n)),
        out_shape=jax.ShapeDtypeStruct((T, D_FF), BF16),
        scratch_shapes=[pltpu.VMEM((D, tn), BF16), pltpu.VMEM((D, tn), BF16)],
        compiler_params=_params("arbitrary", "arbitrary"),
        name="mm_swiglu",
    )(a, wg, wu)


def _log_sigmoid(v):
    return jnp.minimum(v, 0.0) - jnp.log(1.0 + jnp.exp(-jnp.abs(v)))


def _gla_gate_kernel(h_ref, wz_ref, wf_ref, bf_ref, wb_ref, bb_ref, gf_ref, gb_ref):
    z = _dot(h_ref[...], wz_ref[...].astype(BF16))
    af = jnp.dot(z, wf_ref[...], precision=HIGHEST, preferred_element_type=F32) + bf_ref[...]
    ab = jnp.dot(z, wb_ref[...], precision=HIGHEST, preferred_element_type=F32) + bb_ref[...]
    gf = _log_sigmoid(af) / GLA_TAU
    gb = _log_sigmoid(ab) / GLA_TAU
    c = GLA_C
    ri = lax.broadcasted_iota(jnp.int32, (c, c), 0)
    ci = lax.broadcasted_iota(jnp.int32, (c, c), 1)
    lower = (ci <= ri).astype(BF16)
    upper = (ci >= ri).astype(BF16)

    def terms(v):
        hi = v.astype(BF16)
        rest = v - hi.astype(F32)
        mid = rest.astype(BF16)
        return hi, mid, (rest - mid.astype(F32)).astype(BF16)

    gf3, gb3 = terms(gf), terms(gb)
    for ch in range(h_ref.shape[0] // c):
        rows = slice(ch * c, (ch + 1) * c)
        gf_ref[rows, :] = _dot(lower, gf3[0][rows]) + _dot(lower, gf3[1][rows]) + _dot(lower, gf3[2][rows])
        gb_ref[rows, :] = _dot(upper, gb3[0][rows]) + _dot(upper, gb3[1][rows]) + _dot(upper, gb3[2][rows])


def gla_gates(h, wz_pad, wf_pad, bg_f, wb_pad, bg_b, j, tm):
    row = lambda i: (i, 0)
    return pl.pallas_call(
        _gla_gate_kernel,
        grid=(T // tm,),
        in_specs=[
            pl.BlockSpec((tm, D), row),
            pl.BlockSpec((D, LANES), lambda i: (0, 0)),
            pl.BlockSpec((LANES, GLA_QK), lambda i: (0, 0)),
            pl.BlockSpec((None, 1, GLA_QK), lambda i: (j, 0, 0)),
            pl.BlockSpec((LANES, GLA_QK), lambda i: (0, 0)),
            pl.BlockSpec((None, 1, GLA_QK), lambda i: (j, 0, 0)),
        ],
        out_specs=[pl.BlockSpec((tm, GLA_QK), row), pl.BlockSpec((tm, GLA_QK), row)],
        out_shape=[jax.ShapeDtypeStruct((T, GLA_QK), F32)] * 2,
        compiler_params=_params("arbitrary"),
        name="gla_gates",
    )(h, wz_pad, wf_pad, bg_f.reshape(-1, 1, GLA_QK), wb_pad, bg_b.reshape(-1, 1, GLA_QK))


def _gla_intra_direct(q, k, bc, rev):
    c = GLA_C
    lane8 = lax.broadcasted_iota(jnp.int32, (8, c), 1)
    srow8 = lax.broadcasted_iota(jnp.int32, (8, 1), 0)
    blocks = []
    for r0 in range(0, c, 8):
        q_r, b_r = q[r0:r0 + 8], bc[r0:r0 + 8]
        a_r = jnp.zeros((8, c), F32)
        for jj in (range(r0, c) if rev else range(r0 + 8)):
            t = q_r * k[jj:jj + 1] * jnp.exp(b_r - bc[jj:jj + 1])
            s = jnp.sum(t, axis=-1, keepdims=True)
            seen = (srow8 + r0 <= jj) if rev else (srow8 + r0 >= jj)
            a_r = jnp.where(lane8 == jj, jnp.where(seen, s, 0.0), a_r)
        blocks.append(a_r)
    return jnp.concatenate(blocks, axis=0).astype(BF16)


def _gla_chunks(streams):
    c = GLA_C
    ri = lax.broadcasted_iota(jnp.int32, (c, c), 0)
    ci = lax.broadcasted_iota(jnp.int32, (c, c), 1)
    bcs = [b for (q, k, v, b, st, rev) in streams]

    partial = []
    for (q, k, v, g, st, rev), bc in zip(streams, bcs):
        b_end = bc[0:1] if rev else bc[c - 1:c]
        qe = (q * jnp.exp(bc)).astype(BF16)
        st_new = st * jnp.exp(b_end) + _dot_tn(v, (k * jnp.exp(b_end - bc)).astype(BF16))
        partial.append((st.astype(BF16), st_new, qe, b_end))

    scores = []
    for (q, k, v, g, st, rev), bc, (_, _, qe, _) in zip(streams, bcs, partial):
        a = _dot_nt(qe, (k * jnp.exp(jnp.minimum(-bc, GLA_SPLIT_DECAY_MAX))).astype(BF16))
        scores.append(jnp.where(ci >= ri if rev else ci <= ri, a, 0.0).astype(BF16))

    out = []
    for (q, k, v, g, st, rev), (st_b, st_new, qe, _), a in zip(streams, partial, scores):
        o_state = _dot_nt(qe, st_b)
        out.append((o_state + _dot(a, v), st_new, o_state))
    total_decay = functools.reduce(jnp.maximum, [jnp.max(-b_end) for (_, _, _, b_end) in partial])
    return out, total_decay


def _gla_direct_output(q, k, v, bc, o_state, rev):
    a = _gla_intra_direct(q, k, bc, rev).astype(F32)
    vf = v.astype(F32)
    o = o_state
    for jj in range(GLA_C):
        o = o + a[:, jj:jj + 1] * vf[jj:jj + 1, :]
    return o


def _gla_kernel(q_ref, k_ref, v_ref, r_ref, gf_ref, gb_ref, ng_ref, y_ref, stf_ref, stb_ref, of_ref, ob_ref):
    nch = ROWS // GLA_C
    nctx = CTX // GLA_C

    def chunk_rows(ch):
        return pl.ds(pl.multiple_of(ch * GLA_C, GLA_C), GLA_C)

    def stream(rows, g_ref, st_ref, rev):
        return q_ref[rows, :], k_ref[rows, :], v_ref[rows, :], g_ref[rows, :], st_ref[...], rev

    stf_ref[...] = jnp.zeros_like(stf_ref)
    stb_ref[...] = jnp.zeros_like(stb_ref)

    def scan(s, carry):
        rows_f = chunk_rows(s)
        rows_b = chunk_rows(jnp.where(s < nctx, nctx - 1 - s, nch + nctx - 1 - s))
        s_f, s_b = stream(rows_f, gf_ref, stf_ref, False), stream(rows_b, gb_ref, stb_ref, True)
        ((o_f, st_f, os_f), (o_b, st_b, os_b)), total_decay = _gla_chunks([s_f, s_b])
        of_ref[rows_f, :] = o_f
        ob_ref[rows_b, :] = o_b
        stf_ref[...] = st_f
        stb_ref[...] = st_b

        @pl.when(total_decay > GLA_SPLIT_DECAY_MAX)
        def _():
            of_ref[rows_f, :] = _gla_direct_output(s_f[0], s_f[1], s_f[2], s_f[3], os_f, False)
            ob_ref[rows_b, :] = _gla_direct_output(s_b[0], s_b[1], s_b[2], s_b[3], os_b, True)

        return carry

    lax.fori_loop(0, nch, scan, 0)

    def readout(ch, carry):
        rows = chunk_rows(ch)
        o = of_ref[rows, :] + ob_ref[rows, :]
        on = o * lax.rsqrt(jnp.mean(o * o, axis=-1, keepdims=True) + EPS) * ng_ref[...]
        y_ref[rows, :] = (on * _silu(r_ref[rows, :].astype(F32))).astype(y_ref.dtype)
        return carry

    lax.fori_loop(0, nch, readout, 0)


def gla_scan(qk, vr, gf, gb, norm_g, j):
    return pl.pallas_call(
        _gla_kernel,
        grid=(BATCH, GLA_H),
        in_specs=[
            pl.BlockSpec((ROWS, GLA_DK), lambda b, h: (b, h)),
            pl.BlockSpec((ROWS, GLA_DK), lambda b, h: (b, GLA_H + h)),
            pl.BlockSpec((ROWS, GLA_DV), lambda b, h: (b, h)),
            pl.BlockSpec((ROWS, GLA_DV), lambda b, h: (b, GLA_H + h)),
            pl.BlockSpec((ROWS, GLA_DK), lambda b, h: (b, h)),
            pl.BlockSpec((ROWS, GLA_DK), lambda b, h: (b, h)),
            pl.BlockSpec((None, 1, GLA_DV), lambda b, h: (j, 0, 0)),
        ],
        out_specs=pl.BlockSpec((ROWS, GLA_DV), lambda b, h: (b, h)),
        out_shape=jax.ShapeDtypeStruct((T, D), BF16),
        scratch_shapes=[pltpu.VMEM((GLA_DV, GLA_DK), F32), pltpu.VMEM((GLA_DV, GLA_DK), F32),
                        pltpu.VMEM((ROWS, GLA_DV), F32), pltpu.VMEM((ROWS, GLA_DV), F32)],
        compiler_params=_params("arbitrary", "arbitrary"),
        name="gla_scan",
    )(qk, qk, vr, vr, gf, gb, norm_g.reshape(-1, 1, GLA_DV))


def rope_tables():
    half = GLA_DK // 2
    freqs = ROPE_BASE ** (-np.arange(0, half, 2, dtype=np.float32) / half)
    t = np.arange(SEQ)
    pos = np.stack([t // GRID_W, t % GRID_W], axis=-1).astype(np.float32)
    ang = pos[:, :, None] * freqs
    cos, sin = np.cos(ang), np.sin(ang)
    cos_t = np.concatenate([cos, cos], axis=-1).reshape(SEQ, GLA_DK)
    sin_t = np.concatenate([-sin, sin], axis=-1).reshape(SEQ, GLA_DK)
    cos_t = np.concatenate([np.ones((CTX, GLA_DK), np.float32), cos_t], axis=0)
    sin_t = np.concatenate([np.zeros((CTX, GLA_DK), np.float32), sin_t], axis=0)
    return jnp.asarray(cos_t, F32), jnp.asarray(sin_t, F32)


def _nat_kernel(q_ref, k_ref, v_ref, bias_ref, y_ref, *, need_ctx):
    kc = k_ref[0:CTX, :]
    vc = v_ref[0:CTX, :]
    if need_ctx:
        s = _dot_nt(q_ref[0:CTX, :], kc)
        p = jnp.exp(s - jnp.max(s, axis=-1, keepdims=True))
        o = _dot(p.astype(BF16), vc) / jnp.sum(p, axis=-1, keepdims=True)
        y_ref[0:CTX, :] = o.astype(y_ref.dtype)
    else:
        y_ref[0:CTX, :] = jnp.zeros((CTX, NAT_DH), y_ref.dtype)
    n_keys = WIN_R * GRID_W

    def scores(r):
        rs = jnp.clip(r - WIN_R // 2, 0, GRID_H - WIN_R)
        q0 = pl.multiple_of(CTX + r * GRID_W, GRID_W)
        k0 = pl.multiple_of(CTX + rs * GRID_W, GRID_W)
        q = q_ref[pl.ds(q0, GRID_W), :]
        s_lat = _dot_nt(q, k_ref[pl.ds(k0, n_keys), :]) + bias_ref[r - rs]
        s_ctx = _dot_nt(q, kc)
        return q0, k0, s_lat, s_ctx

    def probs(q0, k0, s_lat, s_ctx):
        m = jnp.maximum(jnp.max(s_lat, axis=-1, keepdims=True), jnp.max(s_ctx, axis=-1, keepdims=True))
        p_lat = jnp.exp(s_lat - m)
        p_ctx = jnp.exp(s_ctx - m)
        denom = jnp.sum(p_lat, axis=-1, keepdims=True) + jnp.sum(p_ctx, axis=-1, keepdims=True)
        return q0, k0, p_lat.astype(BF16), p_ctx.astype(BF16), denom

    def values(q0, k0, p_lat, p_ctx, denom):
        o = _dot(p_lat, v_ref[pl.ds(k0, n_keys), :]) + _dot(p_ctx, vc)
        return q0, (o / denom).astype(y_ref.dtype)

    def body(it, carry):
        rows = [it * NAT_ROWS_PER_STEP + u for u in range(NAT_ROWS_PER_STEP)]
        outs = [values(*pr) for pr in [probs(*sc) for sc in [scores(r) for r in rows]]]
        for q0, o in outs:
            y_ref[pl.ds(q0, GRID_W), :] = o
        return carry

    lax.fori_loop(0, GRID_H // NAT_ROWS_PER_STEP, body, 0)


def nat_attention(qkv, bias_tbl, need_ctx):
    return pl.pallas_call(
        functools.partial(_nat_kernel, need_ctx=need_ctx),
        grid=(NAT_H, BATCH),
        in_specs=[
            pl.BlockSpec((ROWS, NAT_DH), lambda h, b: (b, h)),
            pl.BlockSpec((ROWS, NAT_DH), lambda h, b: (b, NAT_H + h)),
            pl.BlockSpec((ROWS, NAT_DH), lambda h, b: (b, 2 * NAT_H + h)),
            pl.BlockSpec((None, WIN_R, GRID_W, WIN_R * GRID_W), lambda h, b: (h, 0, 0, 0)),
        ],
        out_specs=pl.BlockSpec((ROWS, NAT_DH), lambda h, b: (b, h)),
        out_shape=jax.ShapeDtypeStruct((T, D), BF16),
        compiler_params=_params("arbitrary", "arbitrary"),
        name="nat_attention",
    )(qkv, qkv, qkv, bias_tbl)


def nat_bias_table(rpb):
    qc = np.arange(GRID_W)[:, None]
    kcol = np.arange(GRID_W)[None, :]
    start = np.clip(qc - WIN_C // 2, 0, GRID_W - WIN_C)
    in_win = (kcol >= start) & (kcol < start + WIN_C)
    dc = np.clip(kcol - qc + WIN_C - 1, 0, 2 * WIN_C - 2)
    pick = jnp.asarray(dc[None] == np.arange(2 * WIN_C - 1)[:, None, None], F32)
    by_col = jnp.einsum("hdc,cqk->hqdk", rpb, pick, precision=HIGHEST)
    by_col = jnp.where(in_win[None, :, None, :], by_col, -jnp.inf)
    tbl = jnp.stack([by_col[:, :, WIN_R - 1 - off:2 * WIN_R - 1 - off, :] for off in range(WIN_R)], axis=1)
    return tbl.reshape(NAT_H, WIN_R, GRID_W, WIN_R * GRID_W)


def _pair_slots(e0_ref, e1_ref, r0_ref, r1_ref, start_ref, t):
    return start_ref[e0_ref[t]] + r0_ref[t], start_ref[e1_ref[t]] + r1_ref[t]


def _is_routed_tile(i, latent_only):
    return (i % (ROWS // TG) >= CTX // TG) if latent_only else None


def _dispatch_kernel(e0_ref, e1_ref, r0_ref, r1_ref, start_ref, part_ref, h_ref, o_ref, zero_ref, sem, *,
                     latent_only):
    i = pl.program_id(0)
    base = i * TG

    @pl.when(i == 0)
    def _():
        zero_ref[...] = jnp.zeros_like(zero_ref)

        def zero_copy(p):
            return pltpu.make_async_copy(zero_ref, o_ref.at[pl.ds(pl.multiple_of(p * TM_PART, TM_PART), TM_PART)],
                                         sem.at[2])

        def start(p, carry):
            pl.when(part_ref[p] < TM_PART)(lambda: zero_copy(p).start())
            return carry

        def wait(p, carry):
            pl.when(part_ref[p] < TM_PART)(lambda: zero_copy(p).wait())
            return carry

        n_parts = o_ref.shape[0] // TM_PART
        lax.fori_loop(0, n_parts, start, 0)
        lax.fori_loop(0, n_parts, wait, 0)

    def start_rows(r, carry):
        row = h_ref.at[pl.ds(r, 1)]
        s0, s1 = _pair_slots(e0_ref, e1_ref, r0_ref, r1_ref, start_ref, base + r)
        pltpu.make_async_copy(row, o_ref.at[pl.ds(s0, 1)], sem.at[0]).start()
        pltpu.make_async_copy(row, o_ref.at[pl.ds(s1, 1)], sem.at[1]).start()
        return carry

    def run():
        lax.fori_loop(0, TG, start_rows, 0, unroll=DMA_UNROLL)
        for s in range(2):
            pltpu.make_async_copy(h_ref, o_ref.at[pl.ds(0, TG)], sem.at[s]).wait()

    routed = _is_routed_tile(i, latent_only)
    if routed is None:
        run()
    else:
        pl.when(routed)(run)


def moe_dispatch(routing, part_rows, h_words, n_slots, latent_only):
    words = h_words.shape[1]
    return pl.pallas_call(
        functools.partial(_dispatch_kernel, latent_only=latent_only),
        grid_spec=pltpu.PrefetchScalarGridSpec(
            num_scalar_prefetch=6,
            grid=(T // TG,),
            in_specs=[pl.BlockSpec((TG, words), lambda i, *_: (i, 0))],
            out_specs=pl.BlockSpec(memory_space=pl.ANY),
            scratch_shapes=[pltpu.VMEM((TM_PART, words), jnp.uint32), pltpu.SemaphoreType.DMA((3,))],
        ),
        out_shape=jax.ShapeDtypeStruct((n_slots, words), jnp.uint32),
        compiler_params=_params("arbitrary"),
        name="moe_dispatch",
    )(*routing, part_rows, h_words)


def _expert_changed(te_ref, i):
    return (i == 0) | (te_ref[i] != te_ref[jnp.maximum(i - 1, 0)])


def _on_occupied_prefix(n_rows, tile, o_ref, compute):
    for parts in range(tile // TM_PART + 1):
        p = parts * TM_PART

        @pl.when((n_rows > p - TM_PART) & (n_rows <= p))
        def _():
            if p > 0:
                o_ref[0:p, :] = compute(p)
            if p < tile:
                o_ref[p:tile, :] = jnp.zeros((tile - p, o_ref.shape[1]), o_ref.dtype)


def _moe_up_kernel(te_ref, tr_ref, ts_ref, to_ref, a_ref, wg_ref, wu_ref, o_ref, wgb_ref, wub_ref):
    del ts_ref, to_ref
    i = pl.program_id(1)

    @pl.when(_expert_changed(te_ref, i))
    def _():
        wgb_ref[...] = wg_ref[...].astype(BF16)
        wub_ref[...] = wu_ref[...].astype(BF16)

    half = D // 2

    def compute(p):
        lo, hi = _unpack_words(a_ref[0:p, :])
        gate = _dot(lo, wgb_ref[0:half, :]) + _dot(hi, wgb_ref[half:D, :])
        up = _dot(lo, wub_ref[0:half, :]) + _dot(hi, wub_ref[half:D, :])
        return (_silu(gate) * up).astype(o_ref.dtype)

    _on_occupied_prefix(tr_ref[i], TM_UP, o_ref, compute)


def moe_up(tiles, xw, wg, wu, j):
    n_slots = xw.shape[0]
    w_spec = pl.BlockSpec((None, None, D, TF_MOE), lambda f, i, te, tr, ts, to: (j, te[i], 0, f))
    return pl.pallas_call(
        _moe_up_kernel,
        grid_spec=pltpu.PrefetchScalarGridSpec(
            num_scalar_prefetch=4,
            grid=(D_FF // TF_MOE, n_slots // TM_UP),
            in_specs=[pl.BlockSpec((TM_UP, D // 2), lambda f, i, te, tr, ts, to: (ts[i], 0)), w_spec, w_spec],
            out_specs=pl.BlockSpec((TM_UP, TF_MOE), lambda f, i, te, tr, ts, to: (to[i], f)),
            scratch_shapes=[pltpu.VMEM((D, TF_MOE), BF16), pltpu.VMEM((D, TF_MOE), BF16)],
        ),
        out_shape=jax.ShapeDtypeStruct((n_slots, D_FF), BF16),
        compiler_params=_params("arbitrary", "arbitrary"),
        name="moe_up",
    )(*tiles, xw, wg, wu)


def _moe_down_kernel(te_ref, tr_ref, ts_ref, to_ref, a_ref, w_ref, o_ref, wb_ref):
    del ts_ref, to_ref
    i = pl.program_id(1)

    @pl.when(_expert_changed(te_ref, i))
    def _():
        wb_ref[...] = w_ref[...].astype(BF16)

    _on_occupied_prefix(tr_ref[i], TM_DOWN, o_ref, lambda p: _dot(a_ref[0:p, :], wb_ref[...]))


def moe_down(tiles, act, wd, j):
    n_slots = act.shape[0]
    tn = TN_MOE_DOWN
    return pl.pallas_call(
        _moe_down_kernel,
        grid_spec=pltpu.PrefetchScalarGridSpec(
            num_scalar_prefetch=4,
            grid=(D // tn, n_slots // TM_DOWN),
            in_specs=[
                pl.BlockSpec((TM_DOWN, D_FF), lambda n, i, te, tr, ts, to: (ts[i], 0)),
                pl.BlockSpec((None, None, D_FF, tn), lambda n, i, te, tr, ts, to: (j, te[i], 0, n),
                             pipeline_mode=pl.Buffered(1)),
            ],
            out_specs=pl.BlockSpec((TM_DOWN, tn), lambda n, i, te, tr, ts, to: (to[i], n)),
            scratch_shapes=[pltpu.VMEM((D_FF, tn), BF16)],
        ),
        out_shape=jax.ShapeDtypeStruct((n_slots, D), F32),
        compiler_params=_params("arbitrary", "arbitrary"),
        name="moe_down",
    )(*tiles, act, wd)


def _combine_kernel(e0_ref, e1_ref, r0_ref, r1_ref, start_ref, ys_ref, x_ref, gt_ref, w_ref, fg_ref, o_ref,
                    b0_ref, b1_ref, sem, *, latent_only, final):
    i = pl.program_id(0)
    base = i * TG

    def start_rows(r, carry):
        s0, s1 = _pair_slots(e0_ref, e1_ref, r0_ref, r1_ref, start_ref, base + r)
        pltpu.make_async_copy(ys_ref.at[pl.ds(s0, 1)], b0_ref.at[pl.ds(r, 1)], sem.at[0]).start()
        pltpu.make_async_copy(ys_ref.at[pl.ds(s1, 1)], b1_ref.at[pl.ds(r, 1)], sem.at[1]).start()
        return carry

    def run():
        lax.fori_loop(0, TG, start_rows, 0, unroll=DMA_UNROLL)
        pltpu.make_async_copy(ys_ref.at[pl.ds(0, TG)], b0_ref, sem.at[0]).wait()
        pltpu.make_async_copy(ys_ref.at[pl.ds(0, TG)], b1_ref, sem.at[1]).wait()
        mix = w_ref[:, 0:1] * b0_ref[...] + w_ref[:, 1:2] * b1_ref[...]
        out = x_ref[...] + _row_gate(gt_ref, i, TG) * mix
        if final:
            out = out * lax.rsqrt(jnp.mean(out * out, axis=-1, keepdims=True) + EPS) * fg_ref[...]
        o_ref[...] = out

    routed = _is_routed_tile(i, latent_only)
    if routed is None:
        run()
    else:
        pl.when(routed)(run)
        if not final:
            @pl.when(jnp.logical_not(routed))
            def _():
                o_ref[...] = x_ref[...]


def moe_combine(routing, ys, x, mods, layer, gate_col, wcol, latent_only, final_g=None):
    final = final_g is not None
    assert latent_only or not final
    per_batch, n_ctx, n_lat = ROWS // TG, CTX // TG, SEQ // TG
    if final:
        out_rows = BATCH * SEQ
        out_map = lambda i, *_: ((i // per_batch) * n_lat + jnp.maximum(i % per_batch - n_ctx, 0), 0)
    else:
        out_rows = T
        out_map = lambda i, *_: (i, 0)
        final_g = jnp.ones((D,), F32)
    return pl.pallas_call(
        functools.partial(_combine_kernel, latent_only=latent_only, final=final),
        grid_spec=pltpu.PrefetchScalarGridSpec(
            num_scalar_prefetch=5,
            grid=(T // TG,),
            in_specs=[
                pl.BlockSpec(memory_space=pl.ANY),
                pl.BlockSpec((TG, D), lambda i, *_: (i, 0)),
                pl.BlockSpec((None, N_MOD, D), lambda i, *_: (layer, 0, gate_col)),
                pl.BlockSpec((TG, LANES), lambda i, *_: (i, 0)),
                pl.BlockSpec((1, D), lambda i, *_: (0, 0)),
            ],
            out_specs=pl.BlockSpec((TG, D), out_map),
            scratch_shapes=[pltpu.VMEM((TG, D), F32), pltpu.VMEM((TG, D), F32), pltpu.SemaphoreType.DMA((2,))],
        ),
        out_shape=jax.ShapeDtypeStruct((out_rows, D), F32),
        compiler_params=_params("arbitrary"),
        name="moe_combine",
    )(*routing, ys, x, mods, wcol, final_g.reshape(1, D))


def moe_slots(n_tokens):
    return (2 * n_tokens // TM_UP + N_EXP) * TM_UP


def moe_plan(counts, n_slots):
    group = (counts + TM_UP - 1) // TM_UP * TM_UP
    start = jnp.cumsum(group) - group
    end = start + counts
    last_row = jnp.max(jnp.where(counts > 0, end, 0)) - 1

    def of_expert(table, expert):
        return jnp.sum(jnp.where(expert[:, None] == jnp.arange(N_EXP), table, 0), axis=1)

    def occupied_rows(tm):
        ids = jnp.arange(n_slots // tm)
        expert = jnp.sum((start + group)[None, :] <= jnp.minimum(ids * tm, last_row)[:, None], axis=1)
        return ids, expert, jnp.clip(of_expert(end, expert) - ids * tm, 0, tm)

    def tiles(tm):
        ids, expert, rows = occupied_rows(tm)
        first = of_expert(start, expert) // tm
        n_occ = (of_expert(counts, expert) + tm - 1) // tm
        k = ids - first
        tile = jnp.where(k < n_occ, first + (k - 1) % jnp.maximum(n_occ, 1), ids)
        rows_v = jnp.clip(of_expert(end, expert) - tile * tm, 0, tm)
        fetched = jnp.where(rows_v > 0, ids, 0)
        last_fetch = jnp.max(jnp.where(ids[None, :] <= ids[:, None], fetched[None, :], 0), axis=1)
        block = jnp.sum(jnp.where(ids[None, :] == last_fetch[:, None], tile[None, :], 0), axis=1)
        return tuple(a.astype(jnp.int32) for a in (expert, rows_v, block, tile))

    return start.astype(jnp.int32), tiles(TM_UP), tiles(TM_DOWN), occupied_rows(TM_PART)[2].astype(jnp.int32)


def _pad_cols(w, n):
    return jnp.pad(w, ((0, 0), (0, n - w.shape[1])))


def kernel(x, c, ctx, c_ctx, ada_w, ada_b, norm_mix_g, norm_ffn_g, gla_w_in, gla_wg_fwd, gla_bg_fwd, gla_wg_bwd, gla_bg_bwd, gla_norm_g, gla_w_out, nat_w_in, nat_rpb, nat_w_out, ffn_w_gate, ffn_w_up, ffn_w_down, moe_router, moe_w_gate, moe_w_up, moe_w_down, final_norm_g):
    c8 = jnp.concatenate([c, c_ctx[None, :], jnp.zeros((N_MOD - BATCH - 1, D), F32)], axis=0)
    mods = ada_tables(c8, ada_w, ada_b)
    cos_t, sin_t = rope_tables()
    router_pad = jnp.pad(moe_router, ((0, 0), (0, 0), (0, LANES - N_EXP)))

    for i in range(DEPTH):
        j = i // 2
        last = i == DEPTH - 1
        if i == 0:
            xs, h = embed_modulate(x, ctx, norm_mix_g, mods, i, 0, 1)
        else:
            h = modulate(xs, norm_mix_g, mods, i, 0, 1)
        if i % 2 == 0:
            qk = mm_rope(h, gla_w_in, j, cos_t, sin_t, tm=1152)
            vr = mm_plain(h, gla_w_in, j, 2 * GLA_QK, 2 * D, BF16, tm=1152, tn=1024)
            z0 = 2 * GLA_QK + 2 * D
            wz_pad = _pad_cols(gla_w_in[j, :, z0:z0 + 2 * GLA_RANK], LANES)
            wf_pad = jnp.pad(gla_wg_fwd[j], ((0, LANES - GLA_RANK), (0, 0)))
            wb_pad = jnp.pad(gla_wg_bwd[j], ((GLA_RANK, LANES - 2 * GLA_RANK), (0, 0)))
            gf, gb = gla_gates(h, wz_pad, wf_pad, gla_bg_fwd, wb_pad, gla_bg_bwd, j, tm=1152)
            y = gla_scan(qk, vr, gf, gb, gla_norm_g, j)
            xs = mm_resid(y, gla_w_out, j, xs, mods, i, 2, tm=1152, tn=1024)
        else:
            qkv = mm_plain(h, nat_w_in, j, 0, 3 * D, BF16, tm=1152, tn=1024,
                           scale=NAT_DH ** -0.5, n_scaled=D // 1024)
            y = nat_attention(qkv, nat_bias_table(nat_rpb[j]), not last)
            xs = mm_resid(y, nat_w_out, j, xs, mods, i, 2, tm=1152, tn=1024)
        if i % 2 == 0:
            h2 = modulate(xs, norm_ffn_g, mods, i, 3, 4)
            act = mm_swiglu(h2, ffn_w_gate, ffn_w_up, j, tm=1152, tn=512)
            xs = mm_resid(act, ffn_w_down, j, xs, mods, i, 5, tm=576, tn=512)
        else:
            h_words, idx, rank, wcol, counts = modulate_route(xs, norm_ffn_g, mods, i, 3, 4, router_pad, j, last)
            n_slots = moe_slots(BATCH * SEQ if last else T)
            start, up_tiles, down_tiles, part_rows = moe_plan(counts[:, 0], n_slots)
            routing = (idx[0], idx[1], rank[0], rank[1], start)
            xw = moe_dispatch(routing, part_rows, h_words, n_slots, last)
            act = moe_up(up_tiles, xw, moe_w_gate, moe_w_up, j)
            ys = moe_down(down_tiles, act, moe_w_down, j)
            xs = moe_combine(routing, ys, xs, mods, i, 5, wcol, last, final_norm_g if last else None)
    return xs.reshape(BATCH, SEQ, D)
```

```python
import functools

import numpy as np
import jax
import jax.numpy as jnp
from jax import lax
from jax.experimental import pallas as pl
from jax.experimental.pallas import tpu as pltpu

F32 = jnp.float32
BF16 = jnp.bfloat16
HIGHEST = lax.Precision.HIGHEST

D = 2048
BATCH = 4
SEQ = 2048
CTX = 256
ROWS = CTX + SEQ
T = BATCH * ROWS
DEPTH = 4
GRID_W = 64
GRID_H = SEQ // GRID_W
EPS = 1e-6
ROPE_BASE = 10000.0
CTX_GROUP = BATCH
N_MOD = 8

GLA_H = 4
GLA_DK = 256
GLA_DV = 512
GLA_QK = GLA_H * GLA_DK
GLA_RANK = 16
GLA_TAU = 16.0
GLA_C = 64
GLA_SPLIT_DECAY_MAX = 60.0

NAT_H = 16
NAT_DH = 128
WIN_R = 8
WIN_C = 16
NAT_ROWS_PER_STEP = 32

D_FF = 5632
N_EXP = 8

LANES = 128
VMEM_LIMIT = 56 * 1024 * 1024

TM_MOD = 256
TM_MODULATE = 768
TM_UP = 1024
TM_DOWN = 512
TM_PART = 128
TF_MOE = 512
TN_MOE_DOWN = 1024
TG = 256
DMA_UNROLL = 16


def _params(*sem):
    return pltpu.CompilerParams(dimension_semantics=sem, vmem_limit_bytes=VMEM_LIMIT)


def _dot(a, b):
    return jnp.dot(a, b, preferred_element_type=F32)


def _dot_nt(a, b):
    return lax.dot_general(a, b, (((1,), (1,)), ((), ())), preferred_element_type=F32)


def _dot_tn(a, b):
    return lax.dot_general(a, b, (((0,), (0,)), ((), ())), preferred_element_type=F32)


def _dot_hi_lo(a, b):
    a_hi, b_hi = a.astype(BF16), b.astype(BF16)
    a_lo = (a - a_hi.astype(F32)).astype(BF16)
    b_lo = (b - b_hi.astype(F32)).astype(BF16)
    return _dot(a_hi, b_hi) + (_dot(a_hi, b_lo) + _dot(a_lo, b_hi))


def _silu(v):
    return v * jax.nn.sigmoid(v)


def _row_gate(tab_ref, m, tm):
    start = m * tm
    b = start // ROWS
    rows = start % ROWS + lax.broadcasted_iota(jnp.int32, (tm, 1), 0)
    per_batch = tab_ref[pl.ds(b, 1), :]
    per_ctx = tab_ref[CTX_GROUP:CTX_GROUP + 1, :]
    return jnp.where(rows < CTX, per_ctx, per_batch)


def _ada_kernel(c_ref, w_ref, b_ref, o_ref):
    a = _silu(c_ref[...]).astype(BF16)
    o_ref[...] = _dot(a, w_ref[...].astype(BF16)) + b_ref[...]


def ada_tables(c8, ada_w, ada_b):
    tn = 1024
    return pl.pallas_call(
        _ada_kernel,
        grid=(DEPTH, 6 * D // tn),
        in_specs=[
            pl.BlockSpec((N_MOD, D), lambda l, n: (0, 0)),
            pl.BlockSpec((None, D, tn), lambda l, n: (l, 0, n)),
            pl.BlockSpec((None, 1, tn), lambda l, n: (l, 0, n)),
        ],
        out_specs=pl.BlockSpec((None, N_MOD, tn), lambda l, n: (l, 0, n)),
        out_shape=jax.ShapeDtypeStruct((DEPTH, N_MOD, 6 * D), F32),
        compiler_params=_params("arbitrary", "arbitrary"),
        name="ada_tables",
    )(c8, ada_w, ada_b.reshape(DEPTH, 1, 6 * D))


def _modulate_rows(x, g_ref, sh_ref, sc_ref):
    tm = x.shape[0]
    y = x * lax.rsqrt(jnp.mean(x * x, axis=-1, keepdims=True) + EPS) * g_ref[...]
    i = pl.program_id(0)
    return y * (1.0 + _row_gate(sc_ref, i, tm)) + _row_gate(sh_ref, i, tm)


def _modulated(x_ref, g_ref, sh_ref, sc_ref):
    return _modulate_rows(x_ref[...], g_ref, sh_ref, sc_ref)


def _embed_modulate_kernel(x_ref, c_ref, g_ref, sh_ref, sc_ref, xs_ref, h_ref):
    is_ctx = pl.program_id(0) % (ROWS // TM_MOD) < CTX // TM_MOD
    rows = jnp.where(is_ctx, c_ref[...], x_ref[...])
    xs_ref[...] = rows
    h_ref[...] = _modulate_rows(rows, g_ref, sh_ref, sc_ref).astype(h_ref.dtype)


def embed_modulate(x, ctx, norm_g, mods, layer, sh_col, sc_col):
    per_batch = ROWS // TM_MOD
    n_ctx = CTX // TM_MOD
    n_lat = SEQ // TM_MOD
    specs = _mod_specs(layer, sh_col, sc_col, TM_MOD)
    return pl.pallas_call(
        _embed_modulate_kernel,
        grid=(T // TM_MOD,),
        in_specs=[
            pl.BlockSpec((TM_MOD, D), lambda i: ((i // per_batch) * n_lat + jnp.maximum(i % per_batch - n_ctx, 0), 0)),
            pl.BlockSpec((TM_MOD, D), lambda i: ((i // per_batch) * n_ctx + jnp.minimum(i % per_batch, n_ctx - 1), 0)),
        ] + specs[1:],
        out_specs=[pl.BlockSpec((TM_MOD, D), lambda i: (i, 0)), pl.BlockSpec((TM_MOD, D), lambda i: (i, 0))],
        out_shape=[jax.ShapeDtypeStruct((T, D), F32), jax.ShapeDtypeStruct((T, D), BF16)],
        compiler_params=_params("arbitrary"),
        name="embed_modulate",
    )(x.reshape(BATCH * SEQ, D), ctx.reshape(BATCH * CTX, D), norm_g.reshape(DEPTH, 1, D), mods, mods)


def _modulate_kernel(x_ref, g_ref, sh_ref, sc_ref, o_ref):
    o_ref[...] = _modulated(x_ref, g_ref, sh_ref, sc_ref).astype(o_ref.dtype)


def _mod_specs(layer, sh_col, sc_col, tm):
    return [
        pl.BlockSpec((tm, D), lambda i: (i, 0)),
        pl.BlockSpec((None, 1, D), lambda i: (layer, 0, 0)),
        pl.BlockSpec((None, N_MOD, D), lambda i: (layer, 0, sh_col)),
        pl.BlockSpec((None, N_MOD, D), lambda i: (layer, 0, sc_col)),
    ]


def modulate(x, norm_g, mods, layer, sh_col, sc_col):
    tm = TM_MODULATE
    return pl.pallas_call(
        _modulate_kernel,
        grid=(T // tm,),
        in_specs=_mod_specs(layer, sh_col, sc_col, tm),
        out_specs=pl.BlockSpec((tm, D), lambda i: (i, 0)),
        out_shape=jax.ShapeDtypeStruct((T, D), BF16),
        compiler_params=_params("arbitrary"),
        name="modulate",
    )(x, norm_g.reshape(DEPTH, 1, D), mods, mods)


def _pack_words(h):
    half = h.shape[1] // 2
    lo = lax.bitcast_convert_type(h[:, :half].astype(BF16).astype(F32), jnp.uint32)
    hi = lax.bitcast_convert_type(h[:, half:].astype(BF16).astype(F32), jnp.uint32)
    return (lo >> 16) | (hi & jnp.uint32(0xFFFF0000))


def _unpack_words(w):
    lo = lax.bitcast_convert_type(w << 16, F32).astype(BF16)
    hi = lax.bitcast_convert_type(w & jnp.uint32(0xFFFF0000), F32).astype(BF16)
    return lo, hi


def _modulate_route_kernel(x_ref, g_ref, sh_ref, sc_ref, rt_ref, hw_ref, idx_ref, rank_ref, wcol_ref, cnt_ref,
                           base_ref, *, latent_only):
    i = pl.program_id(0)

    @pl.when(i == 0)
    def _():
        base_ref[...] = jnp.zeros_like(base_ref)

    h = _modulated(x_ref, g_ref, sh_ref, sc_ref)
    hw_ref[...] = _pack_words(h)
    logits = jnp.dot(h, rt_ref[...], precision=HIGHEST, preferred_element_type=F32)

    lane = lax.broadcasted_iota(jnp.int32, logits.shape, 1)
    lm = jnp.where(lane < N_EXP, logits, -jnp.inf)
    c1 = jnp.max(lm, axis=1, keepdims=True)
    j1 = jnp.min(jnp.where(lm == c1, lane, LANES), axis=1, keepdims=True)
    c2 = jnp.max(jnp.where(lane == j1, -jnp.inf, lm), axis=1, keepdims=True)
    e = jnp.exp(c2 - c1)
    wcol_ref[...] = jnp.where(lane == 0, 1.0 / (1.0 + e), jnp.where(lane == 1, e / (1.0 + e), 0.0))

    lt = logits.T[:N_EXP]
    eid = lax.broadcasted_iota(jnp.int32, lt.shape, 0)
    m1 = jnp.max(lt, axis=0, keepdims=True)
    i1 = jnp.min(jnp.where(lt == m1, eid, N_EXP), axis=0, keepdims=True)
    lt2 = jnp.where(eid == i1, -jnp.inf, lt)
    m2 = jnp.max(lt2, axis=0, keepdims=True)
    i2 = jnp.min(jnp.where(lt2 == m2, eid, N_EXP), axis=0, keepdims=True)
    idx_ref[...] = jnp.concatenate([i1, i2], axis=0)

    routed = 1.0
    if latent_only:
        routed = jnp.where(i % (ROWS // TM_MOD) < CTX // TM_MOD, 0.0, 1.0)
    oh1 = jnp.where(eid == i1, routed, 0.0)
    oh2 = jnp.where(eid == i2, routed, 0.0)
    both = oh1 + oh2
    tj = lax.broadcasted_iota(jnp.int32, (TM_MOD, TM_MOD), 0)
    tt = lax.broadcasted_iota(jnp.int32, (TM_MOD, TM_MOD), 1)
    before = _dot(both.astype(BF16), jnp.where(tj < tt, 1.0, 0.0).astype(BF16))
    seen = base_ref[:, 0:1] + before
    r1 = jnp.sum(oh1 * seen, axis=0, keepdims=True)
    r2 = jnp.sum(oh2 * seen, axis=0, keepdims=True)
    rank_ref[...] = jnp.concatenate([r1, r2], axis=0).astype(jnp.int32)
    total = base_ref[...] + jnp.sum(both, axis=1, keepdims=True)
    base_ref[...] = total
    cnt_ref[...] = total.astype(jnp.int32)


def modulate_route(x, norm_g, mods, layer, sh_col, sc_col, router_pad, j, latent_only):
    pairs = pl.BlockSpec((2, TM_MOD), lambda i: (0, i))
    return pl.pallas_call(
        functools.partial(_modulate_route_kernel, latent_only=latent_only),
        grid=(T // TM_MOD,),
        in_specs=_mod_specs(layer, sh_col, sc_col, TM_MOD) + [
            pl.BlockSpec((None, D, LANES), lambda i: (j, 0, 0)),
        ],
        out_specs=[
            pl.BlockSpec((TM_MOD, D // 2), lambda i: (i, 0)),
            pairs, pairs,
            pl.BlockSpec((TM_MOD, LANES), lambda i: (i, 0)),
            pl.BlockSpec((N_EXP, LANES), lambda i: (0, 0)),
        ],
        out_shape=[
            jax.ShapeDtypeStruct((T, D // 2), jnp.uint32),
            jax.ShapeDtypeStruct((2, T), jnp.int32),
            jax.ShapeDtypeStruct((2, T), jnp.int32),
            jax.ShapeDtypeStruct((T, LANES), F32),
            jax.ShapeDtypeStruct((N_EXP, LANES), jnp.int32),
        ],
        scratch_shapes=[pltpu.VMEM((N_EXP, LANES), F32)],
        compiler_params=_params("arbitrary"),
        name="modulate_route",
    )(x, norm_g.reshape(DEPTH, 1, D), mods, mods, router_pad)


def _cast_weight(w_ref, wb_ref):
    @pl.when(pl.program_id(1) == 0)
    def _():
        wb_ref[...] = w_ref[...].astype(BF16)


def _mm_plain_kernel(a_ref, w_ref, o_ref, wb_ref, *, scale, n_scaled):
    _cast_weight(w_ref, wb_ref)
    acc = _dot(a_ref[...], wb_ref[...])
    if n_scaled:
        acc = acc * jnp.where(pl.program_id(0) < n_scaled, scale, 1.0)
    o_ref[...] = acc.astype(o_ref.dtype)


def mm_plain(a, w, layer, col0, n_out, out_dtype, tm, tn, scale=1.0, n_scaled=0):
    k = a.shape[1]
    c0 = col0 // tn
    return pl.pallas_call(
        functools.partial(_mm_plain_kernel, scale=scale, n_scaled=n_scaled),
        grid=(n_out // tn, T // tm),
        in_specs=[
            pl.BlockSpec((tm, k), lambda n, m: (m, 0)),
            pl.BlockSpec((None, k, tn), lambda n, m: (layer, 0, n + c0)),
        ],
        out_specs=pl.BlockSpec((tm, tn), lambda n, m: (m, n)),
        out_shape=jax.ShapeDtypeStruct((T, n_out), out_dtype),
        scratch_shapes=[pltpu.VMEM((k, tn), BF16)],
        compiler_params=_params("arbitrary", "arbitrary"),
        name="mm_plain",
    )(a, w)


def _mm_rope_kernel(a_ref, w_ref, cos_ref, sin_ref, o_ref, wb_ref, *, tn):
    _cast_weight(w_ref, wb_ref)
    acc = _dot(a_ref[...], wb_ref[...])
    scale = jnp.where(pl.program_id(0) == 0, GLA_DK ** -0.5, 1.0)
    for s in range(tn // LANES):
        xs = acc[:, s * LANES:(s + 1) * LANES]
        t = (s % 2) * LANES
        rot = xs * cos_ref[:, t:t + LANES] + pltpu.roll(xs, LANES // 2, 1) * sin_ref[:, t:t + LANES]
        o_ref[:, s * LANES:(s + 1) * LANES] = rot * scale


def mm_rope(a, w, layer, cos_t, sin_t, tm):
    tn = GLA_QK
    per_batch = ROWS // tm
    return pl.pallas_call(
        functools.partial(_mm_rope_kernel, tn=tn),
        grid=(2, T // tm),
        in_specs=[
            pl.BlockSpec((tm, D), lambda n, m: (m, 0)),
            pl.BlockSpec((None, D, tn), lambda n, m: (layer, 0, n)),
            pl.BlockSpec((tm, GLA_DK), lambda n, m: (m % per_batch, 0)),
            pl.BlockSpec((tm, GLA_DK), lambda n, m: (m % per_batch, 0)),
        ],
        out_specs=pl.BlockSpec((tm, tn), lambda n, m: (m, n)),
        out_shape=jax.ShapeDtypeStruct((T, 2 * GLA_QK), F32),
        scratch_shapes=[pltpu.VMEM((D, tn), BF16)],
        compiler_params=_params("arbitrary", "arbitrary"),
        name="mm_rope",
    )(a, w, cos_t, sin_t)


def _mm_resid_kernel(a_ref, w_ref, x_ref, gt_ref, o_ref, wb_ref, *, tm):
    _cast_weight(w_ref, wb_ref)
    acc = _dot(a_ref[...], wb_ref[...])
    o_ref[...] = x_ref[...] + _row_gate(gt_ref, pl.program_id(1), tm) * acc


def mm_resid(a, w, layer_w, x, mods, layer, gate_col, tm, tn):
    k = a.shape[1]
    g0 = gate_col * D // tn
    return pl.pallas_call(
        functools.partial(_mm_resid_kernel, tm=tm),
        grid=(D // tn, T // tm),
        in_specs=[
            pl.BlockSpec((tm, k), lambda n, m: (m, 0)),
            pl.BlockSpec((None, k, tn), lambda n, m: (layer_w, 0, n)),
            pl.BlockSpec((tm, tn), lambda n, m: (m, n)),
            pl.BlockSpec((None, N_MOD, tn), lambda n, m: (layer, 0, g0 + n)),
        ],
        out_specs=pl.BlockSpec((tm, tn), lambda n, m: (m, n)),
        out_shape=jax.ShapeDtypeStruct((T, D), F32),
        scratch_shapes=[pltpu.VMEM((k, tn), BF16)],
        compiler_params=_params("arbitrary", "arbitrary"),
        name="mm_resid",
    )(a, w, x, mods)


def _mm_swiglu_kernel(a_ref, wg_ref, wu_ref, o_ref, wgb_ref, wub_ref):
    _cast_weight(wg_ref, wgb_ref)
    _cast_weight(wu_ref, wub_ref)
    a = a_ref[...]
    o_ref[...] = (_silu(_dot(a, wgb_ref[...])) * _dot(a, wub_ref[...])).astype(o_ref.dtype)


def mm_swiglu(a, wg, wu, layer, tm, tn):
    return pl.pallas_call(
        _mm_swiglu_kernel,
        grid=(D_FF // tn, T // tm),
        in_specs=[
            pl.BlockSpec((tm, D), lambda n, m: (m, 0)),
            pl.BlockSpec((None, D, tn), lambda n, m: (layer, 0, n)),
            pl.BlockSpec((None, D, tn), lambda n, m: (layer, 0, n)),
        ],
        out_specs=pl.BlockSpec((tm, tn), lambda n, m: (m, n)),
        out_shape=jax.ShapeDtypeStruct((T, D_FF), BF16),
        scratch_shapes=[pltpu.VMEM((D, tn), BF16), pltpu.VMEM((D, tn), BF16)],
        compiler_params=_params("arbitrary", "arbitrary"),
        name="mm_swiglu",
    )(a, wg, wu)


def _log_sigmoid(v):
    return jnp.minimum(v, 0.0) - jnp.log(1.0 + jnp.exp(-jnp.abs(v)))


def _gla_gate_kernel(h_ref, wz_ref, wf_ref, bf_ref, wb_ref, bb_ref, gf_ref, gb_ref):
    z = _dot(h_ref[...], wz_ref[...].astype(BF16))
    af = _dot_hi_lo(z, wf_ref[...]) + bf_ref[...]
    ab = _dot_hi_lo(z, wb_ref[...]) + bb_ref[...]
    gf = _log_sigmoid(af) / GLA_TAU
    gb = _log_sigmoid(ab) / GLA_TAU
    c = GLA_C
    ri = lax.broadcasted_iota(jnp.int32, (c, c), 0)
    ci = lax.broadcasted_iota(jnp.int32, (c, c), 1)
    lower = (ci <= ri).astype(BF16)
    upper = (ci >= ri).astype(BF16)

    def terms(v):
        hi = v.astype(BF16)
        rest = v - hi.astype(F32)
        mid = rest.astype(BF16)
        return hi, mid, (rest - mid.astype(F32)).astype(BF16)

    gf3, gb3 = terms(gf), terms(gb)
    for ch in range(h_ref.shape[0] // c):
        rows = slice(ch * c, (ch + 1) * c)
        gf_ref[rows, :] = _dot(lower, gf3[0][rows]) + _dot(lower, gf3[1][rows]) + _dot(lower, gf3[2][rows])
        gb_ref[rows, :] = _dot(upper, gb3[0][rows]) + _dot(upper, gb3[1][rows]) + _dot(upper, gb3[2][rows])


def gla_gates(h, wz_pad, wf_pad, bg_f, wb_pad, bg_b, j, tm):
    row = lambda i: (i, 0)
    return pl.pallas_call(
        _gla_gate_kernel,
        grid=(T // tm,),
        in_specs=[
            pl.BlockSpec((tm, D), row),
            pl.BlockSpec((D, LANES), lambda i: (0, 0)),
            pl.BlockSpec((LANES, GLA_QK), lambda i: (0, 0)),
            pl.BlockSpec((None, 1, GLA_QK), lambda i: (j, 0, 0)),
            pl.BlockSpec((LANES, GLA_QK), lambda i: (0, 0)),
            pl.BlockSpec((None, 1, GLA_QK), lambda i: (j, 0, 0)),
        ],
        out_specs=[pl.BlockSpec((tm, GLA_QK), row), pl.BlockSpec((tm, GLA_QK), row)],
        out_shape=[jax.ShapeDtypeStruct((T, GLA_QK), F32)] * 2,
        compiler_params=_params("arbitrary"),
        name="gla_gates",
    )(h, wz_pad, wf_pad, bg_f.reshape(-1, 1, GLA_QK), wb_pad, bg_b.reshape(-1, 1, GLA_QK))


def _gla_intra_direct(q, k, bc, rev):
    c = GLA_C
    lane8 = lax.broadcasted_iota(jnp.int32, (8, c), 1)
    srow8 = lax.broadcasted_iota(jnp.int32, (8, 1), 0)
    blocks = []
    for r0 in range(0, c, 8):
        q_r, b_r = q[r0:r0 + 8], bc[r0:r0 + 8]
        a_r = jnp.zeros((8, c), F32)
        for jj in (range(r0, c) if rev else range(r0 + 8)):
            t = q_r * k[jj:jj + 1] * jnp.exp(b_r - bc[jj:jj + 1])
            s = jnp.sum(t, axis=-1, keepdims=True)
            seen = (srow8 + r0 <= jj) if rev else (srow8 + r0 >= jj)
            a_r = jnp.where(lane8 == jj, jnp.where(seen, s, 0.0), a_r)
        blocks.append(a_r)
    return jnp.concatenate(blocks, axis=0).astype(BF16)


def _gla_chunks(streams):
    c = GLA_C
    ri = lax.broadcasted_iota(jnp.int32, (c, c), 0)
    ci = lax.broadcasted_iota(jnp.int32, (c, c), 1)
    bcs = [b for (q, k, v, b, st, rev) in streams]

    partial = []
    for (q, k, v, g, st, rev), bc in zip(streams, bcs):
        b_end = bc[0:1] if rev else bc[c - 1:c]
        qe = (q * jnp.exp(bc)).astype(BF16)
        st_new = st * jnp.exp(b_end) + _dot_tn(v, (k * jnp.exp(b_end - bc)).astype(BF16))
        partial.append((st.astype(BF16), st_new, qe, b_end))

    scores = []
    for (q, k, v, g, st, rev), bc, (_, _, qe, _) in zip(streams, bcs, partial):
        a = _dot_nt(qe, (k * jnp.exp(jnp.minimum(-bc, GLA_SPLIT_DECAY_MAX))).astype(BF16))
        scores.append(jnp.where(ci >= ri if rev else ci <= ri, a, 0.0).astype(BF16))

    out = []
    for (q, k, v, g, st, rev), (st_b, st_new, qe, _), a in zip(streams, partial, scores):
        o_state = _dot_nt(qe, st_b)
        out.append((o_state + _dot(a, v), st_new, o_state))
    total_decay = functools.reduce(jnp.maximum, [jnp.max(-b_end) for (_, _, _, b_end) in partial])
    return out, total_decay


def _gla_direct_output(q, k, v, bc, o_state, rev):
    a = _gla_intra_direct(q, k, bc, rev).astype(F32)
    vf = v.astype(F32)
    o = o_state
    for jj in range(GLA_C):
        o = o + a[:, jj:jj + 1] * vf[jj:jj + 1, :]
    return o


def _gla_kernel(q_ref, k_ref, v_ref, r_ref, gf_ref, gb_ref, ng_ref, y_ref, stf_ref, stb_ref, of_ref, ob_ref):
    nch = ROWS // GLA_C
    nctx = CTX // GLA_C

    def chunk_rows(ch):
        return pl.ds(pl.multiple_of(ch * GLA_C, GLA_C), GLA_C)

    def stream(rows, g_ref, st_ref, rev):
        return q_ref[rows, :], k_ref[rows, :], v_ref[rows, :], g_ref[rows, :], st_ref[...], rev

    stf_ref[...] = jnp.zeros_like(stf_ref)
    stb_ref[...] = jnp.zeros_like(stb_ref)

    def scan(s, carry):
        rows_f = chunk_rows(s)
        rows_b = chunk_rows(jnp.where(s < nctx, nctx - 1 - s, nch + nctx - 1 - s))
        s_f, s_b = stream(rows_f, gf_ref, stf_ref, False), stream(rows_b, gb_ref, stb_ref, True)
        ((o_f, st_f, os_f), (o_b, st_b, os_b)), total_decay = _gla_chunks([s_f, s_b])
        of_ref[rows_f, :] = o_f
        ob_ref[rows_b, :] = o_b
        stf_ref[...] = st_f
        stb_ref[...] = st_b

        @pl.when(total_decay > GLA_SPLIT_DECAY_MAX)
        def _():
            of_ref[rows_f, :] = _gla_direct_output(s_f[0], s_f[1], s_f[2], s_f[3], os_f, False)
            ob_ref[rows_b, :] = _gla_direct_output(s_b[0], s_b[1], s_b[2], s_b[3], os_b, True)

        return carry

    lax.fori_loop(0, nch, scan, 0)

    def readout(ch, carry):
        rows = chunk_rows(ch)
        o = of_ref[rows, :] + ob_ref[rows, :]
        on = o * lax.rsqrt(jnp.mean(o * o, axis=-1, keepdims=True) + EPS) * ng_ref[...]
        y_ref[rows, :] = (on * _silu(r_ref[rows, :].astype(F32))).astype(y_ref.dtype)
        return carry

    lax.fori_loop(0, nch, readout, 0)


def gla_scan(qk, vr, gf, gb, norm_g, j):
    return pl.pallas_call(
        _gla_kernel,
        grid=(BATCH, GLA_H),
        in_specs=[
            pl.BlockSpec((ROWS, GLA_DK), lambda b, h: (b, h)),
            pl.BlockSpec((ROWS, GLA_DK), lambda b, h: (b, GLA_H + h)),
            pl.BlockSpec((ROWS, GLA_DV), lambda b, h: (b, h)),
            pl.BlockSpec((ROWS, GLA_DV), lambda b, h: (b, GLA_H + h)),
            pl.BlockSpec((ROWS, GLA_DK), lambda b, h: (b, h)),
            pl.BlockSpec((ROWS, GLA_DK), lambda b, h: (b, h)),
            pl.BlockSpec((None, 1, GLA_DV), lambda b, h: (j, 0, 0)),
        ],
        out_specs=pl.BlockSpec((ROWS, GLA_DV), lambda b, h: (b, h)),
        out_shape=jax.ShapeDtypeStruct((T, D), BF16),
        scratch_shapes=[pltpu.VMEM((GLA_DV, GLA_DK), F32), pltpu.VMEM((GLA_DV, GLA_DK), F32),
                        pltpu.VMEM((ROWS, GLA_DV), F32), pltpu.VMEM((ROWS, GLA_DV), F32)],
        compiler_params=_params("arbitrary", "arbitrary"),
        name="gla_scan",
    )(qk, qk, vr, vr, gf, gb, norm_g.reshape(-1, 1, GLA_DV))


def rope_tables():
    half = GLA_DK // 2
    freqs = ROPE_BASE ** (-np.arange(0, half, 2, dtype=np.float32) / half)
    t = np.arange(SEQ)
    pos = np.stack([t // GRID_W, t % GRID_W], axis=-1).astype(np.float32)
    ang = pos[:, :, None] * freqs
    cos, sin = np.cos(ang), np.sin(ang)
    cos_t = np.concatenate([cos, cos], axis=-1).reshape(SEQ, GLA_DK)
    sin_t = np.concatenate([-sin, sin], axis=-1).reshape(SEQ, GLA_DK)
    cos_t = np.concatenate([np.ones((CTX, GLA_DK), np.float32), cos_t], axis=0)
    sin_t = np.concatenate([np.zeros((CTX, GLA_DK), np.float32), sin_t], axis=0)
    return jnp.asarray(cos_t, F32), jnp.asarray(sin_t, F32)


def _nat_kernel(q_ref, k_ref, v_ref, bias_ref, y_ref, *, need_ctx):
    kc = k_ref[0:CTX, :]
    vc = v_ref[0:CTX, :]
    if need_ctx:
        s = _dot_nt(q_ref[0:CTX, :], kc)
        p = jnp.exp(s - jnp.max(s, axis=-1, keepdims=True))
        o = _dot(p.astype(BF16), vc) / jnp.sum(p, axis=-1, keepdims=True)
        y_ref[0:CTX, :] = o.astype(y_ref.dtype)
    else:
        y_ref[0:CTX, :] = jnp.zeros((CTX, NAT_DH), y_ref.dtype)
    n_keys = WIN_R * GRID_W

    def scores(r):
        rs = jnp.clip(r - WIN_R // 2, 0, GRID_H - WIN_R)
        q0 = pl.multiple_of(CTX + r * GRID_W, GRID_W)
        k0 = pl.multiple_of(CTX + rs * GRID_W, GRID_W)
        q = q_ref[pl.ds(q0, GRID_W), :]
        s_lat = _dot_nt(q, k_ref[pl.ds(k0, n_keys), :]) + bias_ref[r - rs]
        s_ctx = _dot_nt(q, kc)
        return q0, k0, s_lat, s_ctx

    def probs(q0, k0, s_lat, s_ctx):
        m = jnp.maximum(jnp.max(s_lat, axis=-1, keepdims=True), jnp.max(s_ctx, axis=-1, keepdims=True))
        p_lat = jnp.exp(s_lat - m)
        p_ctx = jnp.exp(s_ctx - m)
        denom = jnp.sum(p_lat, axis=-1, keepdims=True) + jnp.sum(p_ctx, axis=-1, keepdims=True)
        return q0, k0, p_lat.astype(BF16), p_ctx.astype(BF16), denom

    def values(q0, k0, p_lat, p_ctx, denom):
        o = _dot(p_lat, v_ref[pl.ds(k0, n_keys), :]) + _dot(p_ctx, vc)
        return q0, (o / denom).astype(y_ref.dtype)

    def body(it, carry):
        rows = [it * NAT_ROWS_PER_STEP + u for u in range(NAT_ROWS_PER_STEP)]
        outs = [values(*pr) for pr in [probs(*sc) for sc in [scores(r) for r in rows]]]
        for q0, o in outs:
            y_ref[pl.ds(q0, GRID_W), :] = o
        return carry

    lax.fori_loop(0, GRID_H // NAT_ROWS_PER_STEP, body, 0)


def nat_attention(qkv, bias_tbl, need_ctx):
    return pl.pallas_call(
        functools.partial(_nat_kernel, need_ctx=need_ctx),
        grid=(NAT_H, BATCH),
        in_specs=[
            pl.BlockSpec((ROWS, NAT_DH), lambda h, b: (b, h)),
            pl.BlockSpec((ROWS, NAT_DH), lambda h, b: (b, NAT_H + h)),
            pl.BlockSpec((ROWS, NAT_DH), lambda h, b: (b, 2 * NAT_H + h)),
            pl.BlockSpec((None, WIN_R, GRID_W, WIN_R * GRID_W), lambda h, b: (h, 0, 0, 0)),
        ],
        out_specs=pl.BlockSpec((ROWS, NAT_DH), lambda h, b: (b, h)),
        out_shape=jax.ShapeDtypeStruct((T, D), BF16),
        compiler_params=_params("arbitrary", "arbitrary"),
        name="nat_attention",
    )(qkv, qkv, qkv, bias_tbl)


def nat_bias_table(rpb):
    qc = np.arange(GRID_W)[:, None]
    kcol = np.arange(GRID_W)[None, :]
    start = np.clip(qc - WIN_C // 2, 0, GRID_W - WIN_C)
    in_win = (kcol >= start) & (kcol < start + WIN_C)
    dc = np.clip(kcol - qc + WIN_C - 1, 0, 2 * WIN_C - 2)
    pick = jnp.asarray(dc[None] == np.arange(2 * WIN_C - 1)[:, None, None], F32)
    by_col = jnp.einsum("hdc,cqk->hqdk", rpb, pick, precision=HIGHEST)
    by_col = jnp.where(in_win[None, :, None, :], by_col, -jnp.inf)
    tbl = jnp.stack([by_col[:, :, WIN_R - 1 - off:2 * WIN_R - 1 - off, :] for off in range(WIN_R)], axis=1)
    return tbl.reshape(NAT_H, WIN_R, GRID_W, WIN_R * GRID_W)


def _pair_slots(e0_ref, e1_ref, r0_ref, r1_ref, start_ref, t):
    return start_ref[e0_ref[t]] + r0_ref[t], start_ref[e1_ref[t]] + r1_ref[t]


def _is_routed_tile(i, latent_only):
    return (i % (ROWS // TG) >= CTX // TG) if latent_only else None


def _dispatch_kernel(e0_ref, e1_ref, r0_ref, r1_ref, start_ref, part_ref, h_ref, o_ref, zero_ref, sem, *,
                     latent_only):
    i = pl.program_id(0)
    base = i * TG

    @pl.when(i == 0)
    def _():
        zero_ref[...] = jnp.zeros_like(zero_ref)

        def zero_copy(p):
            return pltpu.make_async_copy(zero_ref, o_ref.at[pl.ds(pl.multiple_of(p * TM_PART, TM_PART), TM_PART)],
                                         sem.at[2])

        def start(p, carry):
            pl.when(part_ref[p] < TM_PART)(lambda: zero_copy(p).start())
            return carry

        def wait(p, carry):
            pl.when(part_ref[p] < TM_PART)(lambda: zero_copy(p).wait())
            return carry

        n_parts = o_ref.shape[0] // TM_PART
        lax.fori_loop(0, n_parts, start, 0)
        lax.fori_loop(0, n_parts, wait, 0)

    def start_rows(r, carry):
        row = h_ref.at[pl.ds(r, 1)]
        s0, s1 = _pair_slots(e0_ref, e1_ref, r0_ref, r1_ref, start_ref, base + r)
        pltpu.make_async_copy(row, o_ref.at[pl.ds(s0, 1)], sem.at[0]).start()
        pltpu.make_async_copy(row, o_ref.at[pl.ds(s1, 1)], sem.at[1]).start()
        return carry

    def run():
        lax.fori_loop(0, TG, start_rows, 0, unroll=DMA_UNROLL)
        for s in range(2):
            pltpu.make_async_copy(h_ref, o_ref.at[pl.ds(0, TG)], sem.at[s]).wait()

    routed = _is_routed_tile(i, latent_only)
    if routed is None:
        run()
    else:
        pl.when(routed)(run)


def moe_dispatch(routing, part_rows, h_words, n_slots, latent_only):
    words = h_words.shape[1]
    return pl.pallas_call(
        functools.partial(_dispatch_kernel, latent_only=latent_only),
        grid_spec=pltpu.PrefetchScalarGridSpec(
            num_scalar_prefetch=6,
            grid=(T // TG,),
            in_specs=[pl.BlockSpec((TG, words), lambda i, *_: (i, 0))],
            out_specs=pl.BlockSpec(memory_space=pl.ANY),
            scratch_shapes=[pltpu.VMEM((TM_PART, words), jnp.uint32), pltpu.SemaphoreType.DMA((3,))],
        ),
        out_shape=jax.ShapeDtypeStruct((n_slots, words), jnp.uint32),
        compiler_params=_params("arbitrary"),
        name="moe_dispatch",
    )(*routing, part_rows, h_words)


def _expert_changed(te_ref, i):
    return (i == 0) | (te_ref[i] != te_ref[jnp.maximum(i - 1, 0)])


def _on_occupied_prefix(n_rows, tile, o_ref, compute):
    for parts in range(tile // TM_PART + 1):
        p = parts * TM_PART

        @pl.when((n_rows > p - TM_PART) & (n_rows <= p))
        def _():
            if p > 0:
                o_ref[0:p, :] = compute(p)
            if p < tile:
                o_ref[p:tile, :] = jnp.zeros((tile - p, o_ref.shape[1]), o_ref.dtype)


def _moe_up_kernel(te_ref, tr_ref, ts_ref, to_ref, a_ref, wg_ref, wu_ref, o_ref, wgb_ref, wub_ref):
    del ts_ref, to_ref
    i = pl.program_id(1)

    @pl.when(_expert_changed(te_ref, i))
    def _():
        wgb_ref[...] = wg_ref[...].astype(BF16)
        wub_ref[...] = wu_ref[...].astype(BF16)

    half = D // 2

    def compute(p):
        lo, hi = _unpack_words(a_ref[0:p, :])
        gate = _dot(lo, wgb_ref[0:half, :]) + _dot(hi, wgb_ref[half:D, :])
        up = _dot(lo, wub_ref[0:half, :]) + _dot(hi, wub_ref[half:D, :])
        return (_silu(gate) * up).astype(o_ref.dtype)

    _on_occupied_prefix(tr_ref[i], TM_UP, o_ref, compute)


def moe_up(tiles, xw, wg, wu, j):
    n_slots = xw.shape[0]
    w_spec = pl.BlockSpec((None, None, D, TF_MOE), lambda f, i, te, tr, ts, to: (j, te[i], 0, f))
    return pl.pallas_call(
        _moe_up_kernel,
        grid_spec=pltpu.PrefetchScalarGridSpec(
            num_scalar_prefetch=4,
            grid=(D_FF // TF_MOE, n_slots // TM_UP),
            in_specs=[pl.BlockSpec((TM_UP, D // 2), lambda f, i, te, tr, ts, to: (ts[i], 0)), w_spec, w_spec],
            out_specs=pl.BlockSpec((TM_UP, TF_MOE), lambda f, i, te, tr, ts, to: (to[i], f)),
            scratch_shapes=[pltpu.VMEM((D, TF_MOE), BF16), pltpu.VMEM((D, TF_MOE), BF16)],
        ),
        out_shape=jax.ShapeDtypeStruct((n_slots, D_FF), BF16),
        compiler_params=_params("arbitrary", "arbitrary"),
        name="moe_up",
    )(*tiles, xw, wg, wu)


def _moe_down_kernel(te_ref, tr_ref, ts_ref, to_ref, a_ref, w_ref, o_ref, wb_ref):
    del ts_ref, to_ref
    i = pl.program_id(1)

    @pl.when(_expert_changed(te_ref, i))
    def _():
        wb_ref[...] = w_ref[...].astype(BF16)

    _on_occupied_prefix(tr_ref[i], TM_DOWN, o_ref, lambda p: _dot(a_ref[0:p, :], wb_ref[...]))


def moe_down(tiles, act, wd, j):
    n_slots = act.shape[0]
    tn = TN_MOE_DOWN
    return pl.pallas_call(
        _moe_down_kernel,
        grid_spec=pltpu.PrefetchScalarGridSpec(
            num_scalar_prefetch=4,
            grid=(D // tn, n_slots // TM_DOWN),
            in_specs=[
                pl.BlockSpec((TM_DOWN, D_FF), lambda n, i, te, tr, ts, to: (ts[i], 0)),
                pl.BlockSpec((None, None, D_FF, tn), lambda n, i, te, tr, ts, to: (j, te[i], 0, n),
                             pipeline_mode=pl.Buffered(1)),
            ],
            out_specs=pl.BlockSpec((TM_DOWN, tn), lambda n, i, te, tr, ts, to: (to[i], n)),
            scratch_shapes=[pltpu.VMEM((D_FF, tn), BF16)],
        ),
        out_shape=jax.ShapeDtypeStruct((n_slots, D), F32),
        compiler_params=_params("arbitrary", "arbitrary"),
        name="moe_down",
    )(*tiles, act, wd)


def _combine_kernel(e0_ref, e1_ref, r0_ref, r1_ref, start_ref, ys_ref, x_ref, gt_ref, w_ref, fg_ref, o_ref,
                    b0_ref, b1_ref, sem, *, latent_only, final):
    i = pl.program_id(0)
    base = i * TG

    def start_rows(r, carry):
        s0, s1 = _pair_slots(e0_ref, e1_ref, r0_ref, r1_ref, start_ref, base + r)
        pltpu.make_async_copy(ys_ref.at[pl.ds(s0, 1)], b0_ref.at[pl.ds(r, 1)], sem.at[0]).start()
        pltpu.make_async_copy(ys_ref.at[pl.ds(s1, 1)], b1_ref.at[pl.ds(r, 1)], sem.at[1]).start()
        return carry

    def run():
        lax.fori_loop(0, TG, start_rows, 0, unroll=DMA_UNROLL)
        pltpu.make_async_copy(ys_ref.at[pl.ds(0, TG)], b0_ref, sem.at[0]).wait()
        pltpu.make_async_copy(ys_ref.at[pl.ds(0, TG)], b1_ref, sem.at[1]).wait()
        mix = w_ref[:, 0:1] * b0_ref[...] + w_ref[:, 1:2] * b1_ref[...]
        out = x_ref[...] + _row_gate(gt_ref, i, TG) * mix
        if final:
            out = out * lax.rsqrt(jnp.mean(out * out, axis=-1, keepdims=True) + EPS) * fg_ref[...]
        o_ref[...] = out

    routed = _is_routed_tile(i, latent_only)
    if routed is None:
        run()
    else:
        pl.when(routed)(run)
        if not final:
            @pl.when(jnp.logical_not(routed))
            def _():
                o_ref[...] = x_ref[...]


def moe_combine(routing, ys, x, mods, layer, gate_col, wcol, latent_only, final_g=None):
    final = final_g is not None
    assert latent_only or not final
    per_batch, n_ctx, n_lat = ROWS // TG, CTX // TG, SEQ // TG
    if final:
        out_rows = BATCH * SEQ
        out_map = lambda i, *_: ((i // per_batch) * n_lat + jnp.maximum(i % per_batch - n_ctx, 0), 0)
    else:
        out_rows = T
        out_map = lambda i, *_: (i, 0)
        final_g = jnp.ones((D,), F32)
    return pl.pallas_call(
        functools.partial(_combine_kernel, latent_only=latent_only, final=final),
        grid_spec=pltpu.PrefetchScalarGridSpec(
            num_scalar_prefetch=5,
            grid=(T // TG,),
            in_specs=[
                pl.BlockSpec(memory_space=pl.ANY),
                pl.BlockSpec((TG, D), lambda i, *_: (i, 0)),
                pl.BlockSpec((None, N_MOD, D), lambda i, *_: (layer, 0, gate_col)),
                pl.BlockSpec((TG, LANES), lambda i, *_: (i, 0)),
                pl.BlockSpec((1, D), lambda i, *_: (0, 0)),
            ],
            out_specs=pl.BlockSpec((TG, D), out_map),
            scratch_shapes=[pltpu.VMEM((TG, D), F32), pltpu.VMEM((TG, D), F32), pltpu.SemaphoreType.DMA((2,))],
        ),
        out_shape=jax.ShapeDtypeStruct((out_rows, D), F32),
        compiler_params=_params("arbitrary"),
        name="moe_combine",
    )(*routing, ys, x, mods, wcol, final_g.reshape(1, D))


def moe_slots(n_tokens):
    return (2 * n_tokens // TM_UP + N_EXP) * TM_UP


def moe_plan(counts, n_slots):
    group = (counts + TM_UP - 1) // TM_UP * TM_UP
    start = jnp.cumsum(group) - group
    end = start + counts
    last_row = jnp.max(jnp.where(counts > 0, end, 0)) - 1

    def of_expert(table, expert):
        return jnp.sum(jnp.where(expert[:, None] == jnp.arange(N_EXP), table, 0), axis=1)

    def occupied_rows(tm):
        ids = jnp.arange(n_slots // tm)
        expert = jnp.sum((start + group)[None, :] <= jnp.minimum(ids * tm, last_row)[:, None], axis=1)
        return ids, expert, jnp.clip(of_expert(end, expert) - ids * tm, 0, tm)

    def tiles(tm):
        ids, expert, rows = occupied_rows(tm)
        first = of_expert(start, expert) // tm
        n_occ = (of_expert(counts, expert) + tm - 1) // tm
        k = ids - first
        tile = jnp.where(k < n_occ, first + (k - 1) % jnp.maximum(n_occ, 1), ids)
        rows_v = jnp.clip(of_expert(end, expert) - tile * tm, 0, tm)
        fetched = jnp.where(rows_v > 0, ids, 0)
        last_fetch = jnp.max(jnp.where(ids[None, :] <= ids[:, None], fetched[None, :], 0), axis=1)
        block = jnp.sum(jnp.where(ids[None, :] == last_fetch[:, None], tile[None, :], 0), axis=1)
        return tuple(a.astype(jnp.int32) for a in (expert, rows_v, block, tile))

    return start.astype(jnp.int32), tiles(TM_UP), tiles(TM_DOWN), occupied_rows(TM_PART)[2].astype(jnp.int32)


def _pad_cols(w, n):
    return jnp.pad(w, ((0, 0), (0, n - w.shape[1])))


def kernel(x, c, ctx, c_ctx, ada_w, ada_b, norm_mix_g, norm_ffn_g, gla_w_in, gla_wg_fwd, gla_bg_fwd, gla_wg_bwd, gla_bg_bwd, gla_norm_g, gla_w_out, nat_w_in, nat_rpb, nat_w_out, ffn_w_gate, ffn_w_up, ffn_w_down, moe_router, moe_w_gate, moe_w_up, moe_w_down, final_norm_g):
    c8 = jnp.concatenate([c, c_ctx[None, :], jnp.zeros((N_MOD - BATCH - 1, D), F32)], axis=0)
    mods = ada_tables(c8, ada_w, ada_b)
    cos_t, sin_t = rope_tables()
    router_pad = jnp.pad(moe_router, ((0, 0), (0, 0), (0, LANES - N_EXP)))

    for i in range(DEPTH):
        j = i // 2
        last = i == DEPTH - 1
        if i == 0:
            xs, h = embed_modulate(x, ctx, norm_mix_g, mods, i, 0, 1)
        else:
            h = modulate(xs, norm_mix_g, mods, i, 0, 1)
        if i % 2 == 0:
            qk = mm_rope(h, gla_w_in, j, cos_t, sin_t, tm=1152)
            vr = mm_plain(h, gla_w_in, j, 2 * GLA_QK, 2 * D, BF16, tm=1152, tn=1024)
            z0 = 2 * GLA_QK + 2 * D
            wz_pad = _pad_cols(gla_w_in[j, :, z0:z0 + 2 * GLA_RANK], LANES)
            wf_pad = jnp.pad(gla_wg_fwd[j], ((0, LANES - GLA_RANK), (0, 0)))
            wb_pad = jnp.pad(gla_wg_bwd[j], ((GLA_RANK, LANES - 2 * GLA_RANK), (0, 0)))
            gf, gb = gla_gates(h, wz_pad, wf_pad, gla_bg_fwd, wb_pad, gla_bg_bwd, j, tm=1152)
            y = gla_scan(qk, vr, gf, gb, gla_norm_g, j)
            xs = mm_resid(y, gla_w_out, j, xs, mods, i, 2, tm=1152, tn=1024)
        else:
            qkv = mm_plain(h, nat_w_in, j, 0, 3 * D, BF16, tm=1152, tn=1024,
                           scale=NAT_DH ** -0.5, n_scaled=D // 1024)
            y = nat_attention(qkv, nat_bias_table(nat_rpb[j]), not last)
            xs = mm_resid(y, nat_w_out, j, xs, mods, i, 2, tm=1152, tn=1024)
        if i % 2 == 0:
            h2 = modulate(xs, norm_ffn_g, mods, i, 3, 4)
            act = mm_swiglu(h2, ffn_w_gate, ffn_w_up, j, tm=1152, tn=512)
            xs = mm_resid(act, ffn_w_down, j, xs, mods, i, 5, tm=576, tn=512)
        else:
            h_words, idx, rank, wcol, counts = modulate_route(xs, norm_ffn_g, mods, i, 3, 4, router_pad, j, last)
            n_slots = moe_slots(BATCH * SEQ if last else T)
            start, up_tiles, down_tiles, part_rows = moe_plan(counts[:, 0], n_slots)
            routing = (idx[0], idx[1], rank[0], rank[1], start)
            xw = moe_dispatch(routing, part_rows, h_words, n_slots, last)
            act = moe_up(up_tiles, xw, moe_w_gate, moe_w_up, j)
            ys = moe_down(down_tiles, act, moe_w_down, j)
            xs = moe_combine(routing, ys, xs, mods, i, 5, wcol, last, final_norm_g if last else None)
    return xs.reshape(BATCH, SEQ, D)
```

```python
import functools

import numpy as np
import jax
import jax.numpy as jnp
from jax import lax
from jax.experimental import pallas as pl
from jax.experimental.pallas import tpu as pltpu

F32 = jnp.float32
BF16 = jnp.bfloat16
HIGHEST = lax.Precision.HIGHEST

D = 2048
BATCH = 4
SEQ = 2048
CTX = 256
ROWS = CTX + SEQ
T = BATCH * ROWS
DEPTH = 4
GRID_W = 64
GRID_H = SEQ // GRID_W
EPS = 1e-6
ROPE_BASE = 10000.0
CTX_GROUP = BATCH
N_MOD = 8

GLA_H = 4
GLA_DK = 256
GLA_DV = 512
GLA_QK = GLA_H * GLA_DK
GLA_RANK = 16
GLA_TAU = 16.0
GLA_C = 64
GLA_SPLIT_DECAY_MAX = 60.0

NAT_H = 16
NAT_DH = 128
WIN_R = 8
WIN_C = 16
NAT_ROWS_PER_STEP = 32

D_FF = 5632
N_EXP = 8

LANES = 128
VMEM_LIMIT = 56 * 1024 * 1024

TM_MOD = 256
TM_MODULATE = 768
TM_UP = 1024
TM_DOWN = 512
TM_PART = 128
TF_MOE = 512
TN_MOE_DOWN = 1024
TG = 256
DMA_UNROLL = 16


def _params(*sem):
    return pltpu.CompilerParams(dimension_semantics=sem, vmem_limit_bytes=VMEM_LIMIT)


def _dot(a, b):
    return jnp.dot(a, b, preferred_element_type=F32)


def _dot_nt(a, b):
    return lax.dot_general(a, b, (((1,), (1,)), ((), ())), preferred_element_type=F32)


def _dot_tn(a, b):
    return lax.dot_general(a, b, (((0,), (0,)), ((), ())), preferred_element_type=F32)


def _dot_hi_lo(a, b):
    a_hi, b_hi = a.astype(BF16), b.astype(BF16)
    a_lo = (a - a_hi.astype(F32)).astype(BF16)
    b_lo = (b - b_hi.astype(F32)).astype(BF16)
    return _dot(a_hi, b_hi) + (_dot(a_hi, b_lo) + _dot(a_lo, b_hi))


def _silu(v):
    return v * jax.nn.sigmoid(v)


def _row_gate(tab_ref, m, tm):
    start = m * tm
    b = start // ROWS
    rows = start % ROWS + lax.broadcasted_iota(jnp.int32, (tm, 1), 0)
    per_batch = tab_ref[pl.ds(b, 1), :]
    per_ctx = tab_ref[CTX_GROUP:CTX_GROUP + 1, :]
    return jnp.where(rows < CTX, per_ctx, per_batch)


def _ada_kernel(c_ref, w_ref, b_ref, o_ref):
    a = _silu(c_ref[...]).astype(BF16)
    o_ref[...] = _dot(a, w_ref[...].astype(BF16)) + b_ref[...]


def ada_tables(c8, ada_w, ada_b):
    tn = 1024
    return pl.pallas_call(
        _ada_kernel,
        grid=(DEPTH, 6 * D // tn),
        in_specs=[
            pl.BlockSpec((N_MOD, D), lambda l, n: (0, 0)),
            pl.BlockSpec((None, D, tn), lambda l, n: (l, 0, n)),
            pl.BlockSpec((None, 1, tn), lambda l, n: (l, 0, n)),
        ],
        out_specs=pl.BlockSpec((None, N_MOD, tn), lambda l, n: (l, 0, n)),
        out_shape=jax.ShapeDtypeStruct((DEPTH, N_MOD, 6 * D), F32),
        compiler_params=_params("arbitrary", "arbitrary"),
        name="ada_tables",
    )(c8, ada_w, ada_b.reshape(DEPTH, 1, 6 * D))


def _modulate_rows(x, g_ref, sh_ref, sc_ref):
    tm = x.shape[0]
    y = x * lax.rsqrt(jnp.mean(x * x, axis=-1, keepdims=True) + EPS) * g_ref[...]
    i = pl.program_id(0)
    return y * (1.0 + _row_gate(sc_ref, i, tm)) + _row_gate(sh_ref, i, tm)


def _modulated(x_ref, g_ref, sh_ref, sc_ref):
    return _modulate_rows(x_ref[...], g_ref, sh_ref, sc_ref)


def _embed_modulate_kernel(x_ref, c_ref, g_ref, sh_ref, sc_ref, xs_ref, h_ref):
    is_ctx = pl.program_id(0) % (ROWS // TM_MOD) < CTX // TM_MOD
    rows = jnp.where(is_ctx, c_ref[...], x_ref[...])
    xs_ref[...] = rows
    h_ref[...] = _modulate_rows(rows, g_ref, sh_ref, sc_ref).astype(h_ref.dtype)


def embed_modulate(x, ctx, norm_g, mods, layer, sh_col, sc_col):
    per_batch = ROWS // TM_MOD
    n_ctx = CTX // TM_MOD
    n_lat = SEQ // TM_MOD
    specs = _mod_specs(layer, sh_col, sc_col, TM_MOD)
    return pl.pallas_call(
        _embed_modulate_kernel,
        grid=(T // TM_MOD,),
        in_specs=[
            pl.BlockSpec((TM_MOD, D), lambda i: ((i // per_batch) * n_lat + jnp.maximum(i % per_batch - n_ctx, 0), 0)),
            pl.BlockSpec((TM_MOD, D), lambda i: ((i // per_batch) * n_ctx + jnp.minimum(i % per_batch, n_ctx - 1), 0)),
        ] + specs[1:],
        out_specs=[pl.BlockSpec((TM_MOD, D), lambda i: (i, 0)), pl.BlockSpec((TM_MOD, D), lambda i: (i, 0))],
        out_shape=[jax.ShapeDtypeStruct((T, D), F32), jax.ShapeDtypeStruct((T, D), BF16)],
        compiler_params=_params("arbitrary"),
        name="embed_modulate",
    )(x.reshape(BATCH * SEQ, D), ctx.reshape(BATCH * CTX, D), norm_g.reshape(DEPTH, 1, D), mods, mods)


def _modulate_kernel(x_ref, g_ref, sh_ref, sc_ref, o_ref):
    o_ref[...] = _modulated(x_ref, g_ref, sh_ref, sc_ref).astype(o_ref.dtype)


def _mod_specs(layer, sh_col, sc_col, tm):
    return [
        pl.BlockSpec((tm, D), lambda i: (i, 0)),
        pl.BlockSpec((None, 1, D), lambda i: (layer, 0, 0)),
        pl.BlockSpec((None, N_MOD, D), lambda i: (layer, 0, sh_col)),
        pl.BlockSpec((None, N_MOD, D), lambda i: (layer, 0, sc_col)),
    ]


def modulate(x, norm_g, mods, layer, sh_col, sc_col):
    tm = TM_MODULATE
    return pl.pallas_call(
        _modulate_kernel,
        grid=(T // tm,),
        in_specs=_mod_specs(layer, sh_col, sc_col, tm),
        out_specs=pl.BlockSpec((tm, D), lambda i: (i, 0)),
        out_shape=jax.ShapeDtypeStruct((T, D), BF16),
        compiler_params=_params("arbitrary"),
        name="modulate",
    )(x, norm_g.reshape(DEPTH, 1, D), mods, mods)


def _pack_words(h):
    half = h.shape[1] // 2
    lo = lax.bitcast_convert_type(h[:, :half].astype(BF16).astype(F32), jnp.uint32)
    hi = lax.bitcast_convert_type(h[:, half:].astype(BF16).astype(F32), jnp.uint32)
    return (lo >> 16) | (hi & jnp.uint32(0xFFFF0000))


def _unpack_words(w):
    lo = lax.bitcast_convert_type(w << 16, F32).astype(BF16)
    hi = lax.bitcast_convert_type(w & jnp.uint32(0xFFFF0000), F32).astype(BF16)
    return lo, hi


def _modulate_route_kernel(x_ref, g_ref, sh_ref, sc_ref, rt_ref, hw_ref, idx_ref, rank_ref, wcol_ref, cnt_ref,
                           base_ref, *, latent_only):
    i = pl.program_id(0)

    @pl.when(i == 0)
    def _():
        base_ref[...] = jnp.zeros_like(base_ref)

    h = _modulated(x_ref, g_ref, sh_ref, sc_ref)
    hw_ref[...] = _pack_words(h)
    logits = jnp.dot(h, rt_ref[...], precision=HIGHEST, preferred_element_type=F32)

    lane = lax.broadcasted_iota(jnp.int32, logits.shape, 1)
    lm = jnp.where(lane < N_EXP, logits, -jnp.inf)
    c1 = jnp.max(lm, axis=1, keepdims=True)
    j1 = jnp.min(jnp.where(lm == c1, lane, LANES), axis=1, keepdims=True)
    c2 = jnp.max(jnp.where(lane == j1, -jnp.inf, lm), axis=1, keepdims=True)
    e = jnp.exp(c2 - c1)
    wcol_ref[...] = jnp.where(lane == 0, 1.0 / (1.0 + e), jnp.where(lane == 1, e / (1.0 + e), 0.0))

    lt = logits.T[:N_EXP]
    eid = lax.broadcasted_iota(jnp.int32, lt.shape, 0)
    m1 = jnp.max(lt, axis=0, keepdims=True)
    i1 = jnp.min(jnp.where(lt == m1, eid, N_EXP), axis=0, keepdims=True)
    lt2 = jnp.where(eid == i1, -jnp.inf, lt)
    m2 = jnp.max(lt2, axis=0, keepdims=True)
    i2 = jnp.min(jnp.where(lt2 == m2, eid, N_EXP), axis=0, keepdims=True)
    idx_ref[...] = jnp.concatenate([i1, i2], axis=0)

    routed = 1.0
    if latent_only:
        routed = jnp.where(i % (ROWS // TM_MOD) < CTX // TM_MOD, 0.0, 1.0)
    oh1 = jnp.where(eid == i1, routed, 0.0)
    oh2 = jnp.where(eid == i2, routed, 0.0)
    both = oh1 + oh2
    tj = lax.broadcasted_iota(jnp.int32, (TM_MOD, TM_MOD), 0)
    tt = lax.broadcasted_iota(jnp.int32, (TM_MOD, TM_MOD), 1)
    before = _dot(both.astype(BF16), jnp.where(tj < tt, 1.0, 0.0).astype(BF16))
    seen = base_ref[:, 0:1] + before
    r1 = jnp.sum(oh1 * seen, axis=0, keepdims=True)
    r2 = jnp.sum(oh2 * seen, axis=0, keepdims=True)
    rank_ref[...] = jnp.concatenate([r1, r2], axis=0).astype(jnp.int32)
    total = base_ref[...] + jnp.sum(both, axis=1, keepdims=True)
    base_ref[...] = total
    cnt_ref[...] = total.astype(jnp.int32)


def modulate_route(x, norm_g, mods, layer, sh_col, sc_col, router_pad, j, latent_only):
    pairs = pl.BlockSpec((2, TM_MOD), lambda i: (0, i))
    return pl.pallas_call(
        functools.partial(_modulate_route_kernel, latent_only=latent_only),
        grid=(T // TM_MOD,),
        in_specs=_mod_specs(layer, sh_col, sc_col, TM_MOD) + [
            pl.BlockSpec((None, D, LANES), lambda i: (j, 0, 0)),
        ],
        out_specs=[
            pl.BlockSpec((TM_MOD, D // 2), lambda i: (i, 0)),
            pairs, pairs,
            pl.BlockSpec((TM_MOD, LANES), lambda i: (i, 0)),
            pl.BlockSpec((N_EXP, LANES), lambda i: (0, 0)),
        ],
        out_shape=[
            jax.ShapeDtypeStruct((T, D // 2), jnp.uint32),
            jax.ShapeDtypeStruct((2, T), jnp.int32),
            jax.ShapeDtypeStruct((2, T), jnp.int32),
            jax.ShapeDtypeStruct((T, LANES), F32),
            jax.ShapeDtypeStruct((N_EXP, LANES), jnp.int32),
        ],
        scratch_shapes=[pltpu.VMEM((N_EXP, LANES), F32)],
        compiler_params=_params("arbitrary"),
        name="modulate_route",
    )(x, norm_g.reshape(DEPTH, 1, D), mods, mods, router_pad)


def _cast_weight(w_ref, wb_ref):
    @pl.when(pl.program_id(1) == 0)
    def _():
        wb_ref[...] = w_ref[...].astype(BF16)


def _mm_plain_kernel(a_ref, w_ref, o_ref, wb_ref, *, scale, n_scaled):
    _cast_weight(w_ref, wb_ref)
    acc = _dot(a_ref[...], wb_ref[...])
    if n_scaled:
        acc = acc * jnp.where(pl.program_id(0) < n_scaled, scale, 1.0)
    o_ref[...] = acc.astype(o_ref.dtype)


def mm_plain(a, w, layer, col0, n_out, out_dtype, tm, tn, scale=1.0, n_scaled=0):
    k = a.shape[1]
    c0 = col0 // tn
    return pl.pallas_call(
        functools.partial(_mm_plain_kernel, scale=scale, n_scaled=n_scaled),
        grid=(n_out // tn, T // tm),
        in_specs=[
            pl.BlockSpec((tm, k), lambda n, m: (m, 0)),
            pl.BlockSpec((None, k, tn), lambda n, m: (layer, 0, n + c0)),
        ],
        out_specs=pl.BlockSpec((tm, tn), lambda n, m: (m, n)),
        out_shape=jax.ShapeDtypeStruct((T, n_out), out_dtype),
        scratch_shapes=[pltpu.VMEM((k, tn), BF16)],
        compiler_params=_params("arbitrary", "arbitrary"),
        name="mm_plain",
    )(a, w)


def _mm_rope_kernel(a_ref, w_ref, cos_ref, sin_ref, o_ref, wb_ref, *, tn):
    _cast_weight(w_ref, wb_ref)
    acc = _dot(a_ref[...], wb_ref[...])
    scale = jnp.where(pl.program_id(0) == 0, GLA_DK ** -0.5, 1.0)
    for s in range(tn // LANES):
        xs = acc[:, s * LANES:(s + 1) * LANES]
        t = (s % 2) * LANES
        rot = xs * cos_ref[:, t:t + LANES] + pltpu.roll(xs, LANES // 2, 1) * sin_ref[:, t:t + LANES]
        o_ref[:, s * LANES:(s + 1) * LANES] = rot * scale


def mm_rope(a, w, layer, cos_t, sin_t, tm):
    tn = GLA_QK
    per_batch = ROWS // tm
    return pl.pallas_call(
        functools.partial(_mm_rope_kernel, tn=tn),
        grid=(2, T // tm),
        in_specs=[
            pl.BlockSpec((tm, D), lambda n, m: (m, 0)),
            pl.BlockSpec((None, D, tn), lambda n, m: (layer, 0, n)),
            pl.BlockSpec((tm, GLA_DK), lambda n, m: (m % per_batch, 0)),
            pl.BlockSpec((tm, GLA_DK), lambda n, m: (m % per_batch, 0)),
        ],
        out_specs=pl.BlockSpec((tm, tn), lambda n, m: (m, n)),
        out_shape=jax.ShapeDtypeStruct((T, 2 * GLA_QK), F32),
        scratch_shapes=[pltpu.VMEM((D, tn), BF16)],
        compiler_params=_params("arbitrary", "arbitrary"),
        name="mm_rope",
    )(a, w, cos_t, sin_t)


def _mm_resid_kernel(a_ref, w_ref, x_ref, gt_ref, o_ref, wb_ref, *, tm):
    _cast_weight(w_ref, wb_ref)
    acc = _dot(a_ref[...], wb_ref[...])
    o_ref[...] = x_ref[...] + _row_gate(gt_ref, pl.program_id(1), tm) * acc


def mm_resid(a, w, layer_w, x, mods, layer, gate_col, tm, tn):
    k = a.shape[1]
    g0 = gate_col * D // tn
    return pl.pallas_call(
        functools.partial(_mm_resid_kernel, tm=tm),
        grid=(D // tn, T // tm),
        in_specs=[
            pl.BlockSpec((tm, k), lambda n, m: (m, 0)),
            pl.BlockSpec((None, k, tn), lambda n, m: (layer_w, 0, n)),
            pl.BlockSpec((tm, tn), lambda n, m: (m, n)),
            pl.BlockSpec((None, N_MOD, tn), lambda n, m: (layer, 0, g0 + n)),
        ],
        out_specs=pl.BlockSpec((tm, tn), lambda n, m: (m, n)),
        out_shape=jax.ShapeDtypeStruct((T, D), F32),
        scratch_shapes=[pltpu.VMEM((k, tn), BF16)],
        compiler_params=_params("arbitrary", "arbitrary"),
        name="mm_resid",
    )(a, w, x, mods)


def _mm_swiglu_kernel(a_ref, wg_ref, wu_ref, o_ref, wgb_ref, wub_ref):
    _cast_weight(wg_ref, wgb_ref)
    _cast_weight(wu_ref, wub_ref)
    a = a_ref[...]
    o_ref[...] = (_silu(_dot(a, wgb_ref[...])) * _dot(a, wub_ref[...])).astype(o_ref.dtype)


def mm_swiglu(a, wg, wu, layer, tm, tn):
    return pl.pallas_call(
        _mm_swiglu_kernel,
        grid=(D_FF // tn, T // tm),
        in_specs=[
            pl.BlockSpec((tm, D), lambda n, m: (m, 0)),
            pl.BlockSpec((None, D, tn), lambda n, m: (layer, 0, n)),
            pl.BlockSpec((None, D, tn), lambda n, m: (layer, 0, n)),
        ],
        out_specs=pl.BlockSpec((tm, tn), lambda n, m: (m, n)),
        out_shape=jax.ShapeDtypeStruct((T, D_FF), BF16),
        scratch_shapes=[pltpu.VMEM((D, tn), BF16), pltpu.VMEM((D, tn), BF16)],
        compiler_params=_params("arbitrary", "arbitrary"),
        name="mm_swiglu",
    )(a, wg, wu)


def _log_sigmoid(v):
    return jnp.minimum(v, 0.0) - jnp.log(1.0 + jnp.exp(-jnp.abs(v)))


def _gla_gate_kernel(h_ref, wz_ref, wf_ref, bf_ref, wb_ref, bb_ref, gf_ref, gb_ref):
    z = _dot(h_ref[...], wz_ref[...].astype(BF16))
    af = _dot_hi_lo(z, wf_ref[...]) + bf_ref[...]
    ab = _dot_hi_lo(z, wb_ref[...]) + bb_ref[...]
    gf = _log_sigmoid(af) / GLA_TAU
    gb = _log_sigmoid(ab) / GLA_TAU
    c = GLA_C
    ri = lax.broadcasted_iota(jnp.int32, (c, c), 0)
    ci = lax.broadcasted_iota(jnp.int32, (c, c), 1)
    lower = (ci <= ri).astype(BF16)
    upper = (ci >= ri).astype(BF16)

    def terms(v):
        hi = v.astype(BF16)
        rest = v - hi.astype(F32)
        mid = rest.astype(BF16)
        return hi, mid, (rest - mid.astype(F32)).astype(BF16)

    gf3, gb3 = terms(gf), terms(gb)
    for ch in range(h_ref.shape[0] // c):
        rows = slice(ch * c, (ch + 1) * c)
        gf_ref[rows, :] = _dot(lower, gf3[0][rows]) + _dot(lower, gf3[1][rows]) + _dot(lower, gf3[2][rows])
        gb_ref[rows, :] = _dot(upper, gb3[0][rows]) + _dot(upper, gb3[1][rows]) + _dot(upper, gb3[2][rows])


def gla_gates(h, wz_pad, wf_pad, bg_f, wb_pad, bg_b, j, tm):
    row = lambda i: (i, 0)
    return pl.pallas_call(
        _gla_gate_kernel,
        grid=(T // tm,),
        in_specs=[
            pl.BlockSpec((tm, D), row),
            pl.BlockSpec((D, LANES), lambda i: (0, 0)),
            pl.BlockSpec((LANES, GLA_QK), lambda i: (0, 0)),
            pl.BlockSpec((None, 1, GLA_QK), lambda i: (j, 0, 0)),
            pl.BlockSpec((LANES, GLA_QK), lambda i: (0, 0)),
            pl.BlockSpec((None, 1, GLA_QK), lambda i: (j, 0, 0)),
        ],
        out_specs=[pl.BlockSpec((tm, GLA_QK), row), pl.BlockSpec((tm, GLA_QK), row)],
        out_shape=[jax.ShapeDtypeStruct((T, GLA_QK), F32)] * 2,
        compiler_params=_params("arbitrary"),
        name="gla_gates",
    )(h, wz_pad, wf_pad, bg_f.reshape(-1, 1, GLA_QK), wb_pad, bg_b.reshape(-1, 1, GLA_QK))


def _gla_intra_direct(q, k, bc, rev):
    c = GLA_C
    lane8 = lax.broadcasted_iota(jnp.int32, (8, c), 1)
    srow8 = lax.broadcasted_iota(jnp.int32, (8, 1), 0)
    blocks = []
    for r0 in range(0, c, 8):
        q_r, b_r = q[r0:r0 + 8], bc[r0:r0 + 8]
        a_r = jnp.zeros((8, c), F32)
        for jj in (range(r0, c) if rev else range(r0 + 8)):
            t = q_r * k[jj:jj + 1] * jnp.exp(b_r - bc[jj:jj + 1])
            s = jnp.sum(t, axis=-1, keepdims=True)
            seen = (srow8 + r0 <= jj) if rev else (srow8 + r0 >= jj)
            a_r = jnp.where(lane8 == jj, jnp.where(seen, s, 0.0), a_r)
        blocks.append(a_r)
    return jnp.concatenate(blocks, axis=0).astype(BF16)


def _gla_chunks(streams):
    c = GLA_C
    ri = lax.broadcasted_iota(jnp.int32, (c, c), 0)
    ci = lax.broadcasted_iota(jnp.int32, (c, c), 1)
    bcs = [b for (q, k, v, b, st, rev) in streams]

    partial = []
    for (q, k, v, g, st, rev), bc in zip(streams, bcs):
        b_end = bc[0:1] if rev else bc[c - 1:c]
        qe = (q * jnp.exp(bc)).astype(BF16)
        st_new = st * jnp.exp(b_end) + _dot_tn(v, (k * jnp.exp(b_end - bc)).astype(BF16))
        partial.append((st.astype(BF16), st_new, qe, b_end))

    scores = []
    for (q, k, v, g, st, rev), bc, (_, _, qe, _) in zip(streams, bcs, partial):
        a = _dot_nt(qe, (k * jnp.exp(jnp.minimum(-bc, GLA_SPLIT_DECAY_MAX))).astype(BF16))
        scores.append(jnp.where(ci >= ri if rev else ci <= ri, a, 0.0).astype(BF16))

    out = []
    for (q, k, v, g, st, rev), (st_b, st_new, qe, _), a in zip(streams, partial, scores):
        o_state = _dot_nt(qe, st_b)
        out.append((o_state + _dot(a, v), st_new, o_state))
    total_decay = functools.reduce(jnp.maximum, [jnp.max(-b_end) for (_, _, _, b_end) in partial])
    return out, total_decay


def _gla_direct_output(q, k, v, bc, o_state, rev):
    a = _gla_intra_direct(q, k, bc, rev).astype(F32)
    vf = v.astype(F32)
    o = o_state
    for jj in range(GLA_C):
        o = o + a[:, jj:jj + 1] * vf[jj:jj + 1, :]
    return o


def _gla_kernel(q_ref, k_ref, v_ref, r_ref, gf_ref, gb_ref, ng_ref, y_ref, stf_ref, stb_ref, of_ref, ob_ref):
    nch = ROWS // GLA_C
    nctx = CTX // GLA_C

    def chunk_rows(ch):
        return pl.ds(pl.multiple_of(ch * GLA_C, GLA_C), GLA_C)

    def stream(rows, g_ref, st_ref, rev):
        return q_ref[rows, :], k_ref[rows, :], v_ref[rows, :], g_ref[rows, :], st_ref[...], rev

    stf_ref[...] = jnp.zeros_like(stf_ref)
    stb_ref[...] = jnp.zeros_like(stb_ref)

    def scan(s, carry):
        rows_f = chunk_rows(s)
        rows_b = chunk_rows(jnp.where(s < nctx, nctx - 1 - s, nch + nctx - 1 - s))
        s_f, s_b = stream(rows_f, gf_ref, stf_ref, False), stream(rows_b, gb_ref, stb_ref, True)
        ((o_f, st_f, os_f), (o_b, st_b, os_b)), total_decay = _gla_chunks([s_f, s_b])
        of_ref[rows_f, :] = o_f
        ob_ref[rows_b, :] = o_b
        stf_ref[...] = st_f
        stb_ref[...] = st_b

        @pl.when(total_decay > GLA_SPLIT_DECAY_MAX)
        def _():
            of_ref[rows_f, :] = _gla_direct_output(s_f[0], s_f[1], s_f[2], s_f[3], os_f, False)
            ob_ref[rows_b, :] = _gla_direct_output(s_b[0], s_b[1], s_b[2], s_b[3], os_b, True)

        return carry

    lax.fori_loop(0, nch, scan, 0)

    def readout(ch, carry):
        rows = chunk_rows(ch)
        o = of_ref[rows, :] + ob_ref[rows, :]
        on = o * lax.rsqrt(jnp.mean(o * o, axis=-1, keepdims=True) + EPS) * ng_ref[...]
        y_ref[rows, :] = (on * _silu(r_ref[rows, :].astype(F32))).astype(y_ref.dtype)
        return carry

    lax.fori_loop(0, nch, readout, 0)


def gla_scan(qk, vr, gf, gb, norm_g, j):
    return pl.pallas_call(
        _gla_kernel,
        grid=(BATCH, GLA_H),
        in_specs=[
            pl.BlockSpec((ROWS, GLA_DK), lambda b, h: (b, h)),
            pl.BlockSpec((ROWS, GLA_DK), lambda b, h: (b, GLA_H + h)),
            pl.BlockSpec((ROWS, GLA_DV), lambda b, h: (b, h)),
            pl.BlockSpec((ROWS, GLA_DV), lambda b, h: (b, GLA_H + h)),
            pl.BlockSpec((ROWS, GLA_DK), lambda b, h: (b, h)),
            pl.BlockSpec((ROWS, GLA_DK), lambda b, h: (b, h)),
            pl.BlockSpec((None, 1, GLA_DV), lambda b, h: (j, 0, 0)),
        ],
        out_specs=pl.BlockSpec((ROWS, GLA_DV), lambda b, h: (b, h)),
        out_shape=jax.ShapeDtypeStruct((T, D), BF16),
        scratch_shapes=[pltpu.VMEM((GLA_DV, GLA_DK), F32), pltpu.VMEM((GLA_DV, GLA_DK), F32),
                        pltpu.VMEM((ROWS, GLA_DV), F32), pltpu.VMEM((ROWS, GLA_DV), F32)],
        compiler_params=_params("arbitrary", "arbitrary"),
        name="gla_scan",
    )(qk, qk, vr, vr, gf, gb, norm_g.reshape(-1, 1, GLA_DV))


def rope_tables():
    half = GLA_DK // 2
    freqs = ROPE_BASE ** (-np.arange(0, half, 2, dtype=np.float32) / half)
    t = np.arange(SEQ)
    pos = np.stack([t // GRID_W, t % GRID_W], axis=-1).astype(np.float32)
    ang = pos[:, :, None] * freqs
    cos, sin = np.cos(ang), np.sin(ang)
    cos_t = np.concatenate([cos, cos], axis=-1).reshape(SEQ, GLA_DK)
    sin_t = np.concatenate([-sin, sin], axis=-1).reshape(SEQ, GLA_DK)
    cos_t = np.concatenate([np.ones((CTX, GLA_DK), np.float32), cos_t], axis=0)
    sin_t = np.concatenate([np.zeros((CTX, GLA_DK), np.float32), sin_t], axis=0)
    return jnp.asarray(cos_t, F32), jnp.asarray(sin_t, F32)


def _nat_kernel(q_ref, k_ref, v_ref, bias_ref, y_ref, *, need_ctx):
    kc = k_ref[0:CTX, :]
    vc = v_ref[0:CTX, :]
    if need_ctx:
        s = _dot_nt(q_ref[0:CTX, :], kc)
        p = jnp.exp(s - jnp.max(s, axis=-1, keepdims=True))
        o = _dot(p.astype(BF16), vc) / jnp.sum(p, axis=-1, keepdims=True)
        y_ref[0:CTX, :] = o.astype(y_ref.dtype)
    else:
        y_ref[0:CTX, :] = jnp.zeros((CTX, NAT_DH), y_ref.dtype)
    n_keys = WIN_R * GRID_W

    def scores(r):
        rs = jnp.clip(r - WIN_R // 2, 0, GRID_H - WIN_R)
        q0 = pl.multiple_of(CTX + r * GRID_W, GRID_W)
        k0 = pl.multiple_of(CTX + rs * GRID_W, GRID_W)
        q = q_ref[pl.ds(q0, GRID_W), :]
        s_lat = _dot_nt(q, k_ref[pl.ds(k0, n_keys), :]) + bias_ref[r - rs]
        s_ctx = _dot_nt(q, kc)
        return q0, k0, s_lat, s_ctx

    def probs(q0, k0, s_lat, s_ctx):
        m = jnp.maximum(jnp.max(s_lat, axis=-1, keepdims=True), jnp.max(s_ctx, axis=-1, keepdims=True))
        p_lat = jnp.exp(s_lat - m)
        p_ctx = jnp.exp(s_ctx - m)
        denom = jnp.sum(p_lat, axis=-1, keepdims=True) + jnp.sum(p_ctx, axis=-1, keepdims=True)
        return q0, k0, p_lat.astype(BF16), p_ctx.astype(BF16), denom

    def values(q0, k0, p_lat, p_ctx, denom):
        o = _dot(p_lat, v_ref[pl.ds(k0, n_keys), :]) + _dot(p_ctx, vc)
        return q0, (o / denom).astype(y_ref.dtype)

    def body(it, carry):
        rows = [it * NAT_ROWS_PER_STEP + u for u in range(NAT_ROWS_PER_STEP)]
        outs = [values(*pr) for pr in [probs(*sc) for sc in [scores(r) for r in rows]]]
        for q0, o in outs:
            y_ref[pl.ds(q0, GRID_W), :] = o
        return carry

    lax.fori_loop(0, GRID_H // NAT_ROWS_PER_STEP, body, 0)


def nat_attention(qkv, bias_tbl, need_ctx):
    return pl.pallas_call(
        functools.partial(_nat_kernel, need_ctx=need_ctx),
        grid=(NAT_H, BATCH),
        in_specs=[
            pl.BlockSpec((ROWS, NAT_DH), lambda h, b: (b, h)),
            pl.BlockSpec((ROWS, NAT_DH), lambda h, b: (b, NAT_H + h)),
            pl.BlockSpec((ROWS, NAT_DH), lambda h, b: (b, 2 * NAT_H + h)),
            pl.BlockSpec((None, WIN_R, GRID_W, WIN_R * GRID_W), lambda h, b: (h, 0, 0, 0)),
        ],
        out_specs=pl.BlockSpec((ROWS, NAT_DH), lambda h, b: (b, h)),
        out_shape=jax.ShapeDtypeStruct((T, D), BF16),
        compiler_params=_params("arbitrary", "arbitrary"),
        name="nat_attention",
    )(qkv, qkv, qkv, bias_tbl)


def nat_bias_table(rpb):
    qc = np.arange(GRID_W)[:, None]
    kcol = np.arange(GRID_W)[None, :]
    start = np.clip(qc - WIN_C // 2, 0, GRID_W - WIN_C)
    in_win = (kcol >= start) & (kcol < start + WIN_C)
    dc = np.clip(kcol - qc + WIN_C - 1, 0, 2 * WIN_C - 2)
    pick = jnp.asarray(dc[None] == np.arange(2 * WIN_C - 1)[:, None, None], F32)
    by_col = jnp.einsum("hdc,cqk->hqdk", rpb, pick, precision=HIGHEST)
    by_col = jnp.where(in_win[None, :, None, :], by_col, -jnp.inf)
    tbl = jnp.stack([by_col[:, :, WIN_R - 1 - off:2 * WIN_R - 1 - off, :] for off in range(WIN_R)], axis=1)
    return tbl.reshape(NAT_H, WIN_R, GRID_W, WIN_R * GRID_W)


def _pair_slots(e0_ref, e1_ref, r0_ref, r1_ref, start_ref, t):
    return start_ref[e0_ref[t]] + r0_ref[t], start_ref[e1_ref[t]] + r1_ref[t]


def _is_routed_tile(i, latent_only):
    return (i % (ROWS // TG) >= CTX // TG) if latent_only else None


def _dispatch_kernel(e0_ref, e1_ref, r0_ref, r1_ref, start_ref, part_ref, h_ref, o_ref, zero_ref, sem, *,
                     latent_only):
    i = pl.program_id(0)
    base = i * TG

    @pl.when(i == 0)
    def _():
        zero_ref[...] = jnp.zeros_like(zero_ref)

        def zero_copy(p):
            return pltpu.make_async_copy(zero_ref, o_ref.at[pl.ds(pl.multiple_of(p * TM_PART, TM_PART), TM_PART)],
                                         sem.at[2])

        def start(p, carry):
            pl.when(part_ref[p] < TM_PART)(lambda: zero_copy(p).start())
            return carry

        def wait(p, carry):
            pl.when(part_ref[p] < TM_PART)(lambda: zero_copy(p).wait())
            return carry

        n_parts = o_ref.shape[0] // TM_PART
        lax.fori_loop(0, n_parts, start, 0)
        lax.fori_loop(0, n_parts, wait, 0)

    def start_rows(r, carry):
        row = h_ref.at[pl.ds(r, 1)]
        s0, s1 = _pair_slots(e0_ref, e1_ref, r0_ref, r1_ref, start_ref, base + r)
        pltpu.make_async_copy(row, o_ref.at[pl.ds(s0, 1)], sem.at[0]).start(priority=0)
        pltpu.make_async_copy(row, o_ref.at[pl.ds(s1, 1)], sem.at[1]).start(priority=1)
        return carry

    def run():
        lax.fori_loop(0, TG, start_rows, 0, unroll=DMA_UNROLL)
        for s in range(2):
            pltpu.make_async_copy(h_ref, o_ref.at[pl.ds(0, TG)], sem.at[s]).wait()

    routed = _is_routed_tile(i, latent_only)
    if routed is None:
        run()
    else:
        pl.when(routed)(run)


def moe_dispatch(routing, part_rows, h_words, n_slots, latent_only):
    words = h_words.shape[1]
    return pl.pallas_call(
        functools.partial(_dispatch_kernel, latent_only=latent_only),
        grid_spec=pltpu.PrefetchScalarGridSpec(
            num_scalar_prefetch=6,
            grid=(T // TG,),
            in_specs=[pl.BlockSpec((TG, words), lambda i, *_: (i, 0))],
            out_specs=pl.BlockSpec(memory_space=pl.ANY),
            scratch_shapes=[pltpu.VMEM((TM_PART, words), jnp.uint32), pltpu.SemaphoreType.DMA((3,))],
        ),
        out_shape=jax.ShapeDtypeStruct((n_slots, words), jnp.uint32),
        compiler_params=_params("arbitrary"),
        name="moe_dispatch",
    )(*routing, part_rows, h_words)


def _expert_changed(te_ref, i):
    return (i == 0) | (te_ref[i] != te_ref[jnp.maximum(i - 1, 0)])


def _on_occupied_prefix(n_rows, tile, o_ref, compute):
    for parts in range(tile // TM_PART + 1):
        p = parts * TM_PART

        @pl.when((n_rows > p - TM_PART) & (n_rows <= p))
        def _():
            if p > 0:
                o_ref[0:p, :] = compute(p)
            if p < tile:
                o_ref[p:tile, :] = jnp.zeros((tile - p, o_ref.shape[1]), o_ref.dtype)


def _moe_up_kernel(te_ref, tr_ref, ts_ref, to_ref, a_ref, wg_ref, wu_ref, o_ref, wgb_ref, wub_ref):
    del ts_ref, to_ref
    i = pl.program_id(1)

    @pl.when(_expert_changed(te_ref, i))
    def _():
        wgb_ref[...] = wg_ref[...].astype(BF16)
        wub_ref[...] = wu_ref[...].astype(BF16)

    half = D // 2

    def compute(p):
        lo, hi = _unpack_words(a_ref[0:p, :])
        gate = _dot(lo, wgb_ref[0:half, :]) + _dot(hi, wgb_ref[half:D, :])
        up = _dot(lo, wub_ref[0:half, :]) + _dot(hi, wub_ref[half:D, :])
        return (_silu(gate) * up).astype(o_ref.dtype)

    _on_occupied_prefix(tr_ref[i], TM_UP, o_ref, compute)


def moe_up(tiles, xw, wg, wu, j):
    n_slots = xw.shape[0]
    w_spec = pl.BlockSpec((None, None, D, TF_MOE), lambda f, i, te, tr, ts, to: (j, te[i], 0, f))
    return pl.pallas_call(
        _moe_up_kernel,
        grid_spec=pltpu.PrefetchScalarGridSpec(
            num_scalar_prefetch=4,
            grid=(D_FF // TF_MOE, n_slots // TM_UP),
            in_specs=[pl.BlockSpec((TM_UP, D // 2), lambda f, i, te, tr, ts, to: (ts[i], 0)), w_spec, w_spec],
            out_specs=pl.BlockSpec((TM_UP, TF_MOE), lambda f, i, te, tr, ts, to: (to[i], f)),
            scratch_shapes=[pltpu.VMEM((D, TF_MOE), BF16), pltpu.VMEM((D, TF_MOE), BF16)],
        ),
        out_shape=jax.ShapeDtypeStruct((n_slots, D_FF), BF16),
        compiler_params=_params("arbitrary", "arbitrary"),
        name="moe_up",
    )(*tiles, xw, wg, wu)


def _moe_down_kernel(te_ref, tr_ref, ts_ref, to_ref, a_ref, w_ref, o_ref, wb_ref):
    del ts_ref, to_ref
    i = pl.program_id(1)

    @pl.when(_expert_changed(te_ref, i))
    def _():
        wb_ref[...] = w_ref[...].astype(BF16)

    _on_occupied_prefix(tr_ref[i], TM_DOWN, o_ref, lambda p: _dot(a_ref[0:p, :], wb_ref[...]))


def moe_down(tiles, act, wd, j):
    n_slots = act.shape[0]
    tn = TN_MOE_DOWN
    return pl.pallas_call(
        _moe_down_kernel,
        grid_spec=pltpu.PrefetchScalarGridSpec(
            num_scalar_prefetch=4,
            grid=(D // tn, n_slots // TM_DOWN),
            in_specs=[
                pl.BlockSpec((TM_DOWN, D_FF), lambda n, i, te, tr, ts, to: (ts[i], 0)),
                pl.BlockSpec((None, None, D_FF, tn), lambda n, i, te, tr, ts, to: (j, te[i], 0, n),
                             pipeline_mode=pl.Buffered(1)),
            ],
            out_specs=pl.BlockSpec((TM_DOWN, tn), lambda n, i, te, tr, ts, to: (to[i], n)),
            scratch_shapes=[pltpu.VMEM((D_FF, tn), BF16)],
        ),
        out_shape=jax.ShapeDtypeStruct((n_slots, D), F32),
        compiler_params=_params("arbitrary", "arbitrary"),
        name="moe_down",
    )(*tiles, act, wd)


def _combine_kernel(e0_ref, e1_ref, r0_ref, r1_ref, start_ref, ys_ref, x_ref, gt_ref, w_ref, fg_ref, o_ref,
                    b0_ref, b1_ref, sem, *, latent_only, final):
    i = pl.program_id(0)
    base = i * TG

    def start_rows(r, carry):
        s0, s1 = _pair_slots(e0_ref, e1_ref, r0_ref, r1_ref, start_ref, base + r)
        pltpu.make_async_copy(ys_ref.at[pl.ds(s0, 1)], b0_ref.at[pl.ds(r, 1)], sem.at[0]).start(priority=0)
        pltpu.make_async_copy(ys_ref.at[pl.ds(s1, 1)], b1_ref.at[pl.ds(r, 1)], sem.at[1]).start(priority=1)
        return carry

    def run():
        lax.fori_loop(0, TG, start_rows, 0, unroll=DMA_UNROLL)
        pltpu.make_async_copy(ys_ref.at[pl.ds(0, TG)], b0_ref, sem.at[0]).wait()
        pltpu.make_async_copy(ys_ref.at[pl.ds(0, TG)], b1_ref, sem.at[1]).wait()
        mix = w_ref[:, 0:1] * b0_ref[...] + w_ref[:, 1:2] * b1_ref[...]
        out = x_ref[...] + _row_gate(gt_ref, i, TG) * mix
        if final:
            out = out * lax.rsqrt(jnp.mean(out * out, axis=-1, keepdims=True) + EPS) * fg_ref[...]
        o_ref[...] = out

    routed = _is_routed_tile(i, latent_only)
    if routed is None:
        run()
    else:
        pl.when(routed)(run)
        if not final:
            @pl.when(jnp.logical_not(routed))
            def _():
                o_ref[...] = x_ref[...]


def moe_combine(routing, ys, x, mods, layer, gate_col, wcol, latent_only, final_g=None):
    final = final_g is not None
    assert latent_only or not final
    per_batch, n_ctx, n_lat = ROWS // TG, CTX // TG, SEQ // TG
    if final:
        out_rows = BATCH * SEQ
        out_map = lambda i, *_: ((i // per_batch) * n_lat + jnp.maximum(i % per_batch - n_ctx, 0), 0)
    else:
        out_rows = T
        out_map = lambda i, *_: (i, 0)
        final_g = jnp.ones((D,), F32)
    return pl.pallas_call(
        functools.partial(_combine_kernel, latent_only=latent_only, final=final),
        grid_spec=pltpu.PrefetchScalarGridSpec(
            num_scalar_prefetch=5,
            grid=(T // TG,),
            in_specs=[
                pl.BlockSpec(memory_space=pl.ANY),
                pl.BlockSpec((TG, D), lambda i, *_: (i, 0)),
                pl.BlockSpec((None, N_MOD, D), lambda i, *_: (layer, 0, gate_col)),
                pl.BlockSpec((TG, LANES), lambda i, *_: (i, 0)),
                pl.BlockSpec((1, D), lambda i, *_: (0, 0)),
            ],
            out_specs=pl.BlockSpec((TG, D), out_map),
            scratch_shapes=[pltpu.VMEM((TG, D), F32), pltpu.VMEM((TG, D), F32), pltpu.SemaphoreType.DMA((2,))],
        ),
        out_shape=jax.ShapeDtypeStruct((out_rows, D), F32),
        compiler_params=_params("arbitrary"),
        name="moe_combine",
    )(*routing, ys, x, mods, wcol, final_g.reshape(1, D))


def moe_slots(n_tokens):
    return (2 * n_tokens // TM_UP + N_EXP) * TM_UP


def moe_plan(counts, n_slots):
    group = (counts + TM_UP - 1) // TM_UP * TM_UP
    start = jnp.cumsum(group) - group
    end = start + counts
    last_row = jnp.max(jnp.where(counts > 0, end, 0)) - 1

    def of_expert(table, expert):
        return jnp.sum(jnp.where(expert[:, None] == jnp.arange(N_EXP), table, 0), axis=1)

    def occupied_rows(tm):
        ids = jnp.arange(n_slots // tm)
        expert = jnp.sum((start + group)[None, :] <= jnp.minimum(ids * tm, last_row)[:, None], axis=1)
        return ids, expert, jnp.clip(of_expert(end, expert) - ids * tm, 0, tm)

    def tiles(tm):
        ids, expert, rows = occupied_rows(tm)
        first = of_expert(start, expert) // tm
        n_occ = (of_expert(counts, expert) + tm - 1) // tm
        k = ids - first
        tile = jnp.where(k < n_occ, first + (k - 1) % jnp.maximum(n_occ, 1), ids)
        rows_v = jnp.clip(of_expert(end, expert) - tile * tm, 0, tm)
        fetched = jnp.where(rows_v > 0, ids, 0)
        last_fetch = jnp.max(jnp.where(ids[None, :] <= ids[:, None], fetched[None, :], 0), axis=1)
        block = jnp.sum(jnp.where(ids[None, :] == last_fetch[:, None], tile[None, :], 0), axis=1)
        return tuple(a.astype(jnp.int32) for a in (expert, rows_v, block, tile))

    return start.astype(jnp.int32), tiles(TM_UP), tiles(TM_DOWN), occupied_rows(TM_PART)[2].astype(jnp.int32)


def _pad_cols(w, n):
    return jnp.pad(w, ((0, 0), (0, n - w.shape[1])))


def kernel(x, c, ctx, c_ctx, ada_w, ada_b, norm_mix_g, norm_ffn_g, gla_w_in, gla_wg_fwd, gla_bg_fwd, gla_wg_bwd, gla_bg_bwd, gla_norm_g, gla_w_out, nat_w_in, nat_rpb, nat_w_out, ffn_w_gate, ffn_w_up, ffn_w_down, moe_router, moe_w_gate, moe_w_up, moe_w_down, final_norm_g):
    c8 = jnp.concatenate([c, c_ctx[None, :], jnp.zeros((N_MOD - BATCH - 1, D), F32)], axis=0)
    mods = ada_tables(c8, ada_w, ada_b)
    cos_t, sin_t = rope_tables()
    router_pad = jnp.pad(moe_router, ((0, 0), (0, 0), (0, LANES - N_EXP)))

    for i in range(DEPTH):
        j = i // 2
        last = i == DEPTH - 1
        if i == 0:
            xs, h = embed_modulate(x, ctx, norm_mix_g, mods, i, 0, 1)
        else:
            h = modulate(xs, norm_mix_g, mods, i, 0, 1)
        if i % 2 == 0:
            qk = mm_rope(h, gla_w_in, j, cos_t, sin_t, tm=1152)
            vr = mm_plain(h, gla_w_in, j, 2 * GLA_QK, 2 * D, BF16, tm=1152, tn=1024)
            z0 = 2 * GLA_QK + 2 * D
            wz_pad = _pad_cols(gla_w_in[j, :, z0:z0 + 2 * GLA_RANK], LANES)
            wf_pad = jnp.pad(gla_wg_fwd[j], ((0, LANES - GLA_RANK), (0, 0)))
            wb_pad = jnp.pad(gla_wg_bwd[j], ((GLA_RANK, LANES - 2 * GLA_RANK), (0, 0)))
            gf, gb = gla_gates(h, wz_pad, wf_pad, gla_bg_fwd, wb_pad, gla_bg_bwd, j, tm=1152)
            y = gla_scan(qk, vr, gf, gb, gla_norm_g, j)
            xs = mm_resid(y, gla_w_out, j, xs, mods, i, 2, tm=1152, tn=1024)
        else:
            qkv = mm_plain(h, nat_w_in, j, 0, 3 * D, BF16, tm=1152, tn=1024,
                           scale=NAT_DH ** -0.5, n_scaled=D // 1024)
            y = nat_attention(qkv, nat_bias_table(nat_rpb[j]), not last)
            xs = mm_resid(y, nat_w_out, j, xs, mods, i, 2, tm=1152, tn=1024)
        if i % 2 == 0:
            h2 = modulate(xs, norm_ffn_g, mods, i, 3, 4)
            act = mm_swiglu(h2, ffn_w_gate, ffn_w_up, j, tm=1152, tn=512)
            xs = mm_resid(act, ffn_w_down, j, xs, mods, i, 5, tm=576, tn=512)
        else:
            h_words, idx, rank, wcol, counts = modulate_route(xs, norm_ffn_g, mods, i, 3, 4, router_pad, j, last)
            n_slots = moe_slots(BATCH * SEQ if last else T)
            start, up_tiles, down_tiles, part_rows = moe_plan(counts[:, 0], n_slots)
            routing = (idx[0], idx[1], rank[0], rank[1], start)
            xw = moe_dispatch(routing, part_rows, h_words, n_slots, last)
            act = moe_up(up_tiles, xw, moe_w_gate, moe_w_up, j)
            ys = moe_down(down_tiles, act, moe_w_down, j)
            xs = moe_combine(routing, ys, xs, mods, i, 5, wcol, last, final_norm_g if last else None)
    return xs.reshape(BATCH, SEQ, D)
```
